```python
import math
import jax, jax.numpy as jnp
from jax import lax
import numpy as np

D_MODEL = 2048
BATCH = 2
SEQ = 4096
DEPTH = 2
DEC_BATCH = 128
DEC_SEQ = 4
PAST_LEN = 2048
PAGE_SIZE = 128

D_MIX = D_MODEL
HEAD_DIM = 128
DA_WIDTH = D_MIX // 4
DA_HEADS = DA_WIDTH // HEAD_DIM
DA_QK = HEAD_DIM // 2
DA_V = HEAD_DIM
MB_WIDTH = D_MIX // 4
MB_HEADS = MB_WIDTH // HEAD_DIM
MB_DIM = HEAD_DIM
MOBA_BLOCK = 256
MOBA_TOPK = 3
D_SSM = D_MIX - DA_WIDTH - MB_WIDTH
SSM_HEADDIM = 64
SSM_HEADS = D_SSM // SSM_HEADDIM
SSM_GROUPS = 2
SSM_HPG = SSM_HEADS // SSM_GROUPS
D_STATE = 128
CONV_W = 4
D_XBC = D_SSM + 2 * SSM_GROUPS * D_STATE
SSD_CHUNK = 128
D_IN_PROJ = 2 * DA_HEADS * 2 * DA_QK + DA_HEADS * DA_V + 3 * MB_HEADS * MB_DIM + D_SSM + D_XBC + SSM_HEADS
N_EXPERTS = 16
N_EXPERT_GROUPS = 4
EXPERTS_PER_GROUP = N_EXPERTS // N_EXPERT_GROUPS
TOP_K = 2
D_EXPERT = D_MODEL // 4
D_PLE = 256
Q_BLOCK = 128
ALPHA = (2 * DEPTH) ** 0.25
BETA = (8 * DEPTH) ** -0.25
LN_EPS = 1e-5
RMS_EPS = 1e-6

kernel_name = 'hybrid_diff_moba_ssd_moe_step'

F32 = jnp.float32


def layer_norm(x, w, b):
    xf = x.astype(F32)
    mu = jnp.mean(xf, -1, keepdims=True)
    var = jnp.mean(jnp.square(xf - mu), -1, keepdims=True)
    return ((xf - mu) * lax.rsqrt(var + LN_EPS) * w + b).astype(x.dtype)


def rms_norm(x, w):
    xf = x.astype(F32)
    return (xf * lax.rsqrt(jnp.mean(xf * xf, -1, keepdims=True) + RMS_EPS) * w).astype(x.dtype)


def alibi_slopes():
    n = DA_HEADS + MB_HEADS
    return jnp.asarray(2.0 ** (-8.0 * np.arange(1, n + 1) / n), dtype=F32)


def project(h, w_in_i):
    b, l = h.shape[:2]
    sizes = [DA_HEADS * 2 * DA_QK, DA_HEADS * 2 * DA_QK, DA_HEADS * DA_V,
             MB_HEADS * MB_DIM, MB_HEADS * MB_DIM, MB_HEADS * MB_DIM, D_SSM, D_XBC, SSM_HEADS]
    offsets = np.cumsum(sizes)[:-1].tolist()
    qa, ka, va, qb, kb, vb, zc, xbc, dt_raw = jnp.split(h @ w_in_i, offsets, axis=-1)
    return (qa.reshape(b, l, DA_HEADS, 2 * DA_QK), ka.reshape(b, l, DA_HEADS, 2 * DA_QK),
            va.reshape(b, l, DA_HEADS, DA_V), qb.reshape(b, l, MB_HEADS, MB_DIM),
            kb.reshape(b, l, MB_HEADS, MB_DIM), vb.reshape(b, l, MB_HEADS, MB_DIM), zc, xbc, dt_raw)


def gather_pages(pool, page_table):
    pages = pool[page_table]
    b, n, ps = pages.shape[:3]
    return pages.reshape(b, n * ps, *pages.shape[3:])


def diff_attention(q, k, v, q_pos, k_pos, slopes, lam, lam_init, norm_w):
    b, tq, h, _ = q.shape
    q2 = q.reshape(b, tq, h, 2, DA_QK)
    k2 = k.reshape(b, k.shape[1], h, 2, DA_QK)
    s = jnp.einsum('bqhcd,bkhcd->bhcqk', q2, k2, preferred_element_type=F32) * (DA_QK ** -0.5)
    dist = (q_pos[:, None] - k_pos[None, :]).astype(F32)
    s = s - slopes[None, :, None, None, None] * dist
    s = jnp.where(dist >= 0, s, -jnp.inf)
    p = jax.nn.softmax(s, axis=-1)
    a = p[:, :, 0] - lam * p[:, :, 1]
    o = jnp.einsum('bhqk,bkhd->bqhd', a.astype(v.dtype), v)
    return rms_norm(o, norm_w) * (1.0 - lam_init)


def diff_prompt(q, k, v, slopes, lam, lam_init, norm_w):
    b, s = q.shape[:2]
    nq = s // Q_BLOCK
    qb = q.reshape(b, nq, Q_BLOCK, *q.shape[2:]).swapaxes(0, 1)
    k_pos = jnp.arange(s, dtype=jnp.int32)

    def one(args):
        qi, q0 = args
        q_pos = q0 + jnp.arange(Q_BLOCK, dtype=jnp.int32)
        return diff_attention(qi, k, v, q_pos, k_pos, slopes, lam, lam_init, norm_w)

    o = lax.map(one, (qb, jnp.arange(nq, dtype=jnp.int32) * Q_BLOCK))
    return o.swapaxes(0, 1).reshape(b, s, DA_HEADS, DA_V)


def moba_blocks(k, v):
    b, l, h, d = k.shape
    nb = -(-l // MOBA_BLOCK)
    pad = ((0, 0), (0, nb * MOBA_BLOCK - l), (0, 0), (0, 0))
    kp = jnp.pad(k, pad).reshape(b, nb, MOBA_BLOCK, h, d)
    vp = jnp.pad(v, pad).reshape(b, nb, MOBA_BLOCK, h, d)
    k_mean = jnp.mean(kp.astype(F32), axis=2).astype(k.dtype)
    return kp, vp, k_mean


def moba_query_block(q, kp, vp, k_mean, q0, slopes):
    b, tq, h, d = q.shape
    nb = kp.shape[1]
    n_sel = min(MOBA_TOPK, nb)
    own = q0 // MOBA_BLOCK
    scale = d ** -0.5
    q_pos = q0 + jnp.arange(tq, dtype=jnp.int32)
    blk_pos = jnp.arange(MOBA_BLOCK, dtype=jnp.int32)
    gate = jnp.einsum('bqhd,bnhd->bhqn', q, k_mean, preferred_element_type=F32)
    gate = jnp.where(jnp.arange(nb) < own, gate, -jnp.inf)
    top_val, top_idx = lax.top_k(gate, n_sel)
    valid = jnp.isfinite(top_val)
    k_own = lax.dynamic_index_in_dim(kp, own, axis=1, keepdims=False)
    v_own = lax.dynamic_index_in_dim(vp, own, axis=1, keepdims=False)
    dist_own = (q_pos[:, None] - (own * MOBA_BLOCK + blk_pos)[None, :]).astype(F32)
    s_own = jnp.einsum('bqhd,bkhd->bhqk', q, k_own, preferred_element_type=F32) * scale
    s_own = jnp.where(dist_own >= 0, s_own - slopes[None, :, None, None] * dist_own, -jnp.inf)
    bi = jnp.arange(b)[:, None, None]
    hi = jnp.arange(h)[None, :, None]
    scores = [s_own]
    for j in range(n_sel):
        blk = top_idx[..., j]
        kb = kp[bi, blk, :, hi]
        dist = (q_pos[None, None, :, None] - (blk[..., None] * MOBA_BLOCK + blk_pos)).astype(F32)
        s_j = jnp.einsum('bqhd,bhqkd->bhqk', q, kb, preferred_element_type=F32) * scale
        s_j = s_j - slopes[None, :, None, None] * dist
        scores.append(jnp.where(valid[..., j:j + 1], s_j, -jnp.inf))
    p = jax.nn.softmax(jnp.concatenate(scores, axis=-1), axis=-1).astype(vp.dtype)
    o = jnp.einsum('bhqk,bkhd->bqhd', p[..., :MOBA_BLOCK], v_own)
    for j in range(n_sel):
        vb = vp[bi, top_idx[..., j], :, hi]
        o = o + jnp.einsum('bhqk,bhqkd->bqhd', p[..., (j + 1) * MOBA_BLOCK:(j + 2) * MOBA_BLOCK], vb)
    return o


def moba_prompt(q, k, v, slopes):
    b, s = q.shape[:2]
    nq = s // Q_BLOCK
    kp, vp, k_mean = moba_blocks(k, v)
    qb = q.reshape(b, nq, Q_BLOCK, MB_HEADS, MB_DIM).swapaxes(0, 1)

    def one(args):
        qi, q0 = args
        return moba_query_block(qi, kp, vp, k_mean, q0, slopes)

    o = lax.map(one, (qb, jnp.arange(nq, dtype=jnp.int32) * Q_BLOCK))
    return o.swapaxes(0, 1).reshape(b, s, MB_HEADS, MB_DIM)


def causal_conv(xpad, w, bias):
    l = xpad.shape[1] - (CONV_W - 1)
    y = bias
    for j in range(CONV_W):
        y = y + xpad[:, j:j + l] * w[j]
    return jax.nn.silu(y)


def ssd_scan(x, dt, a, bm, cm, h0, chunk):
    b, l, g, e, p = x.shape
    n = bm.shape[-1]
    nc = l // chunk
    xc = x.astype(F32).reshape(b, nc, chunk, g, e, p)
    dtc = dt.astype(F32).reshape(b, nc, chunk, g, e)
    bc = bm.astype(F32).reshape(b, nc, chunk, g, n)
    cc = cm.astype(F32).reshape(b, nc, chunk, g, n)
    acs = jnp.cumsum(dtc * a, axis=2)
    xdt = xc * dtc[..., None]
    tril = jnp.tril(jnp.ones((chunk, chunk), dtype=bool))
    seg = acs[:, :, :, None] - acs[:, :, None, :]
    decay = jnp.exp(jnp.where(tril[:, :, None, None], seg, -jnp.inf))
    y_diag = jnp.einsum('bclgn,bcsgn,bclsge,bcsgep->bclgep', cc, bc, decay, xdt)
    to_end = jnp.exp(acs[:, :, -1:] - acs)
    states = jnp.einsum('bcsgn,bcsge,bcsgep->bcgepn', bc, to_end, xdt)
    chunk_decay = jnp.exp(acs[:, :, -1])

    def step(h, inp):
        st, dec = inp
        return dec[..., None, None] * h + st, h

    h_fin, h_prev = lax.scan(step, h0.astype(F32),
                             (jnp.moveaxis(states, 1, 0), jnp.moveaxis(chunk_decay, 1, 0)))
    h_prev = jnp.moveaxis(h_prev, 0, 1)
    y_off = jnp.einsum('bclgn,bcgepn,bclge->bclgep', cc, h_prev, jnp.exp(acs))
    y = (y_diag + y_off).reshape(b, l, g, e, p)
    return y.astype(x.dtype), h_fin.astype(h0.dtype)


def ssd_mixer(zc, xbc_pad, dt_raw, h0, conv_w, conv_b, dt_bias, a_log, d_skip, norm_w, chunk):
    b, l = zc.shape[:2]
    xbc = causal_conv(xbc_pad, conv_w, conv_b)
    xs, bm, cm = jnp.split(xbc, [D_SSM, D_SSM + SSM_GROUPS * D_STATE], axis=-1)
    xs = xs.reshape(b, l, SSM_GROUPS, SSM_HPG, SSM_HEADDIM)
    bm = bm.reshape(b, l, SSM_GROUPS, D_STATE)
    cm = cm.reshape(b, l, SSM_GROUPS, D_STATE)
    dt = jax.nn.softplus(dt_raw.astype(F32) + dt_bias).reshape(b, l, SSM_GROUPS, SSM_HPG)
    a = -jnp.exp(a_log.astype(F32)).reshape(SSM_GROUPS, SSM_HPG)
    h0g = h0.reshape(b, SSM_GROUPS, SSM_HPG, SSM_HEADDIM, D_STATE)
    y, h_fin = ssd_scan(xs, dt, a, bm, cm, h0g, chunk)
    y = y + d_skip.reshape(SSM_GROUPS, SSM_HPG)[:, :, None] * xs
    y = y.reshape(b, l, D_SSM) * jax.nn.silu(zc)
    y = rms_norm(y.reshape(b, l, SSM_GROUPS, D_SSM // SSM_GROUPS), norm_w.reshape(SSM_GROUPS, -1))
    return y.reshape(b, l, D_SSM), h_fin.reshape(b, SSM_HEADS, SSM_HEADDIM, D_STATE)


def moe(x, w_router, b_router, wg, wu, wd):
    scores = jax.nn.sigmoid(jnp.dot(x, w_router, preferred_element_type=F32))
    biased = scores + b_router
    grp = biased.reshape(-1, N_EXPERT_GROUPS, EXPERTS_PER_GROUP)
    grp_score = jnp.sum(lax.top_k(grp, TOP_K)[0], axis=-1)
    best = jnp.argmax(grp_score, axis=-1)
    in_grp = (jnp.arange(N_EXPERTS) // EXPERTS_PER_GROUP)[None, :] == best[:, None]
    _, idx = lax.top_k(jnp.where(in_grp, biased, -jnp.inf), TOP_K)
    w = jnp.take_along_axis(scores, idx, axis=-1)
    w = w / jnp.sum(w, axis=-1, keepdims=True)
    gates = jnp.sum(jax.nn.one_hot(idx, N_EXPERTS, dtype=F32) * w[..., None], axis=1)
    hid = jax.nn.silu(jnp.einsum('nd,edf->nef', x, wg)) * jnp.einsum('nd,edf->nef', x, wu)
    return jnp.einsum('nef,efd->nd', hid * gates[..., None].astype(hid.dtype), wd)


def post_block(x, o_a, o_b, o_c, p, w_out, ln1_w, ln1_b, w_router, b_router, wg, wu, wd,
               w_ple_proj, w_ple_gate, ln2_w, ln2_b):
    b, l = x.shape[:2]
    mix = jnp.concatenate([o_a.reshape(b, l, -1), o_b.reshape(b, l, -1), o_c], axis=-1) @ w_out
    x = layer_norm(ALPHA * x + mix, ln1_w, ln1_b)
    ff = moe(x.reshape(b * l, D_MODEL), w_router, b_router, wg, wu, wd).reshape(b, l, D_MODEL)
    ple = jax.nn.sigmoid(x @ w_ple_gate) * (p @ w_ple_proj)
    return layer_norm(ALPHA * x + ff + ple, ln2_w, ln2_b)


def setup_inputs(seed: int = 0) -> dict:
    key = jax.random.key(seed)
    k = jax.random.split(key, 40)
    n_pages = PAST_LEN // PAGE_SIZE
    n_pool = (DEC_BATCH * n_pages * 5) // 4

    def nrm(kk, shape, scale):
        return jax.random.normal(kk, shape, F32) * scale

    kv_shape_a = (DEPTH, n_pool, PAGE_SIZE, DA_HEADS, 2 * DA_QK)
    kv_shape_b = (DEPTH, n_pool, PAGE_SIZE, MB_HEADS, MB_DIM)
    page_table = jax.random.permutation(k[10], n_pool)[:DEC_BATCH * n_pages].reshape(DEC_BATCH, n_pages).astype(jnp.int32)
    dt0 = jnp.exp(jax.random.uniform(k[15], (DEPTH, SSM_HEADS), F32, math.log(1e-3), math.log(1e-1)))
    return {
        'x_prompt': nrm(k[0], (BATCH, SEQ, D_MODEL), 1.0),
        'x_sample': nrm(k[1], (DEC_BATCH, DEC_SEQ, D_MODEL), 1.0),
        'p_prompt': nrm(k[2], (DEPTH, BATCH, SEQ, D_PLE), 1.0),
        'p_sample': nrm(k[3], (DEPTH, DEC_BATCH, DEC_SEQ, D_PLE), 1.0),
        'cache_diff_k': nrm(k[4], kv_shape_a, 1.0),
        'cache_diff_v': nrm(k[5], kv_shape_a, 1.0),
        'cache_moba_k': nrm(k[6], kv_shape_b, 1.0),
        'cache_moba_v': nrm(k[7], kv_shape_b, 1.0),
        'state_ssm': nrm(k[8], (DEPTH, DEC_BATCH, SSM_HEADS, SSM_HEADDIM, D_STATE), 0.1),
        'state_conv': nrm(k[9], (DEPTH, DEC_BATCH, CONV_W - 1, D_XBC), 1.0),
        'page_table': page_table,
        'w_in': nrm(k[11], (DEPTH, D_MODEL, D_IN_PROJ), D_MODEL ** -0.5),
        'w_out': nrm(k[12], (DEPTH, D_MIX, D_MODEL), BETA * D_MIX ** -0.5),
        'diff_lambda': nrm(k[13], (DEPTH, 4, DA_QK), 0.1),
        'diff_norm_w': 1.0 + nrm(k[14], (DEPTH, DA_V), 0.02),
        'ssm_conv_w': nrm(k[16], (DEPTH, CONV_W, D_XBC), CONV_W ** -0.5),
        'ssm_conv_b': nrm(k[17], (DEPTH, D_XBC), 0.01),
        'ssm_dt_bias': dt0 + jnp.log(-jnp.expm1(-dt0)),
        'ssm_a_log': jnp.log(jax.random.uniform(k[18], (DEPTH, SSM_HEADS), F32, 1.0, 16.0)),
        'ssm_d': 1.0 + nrm(k[19], (DEPTH, SSM_HEADS), 0.02),
        'ssm_norm_w': 1.0 + nrm(k[20], (DEPTH, D_SSM), 0.02),
        'ln1_w': 1.0 + nrm(k[21], (DEPTH, D_MODEL), 0.02),
        'ln1_b': nrm(k[22], (DEPTH, D_MODEL), 0.01),
        'ln2_w': 1.0 + nrm(k[23], (DEPTH, D_MODEL), 0.02),
        'ln2_b': nrm(k[24], (DEPTH, D_MODEL), 0.01),
        'w_router': nrm(k[25], (D_MODEL, N_EXPERTS), D_MODEL ** -0.5),
        'b_router': nrm(k[26], (N_EXPERTS,), 0.01),
        'w_exp_gate': nrm(k[27], (DEPTH, N_EXPERTS, D_MODEL, D_EXPERT), D_MODEL ** -0.5),
        'w_exp_up': nrm(k[28], (DEPTH, N_EXPERTS, D_MODEL, D_EXPERT), D_MODEL ** -0.5),
        'w_exp_down': nrm(k[29], (DEPTH, N_EXPERTS, D_EXPERT, D_MODEL), BETA * D_EXPERT ** -0.5),
        'w_ple_proj': nrm(k[30], (DEPTH, D_PLE, D_MODEL), BETA * D_PLE ** -0.5),
        'w_ple_gate': nrm(k[31], (DEPTH, D_MODEL, D_MODEL), D_MODEL ** -0.5),
    }


def reference(x_prompt, x_sample, p_prompt, p_sample, cache_diff_k, cache_diff_v, cache_moba_k,
              cache_moba_v, state_ssm, state_conv, page_table, w_in, w_out, diff_lambda, diff_norm_w,
              ssm_conv_w, ssm_conv_b, ssm_dt_bias, ssm_a_log, ssm_d, ssm_norm_w, ln1_w, ln1_b,
              ln2_w, ln2_b, w_router, b_router, w_exp_gate, w_exp_up, w_exp_down, w_ple_proj, w_ple_gate):
    slopes = alibi_slopes()
    slopes_a, slopes_b = slopes[:DA_HEADS], slopes[DA_HEADS:]
    xp, xs_ = x_prompt, x_sample
    b_p = xp.shape[0]
    t_s = xs_.shape[1]
    q_pos_s = PAST_LEN + jnp.arange(t_s, dtype=jnp.int32)
    k_pos_s = jnp.arange(PAST_LEN + t_s, dtype=jnp.int32)
    pk_a, pv_a, pk_b, pv_b, p_ssm, p_conv = [], [], [], [], [], []
    sk_a, sv_a, sk_b, sv_b, s_ssm, s_conv = [], [], [], [], [], []
    for i in range(DEPTH):
        lam_init = 0.8 - 0.6 * math.exp(-0.3 * i)
        lq1, lk1, lq2, lk2 = [diff_lambda[i, j].astype(F32) for j in range(4)]
        lam = jnp.exp(jnp.sum(lq1 * lk1)) - jnp.exp(jnp.sum(lq2 * lk2)) + lam_init
        ssd_w = (ssm_conv_w[i], ssm_conv_b[i], ssm_dt_bias[i], ssm_a_log[i], ssm_d[i], ssm_norm_w[i])
        post_w = (w_out[i], ln1_w[i], ln1_b[i], w_router, b_router, w_exp_gate[i], w_exp_up[i],
                  w_exp_down[i], w_ple_proj[i], w_ple_gate[i], ln2_w[i], ln2_b[i])

        qa, ka, va, qb, kb, vb, zc, xbc, dt_raw = project(xp, w_in[i])
        o_a = diff_prompt(qa, ka, va, slopes_a, lam, lam_init, diff_norm_w[i])
        o_b = moba_prompt(qb, kb, vb, slopes_b)
        h0 = jnp.zeros((b_p, SSM_HEADS, SSM_HEADDIM, D_STATE), dtype=xp.dtype)
        o_c, h_fin = ssd_mixer(zc, jnp.pad(xbc, ((0, 0), (CONV_W - 1, 0), (0, 0))), dt_raw, h0,
                               *ssd_w, SSD_CHUNK)
        xp = post_block(xp, o_a, o_b, o_c, p_prompt[i], *post_w)
        pk_a.append(ka)
        pv_a.append(va)
        pk_b.append(kb)
        pv_b.append(vb)
        p_ssm.append(h_fin)
        p_conv.append(xbc[:, xbc.shape[1] - (CONV_W - 1):])

        qa, ka, va, qb, kb, vb, zc, xbc, dt_raw = project(xs_, w_in[i])
        ka_all = jnp.concatenate([gather_pages(cache_diff_k[i], page_table), ka], axis=1)
        va_all = jnp.concatenate([gather_pages(cache_diff_v[i], page_table), va], axis=1)
        o_a = diff_attention(qa, ka_all, va_all, q_pos_s, k_pos_s, slopes_a, lam, lam_init, diff_norm_w[i])
        kb_all = jnp.concatenate([gather_pages(cache_moba_k[i], page_table), kb], axis=1)
        vb_all = jnp.concatenate([gather_pages(cache_moba_v[i], page_table), vb], axis=1)
        kp, vp, k_mean = moba_blocks(kb_all, vb_all)
        o_b = moba_query_block(qb, kp, vp, k_mean, PAST_LEN, slopes_b)
        xbc_pad = jnp.concatenate([state_conv[i], xbc], axis=1)
        o_c, h_new = ssd_mixer(zc, xbc_pad, dt_raw, state_ssm[i], *ssd_w, t_s)
        xs_ = post_block(xs_, o_a, o_b, o_c, p_sample[i], *post_w)
        sk_a.append(ka)
        sv_a.append(va)
        sk_b.append(kb)
        sv_b.append(vb)
        s_ssm.append(h_new)
        s_conv.append(xbc_pad[:, xbc_pad.shape[1] - (CONV_W - 1):])

    return (xp, xs_,
            jnp.stack(pk_a), jnp.stack(pv_a), jnp.stack(pk_b), jnp.stack(pv_b), jnp.stack(p_ssm), jnp.stack(p_conv),
            jnp.stack(sk_a), jnp.stack(sv_a), jnp.stack(sk_b), jnp.stack(sv_b), jnp.stack(s_ssm), jnp.stack(s_conv))
```

```python
import functools
import math

import jax
import jax.numpy as jnp
import numpy as np
from jax import lax
from jax.experimental import pallas as pl
from jax.experimental.pallas import tpu as pltpu

F32 = jnp.float32
BF16 = jnp.bfloat16

LANES = 128
HEAD_DIM = 128
DA_HEADS = 4
DA_QK = 64
MB_HEADS = 4
MOBA_BLOCK = 256
MOBA_TOPK = 3
D_SSM = 1024
SSM_HEADS = 16
SSM_HEADDIM = 64
SSM_GROUPS = 2
D_STATE = 128
CONV_W = 4
D_XBC = D_SSM + 2 * SSM_GROUPS * D_STATE
N_EXPERTS = 16
N_EXPERT_GROUPS = 4
EXPERTS_PER_GROUP = 4
LN_EPS = 1e-5
RMS_EPS = 1e-6
VMEM_LIMIT = 56 * 1024 * 1024

QA_BLK, KA_BLK, VA_BLK, QB_BLK, KB_BLK, VB_BLK = 0, 4, 8, 12, 16, 20
ZC_COL, XS_COL, BC_COL, DT_COL = 3072, 4096, 5120, 5632
D_IN_PROJ = 5648
D_IN_PAD = 5760

NT_DIMS = (((1,), (1,)), ((), ()))
TN_DIMS = (((0,), (0,)), ((), ()))


def _cparams(n_axes):
    return pltpu.CompilerParams(dimension_semantics=("arbitrary",) * n_axes, vmem_limit_bytes=VMEM_LIMIT)


def _mm_kernel(x_ref, w_ref, o_ref):
    o_ref[...] = jnp.dot(x_ref[...], w_ref[...], preferred_element_type=F32).astype(o_ref.dtype)


def _matmul(x, w, tm, tn, out_dtype=F32):
    m, k = x.shape
    n = w.shape[1]
    return pl.pallas_call(
        _mm_kernel,
        grid=(n // tn, m // tm),
        in_specs=[pl.BlockSpec((tm, k), lambda j, i: (i, 0)), pl.BlockSpec((k, tn), lambda j, i: (0, j))],
        out_specs=pl.BlockSpec((tm, tn), lambda j, i: (i, j)),
        out_shape=jax.ShapeDtypeStruct((m, n), out_dtype),
        compiler_params=_cparams(2),
        name="in_proj",
    )(x, w)


def _diff_prompt_kernel(slopes_ref, dl_ref, q_ref, k_ref, v_ref, nw_ref, o_ref,
                        kb_sc, vb_sc, m_sc, l_sc, acc_sc, *, tq, tk, lam_init):
    h = pl.program_id(1)
    qi = pl.program_id(2)

    @pl.when(qi == 0)
    def _():
        kb_sc[...] = k_ref[...].astype(BF16)
        vb_sc[...] = v_ref[...].astype(BF16)

    q = q_ref[...] * (DA_QK ** -0.5)
    lane = lax.broadcasted_iota(jnp.int32, q.shape, 1)
    qs = jnp.concatenate([jnp.where(lane < DA_QK, q, 0.0), jnp.where(lane >= DA_QK, q, 0.0)], axis=0).astype(BF16)
    q0 = qi * tq
    slope = slopes_ref[h]
    m_sc[...] = jnp.full(m_sc.shape, -jnp.inf, F32)
    l_sc[...] = jnp.zeros(l_sc.shape, F32)
    acc_sc[...] = jnp.zeros(acc_sc.shape, F32)

    def step(kj, masked):
        k0 = pl.multiple_of(kj * tk, tk)
        s = lax.dot_general(qs, kb_sc[pl.ds(k0, tk), :], NT_DIMS, preferred_element_type=F32)
        col = lax.broadcasted_iota(jnp.int32, (1, tk), 1)
        s = s + slope * (col + (k0 - q0)).astype(F32)
        if masked:
            row = lax.broadcasted_iota(jnp.int32, s.shape, 0)
            row = jnp.where(row >= tq, row - tq, row)
            colf = lax.broadcasted_iota(jnp.int32, s.shape, 1)
            s = jnp.where(colf + k0 <= row + q0, s, -jnp.inf)
        m_old = m_sc[...]
        m_new = jnp.maximum(m_old, jnp.max(s, axis=-1, keepdims=True))
        alpha = jnp.exp(m_old - m_new)
        p = jnp.exp(s - m_new)
        l_sc[...] = alpha * l_sc[...] + jnp.sum(p, axis=-1, keepdims=True)
        acc_sc[...] = alpha * acc_sc[...] + jnp.dot(p.astype(BF16), vb_sc[pl.ds(k0, tk), :],
                                                    preferred_element_type=F32)
        m_sc[...] = m_new

    n_full = q0 // tk

    def body(kj, carry):
        step(kj, False)
        return carry

    lax.fori_loop(0, n_full, body, 0)
    step(n_full, True)

    dl = dl_ref[...]
    lam = (jnp.exp(jnp.sum(dl[0:1] * dl[1:2], axis=-1, keepdims=True))
           - jnp.exp(jnp.sum(dl[2:3] * dl[3:4], axis=-1, keepdims=True)) + lam_init)
    o2 = acc_sc[...] / l_sc[...]
    o = o2[:tq] - lam * o2[tq:]
    var = jnp.mean(o * o, axis=-1, keepdims=True)
    o = o * lax.rsqrt(var + RMS_EPS) * nw_ref[...] * (1.0 - lam_init)
    o_ref[...] = o.astype(o_ref.dtype)


def _diff_prompt(z, slopes, dl, nw, *, batch, seq, lam_init, tq=128, tk=512):
    nq = seq // tq
    kern = functools.partial(_diff_prompt_kernel, tq=tq, tk=tk, lam_init=lam_init)
    return pl.pallas_call(
        kern,
        grid=(batch, DA_HEADS, nq),
        in_specs=[
            pl.BlockSpec(memory_space=pltpu.SMEM),
            pl.BlockSpec((4, DA_QK), lambda b, h, i: (0, 0)),
            pl.BlockSpec((tq, HEAD_DIM), lambda b, h, i: (b * nq + i, QA_BLK + h)),
            pl.BlockSpec((seq, HEAD_DIM), lambda b, h, i: (b, KA_BLK + h)),
            pl.BlockSpec((seq, HEAD_DIM), lambda b, h, i: (b, VA_BLK + h)),
            pl.BlockSpec((1, HEAD_DIM), lambda b, h, i: (0, 0)),
        ],
        out_specs=pl.BlockSpec((tq, HEAD_DIM), lambda b, h, i: (b * nq + i, h)),
        out_shape=jax.ShapeDtypeStruct((batch * seq, DA_HEADS * HEAD_DIM), BF16),
        scratch_shapes=[
            pltpu.VMEM((seq, HEAD_DIM), BF16), pltpu.VMEM((seq, HEAD_DIM), BF16),
            pltpu.VMEM((2 * tq, 1), F32), pltpu.VMEM((2 * tq, 1), F32), pltpu.VMEM((2 * tq, HEAD_DIM), F32),
        ],
        compiler_params=_cparams(3),
        name="diff_prompt",
    )(slopes, dl, z, z, z, nw)


def _top3_bias(gate, n_valid):
    col = lax.broadcasted_iota(jnp.int32, gate.shape, 1)
    g = jnp.where(col < n_valid, gate, -jnp.inf)
    sel = jnp.zeros(gate.shape, jnp.bool_)
    for _ in range(MOBA_TOPK):
        mx = jnp.max(g, axis=-1, keepdims=True)
        idx = jnp.min(jnp.where(g == mx, col, gate.shape[1]), axis=-1, keepdims=True)
        pick = (col == idx) & (mx > -jnp.inf)
        sel = sel | pick
        g = jnp.where(pick, -jnp.inf, g)
    return jnp.where(sel, 0.0, -jnp.inf)


def _moba_prompt_kernel(slopes_ref, q_ref, k_ref, v_ref, o_ref,
                        kb_sc, vb_sc, kmean_sc, bias_sc, m_sc, l_sc, acc_sc, *, nb):
    blk = MOBA_BLOCK
    h = pl.program_id(1)
    own = pl.program_id(2)

    @pl.when(own == 0)
    def _():
        kb_sc[...] = k_ref[...].astype(BF16)
        vb_sc[...] = v_ref[...].astype(BF16)
        for n in range(nb):
            kmean_sc[n:n + 1, :] = jnp.mean(k_ref[n * blk:(n + 1) * blk, :], axis=0, keepdims=True)

    q = q_ref[...]
    gate = lax.dot_general(q, kmean_sc[...], NT_DIMS, preferred_element_type=F32,
                           precision=lax.Precision.HIGHEST)
    sel_bias = _top3_bias(gate, own)
    for n in range(nb):
        bias_sc[n] = jnp.broadcast_to(sel_bias[:, n:n + 1], (blk, LANES))

    qs = (q * (HEAD_DIM ** -0.5)).astype(BF16)
    q0 = own * blk
    slope = slopes_ref[DA_HEADS + h]
    col = lax.broadcasted_iota(jnp.int32, (1, blk), 1)

    def scores(n):
        k0 = pl.multiple_of(n * blk, blk)
        s = lax.dot_general(qs, kb_sc[pl.ds(k0, blk), :], NT_DIMS, preferred_element_type=F32)
        return s + slope * (col + (k0 - q0)).astype(F32), k0

    s, k0 = scores(own)
    row = lax.broadcasted_iota(jnp.int32, s.shape, 0)
    colf = lax.broadcasted_iota(jnp.int32, s.shape, 1)
    s = jnp.where(colf <= row, s, -jnp.inf)
    m0 = jnp.max(s, axis=-1, keepdims=True)
    p = jnp.exp(s - m0)
    m_sc[...] = m0
    l_sc[...] = jnp.sum(p, axis=-1, keepdims=True)
    acc_sc[...] = jnp.dot(p.astype(BF16), vb_sc[pl.ds(k0, blk), :], preferred_element_type=F32)

    def body(n, carry):
        s, k0 = scores(n)
        b = bias_sc[n]
        s = s + jnp.concatenate([b] * (blk // LANES), axis=1)
        m_old = m_sc[...]
        m_new = jnp.maximum(m_old, jnp.max(s, axis=-1, keepdims=True))
        alpha = jnp.exp(m_old - m_new)
        p = jnp.exp(s - m_new)
        l_sc[...] = alpha * l_sc[...] + jnp.sum(p, axis=-1, keepdims=True)
        acc_sc[...] = alpha * acc_sc[...] + jnp.dot(p.astype(BF16), vb_sc[pl.ds(k0, blk), :],
                                                    preferred_element_type=F32)
        m_sc[...] = m_new
        return carry

    lax.fori_loop(0, own, body, 0)
    o_ref[...] = (acc_sc[...] / l_sc[...]).astype(o_ref.dtype)


def _moba_prompt(z, slopes, *, batch, seq):
    blk = MOBA_BLOCK
    nb = seq // blk
    kern = functools.partial(_moba_prompt_kernel, nb=nb)
    return pl.pallas_call(
        kern,
        grid=(batch, MB_HEADS, nb),
        in_specs=[
            pl.BlockSpec(memory_space=pltpu.SMEM),
            pl.BlockSpec((blk, HEAD_DIM), lambda b, h, i: (b * nb + i, QB_BLK + h)),
            pl.BlockSpec((seq, HEAD_DIM), lambda b, h, i: (b, KB_BLK + h)),
            pl.BlockSpec((seq, HEAD_DIM), lambda b, h, i: (b, VB_BLK + h)),
        ],
        out_specs=pl.BlockSpec((blk, HEAD_DIM), lambda b, h, i: (b * nb + i, h)),
        out_shape=jax.ShapeDtypeStruct((batch * seq, MB_HEADS * HEAD_DIM), BF16),
        scratch_shapes=[
            pltpu.VMEM((seq, HEAD_DIM), BF16), pltpu.VMEM((seq, HEAD_DIM), BF16),
            pltpu.VMEM((nb, HEAD_DIM), F32), pltpu.VMEM((nb, blk, LANES), F32),
            pltpu.VMEM((blk, 1), F32), pltpu.VMEM((blk, 1), F32), pltpu.VMEM((blk, HEAD_DIM), F32),
        ],
        compiler_params=_cparams(3),
        name="moba_prompt",
    )(slopes, z, z, z)


def _softplus(x):
    return jnp.maximum(x, 0.0) + jnp.log1p(jnp.exp(-jnp.abs(x)))


def _silu(x):
    return x * jax.nn.sigmoid(x)


def _conv_silu(xp_sc, w, b, t):
    y = b
    for j in range(CONV_W):
        y = y + xp_sc[5 + j:5 + j + t, :] * w[j:j + 1, :]
    return _silu(y)


def _ssd_prompt_kernel(zc_ref, x_ref, bc_ref, dt_ref, cw_ref, cb_ref, dtb_ref, alog_ref, dsk_ref, nw_ref,
                       y_ref, hfin_ref, xpx_sc, xpbc_sc, h_sc, y_sc, *, t):
    c = pl.program_id(1)
    nc = pl.num_programs(1)
    hp = LANES // SSM_HEADDIM
    hpg = SSM_HEADS // SSM_GROUPS

    @pl.when(c == 0)
    def _():
        xpx_sc[0:8, :] = jnp.zeros((8, D_SSM), F32)
        xpbc_sc[0:8, :] = jnp.zeros((8, D_XBC - D_SSM), F32)
        h_sc[...] = jnp.zeros(h_sc.shape, F32)

    xpx_sc[8:8 + t, :] = x_ref[...]
    xpbc_sc[8:8 + t, :] = bc_ref[...]
    cw = cw_ref[...]
    cb = cb_ref[...]
    xs = _conv_silu(xpx_sc, cw[:, :D_SSM], cb[:, :D_SSM], t)
    bc = _conv_silu(xpbc_sc, cw[:, D_SSM:], cb[:, D_SSM:], t)
    xpx_sc[5:8, :] = xpx_sc[t + 5:t + 8, :]
    xpbc_sc[5:8, :] = xpbc_sc[t + 5:t + 8, :]

    dt = _softplus(dt_ref[...] + dtb_ref[...])
    da = dt * (-jnp.exp(alog_ref[...]))
    r_i = lax.broadcasted_iota(jnp.int32, (t, t), 0)
    c_i = lax.broadcasted_iota(jnp.int32, (t, t), 1)
    tril = c_i <= r_i
    acs = jnp.dot(tril.astype(F32), da, preferred_element_type=F32, precision=lax.Precision.HIGHEST)
    acs_t = acs.T
    lane = lax.broadcasted_iota(jnp.int32, (t, LANES), 1)
    sub = lax.broadcasted_iota(jnp.int32, (LANES, LANES), 0)
    lo_lane = lane < SSM_HEADDIM
    lo_sub = sub < SSM_HEADDIM

    for g in range(SSM_GROUPS):
        bg = bc[:, g * D_STATE:(g + 1) * D_STATE]
        cg = bc[:, (SSM_GROUPS + g) * D_STATE:(SSM_GROUPS + g + 1) * D_STATE]
        bgb = bg.astype(BF16)
        cgb = cg.astype(BF16)
        gmat = lax.dot_general(cgb, bgb, NT_DIMS, preferred_element_type=F32)
        for pr in range(hpg // hp):
            ha = g * hpg + pr * hp
            cb0 = ha * SSM_HEADDIM
            x2 = xs[:, cb0:cb0 + LANES]
            ys = []
            dt2 = jnp.where(lo_lane, dt[:, ha:ha + 1], dt[:, ha + 1:ha + 2])
            xdt = x2 * dt2
            ydiag = jnp.zeros((t, LANES), F32)
            for u in range(hp):
                hh = ha + u
                seg = acs[:, hh:hh + 1] - acs_t[hh:hh + 1, :]
                lmat = jnp.exp(jnp.where(tril, seg, -jnp.inf))
                xu = jnp.where(lo_lane if u == 0 else ~lo_lane, xdt, 0.0)
                ydiag = ydiag + jnp.dot((gmat * lmat).astype(BF16), xu.astype(BF16), preferred_element_type=F32)
            h2 = h_sc[ha:ha + hp].reshape(LANES, D_STATE)
            yoff = lax.dot_general(cgb, h2.astype(BF16), NT_DIMS, preferred_element_type=F32)
            e2 = jnp.where(lo_lane, jnp.exp(acs[:, ha:ha + 1]), jnp.exp(acs[:, ha + 1:ha + 2]))
            y = ydiag + e2 * yoff + dsk_ref[:, cb0:cb0 + LANES] * x2
            y_sc[:, cb0:cb0 + LANES] = y * _silu(zc_ref[:, cb0:cb0 + LANES])
            last = acs[t - 1:t, :]
            te2 = jnp.where(lo_lane, jnp.exp(last[:, ha:ha + 1] - acs[:, ha:ha + 1]),
                            jnp.exp(last[:, ha + 1:ha + 2] - acs[:, ha + 1:ha + 2]))
            upd = lax.dot_general((xdt * te2).astype(BF16), bgb, TN_DIMS, preferred_element_type=F32)
            dec = jnp.where(lo_sub, jnp.exp(last[:, ha:ha + 1]), jnp.exp(last[:, ha + 1:ha + 2]))
            h_sc[ha:ha + hp] = (dec * h2 + upd).reshape(hp, SSM_HEADDIM, D_STATE)

    gw = D_SSM // SSM_GROUPS
    for g in range(SSM_GROUPS):
        yg = y_sc[:, g * gw:(g + 1) * gw]
        var = jnp.mean(yg * yg, axis=-1, keepdims=True)
        y_ref[:, g * gw:(g + 1) * gw] = (yg * lax.rsqrt(var + RMS_EPS) * nw_ref[:, g * gw:(g + 1) * gw]).astype(y_ref.dtype)

    @pl.when(c == nc - 1)
    def _():
        hfin_ref[0] = h_sc[...]


def _ssd_prompt(z, cw, cb, dtb, alog, dsk, nw, *, batch, seq, t=128):
    nc = seq // t
    kern = functools.partial(_ssd_prompt_kernel, t=t)
    full = lambda shape: pl.BlockSpec(shape, lambda b, c: (0,) * len(shape))
    return pl.pallas_call(
        kern,
        grid=(batch, nc),
        in_specs=[
            pl.BlockSpec((t, D_SSM), lambda b, c: (b * nc + c, ZC_COL // D_SSM)),
            pl.BlockSpec((t, D_SSM), lambda b, c: (b * nc + c, XS_COL // D_SSM)),
            pl.BlockSpec((t, 512), lambda b, c: (b * nc + c, BC_COL // 512)),
            pl.BlockSpec((t, LANES), lambda b, c: (b * nc + c, DT_COL // LANES)),
            full((CONV_W, D_XBC)), full((1, D_XBC)), full((1, LANES)), full((1, LANES)),
            full((1, D_SSM)), full((1, D_SSM)),
        ],
        out_specs=[
            pl.BlockSpec((t, D_SSM), lambda b, c: (b * nc + c, 0)),
            pl.BlockSpec((1, SSM_HEADS, SSM_HEADDIM, D_STATE), lambda b, c: (b, 0, 0, 0)),
        ],
        out_shape=[
            jax.ShapeDtypeStruct((batch * seq, D_SSM), BF16),
            jax.ShapeDtypeStruct((batch, SSM_HEADS, SSM_HEADDIM, D_STATE), F32),
        ],
        scratch_shapes=[
            pltpu.VMEM((t + 8, D_SSM), F32), pltpu.VMEM((t + 8, D_XBC - D_SSM), F32),
            pltpu.VMEM((SSM_HEADS, SSM_HEADDIM, D_STATE), F32), pltpu.VMEM((t, D_SSM), F32),
        ],
        compiler_params=_cparams(2),
        name="ssd_prompt",
    )(z, z, z, z, cw, cb, dtb, alog, dsk, nw)


NEW_ROWS = 16


def _attn_decode_kernel(*refs, moba, n_pages, page, t_new, lam_init):
    pt_ref = refs[0]
    del pt_ref
    slopes_ref, dl_ref, nw_ref, q_ref, kn_ref, vn_ref = refs[1:7]
    kp_refs = refs[7:7 + n_pages]
    vp_refs = refs[7 + n_pages:7 + 2 * n_pages]
    o_ref = refs[7 + 2 * n_pages]
    kb_sc, vb_sc, new_sc = refs[8 + 2 * n_pages:]
    past = n_pages * page
    width = DA_HEADS * HEAD_DIM
    cols_per_head = t_new if moba else 2 * t_new
    lanes_per_col_grp = HEAD_DIM if moba else DA_QK
    scale = HEAD_DIM ** -0.5 if moba else DA_QK ** -0.5

    ksum = []
    for p in range(n_pages):
        kp = kp_refs[p][...]
        kb_sc[p * page:(p + 1) * page, :] = kp.astype(BF16)
        vb_sc[p * page:(p + 1) * page, :] = vp_refs[p][...].astype(BF16)
        if moba:
            ksum.append(jnp.sum(kp, axis=0, keepdims=True))
    for src, dst in ((kn_ref, kb_sc), (vn_ref, vb_sc)):
        new_sc[...] = jnp.zeros(new_sc.shape, F32)
        new_sc[0:t_new, :] = src[...]
        dst[past:past + NEW_ROWS, :] = new_sc[...].astype(BF16)

    q = q_ref[...]
    qt = jnp.concatenate([q] * (LANES // t_new), axis=0)
    r_i = lax.broadcasted_iota(jnp.int32, qt.shape, 0)
    l_i = lax.broadcasted_iota(jnp.int32, qt.shape, 1)
    qmat = jnp.where(l_i // lanes_per_col_grp == r_i // t_new, qt, 0.0)
    qb = (qmat * scale).astype(BF16)

    s = lax.dot_general(kb_sc[...], qb, NT_DIMS, preferred_element_type=F32)
    coli = lax.broadcasted_iota(jnp.int32, (1, LANES), 1)
    slope = slopes_ref[...]
    rowp = lax.broadcasted_iota(jnp.int32, (past, LANES), 0)
    s_past = s[:past] + slope * (rowp - past).astype(F32)
    rown = lax.broadcasted_iota(jnp.int32, (NEW_ROWS, LANES), 0)
    s_new = s[past:] + slope * rown.astype(F32)
    s_new = jnp.where(rown <= coli % t_new, s_new, -jnp.inf)

    if moba:
        nbp = past // MOBA_BLOCK
        ppb = MOBA_BLOCK // page
        gate = jnp.zeros((nbp, LANES), F32)
        kmean = jnp.concatenate(
            [sum(ksum[n * ppb:(n + 1) * ppb]) for n in range(nbp)], axis=0) * (1.0 / MOBA_BLOCK)
        for i in range(MB_HEADS * t_new):
            hd, tok = i // t_new, i % t_new
            lo = hd * HEAD_DIM
            g_i = jnp.sum(kmean[:, lo:lo + HEAD_DIM] * q[tok:tok + 1, lo:lo + HEAD_DIM], axis=-1, keepdims=True)
            gate = jnp.where(coli == i, g_i, gate)
        rowb = lax.broadcasted_iota(jnp.int32, gate.shape, 0)
        sel = jnp.zeros(gate.shape, jnp.bool_)
        g = gate
        for _ in range(MOBA_TOPK):
            mx = jnp.max(g, axis=0, keepdims=True)
            idx = jnp.min(jnp.where(g == mx, rowb, nbp), axis=0, keepdims=True)
            pick = (rowb == idx) & (mx > -jnp.inf)
            sel = sel | pick
            g = jnp.where(pick, -jnp.inf, g)
        sel_bias = jnp.where(sel, 0.0, -jnp.inf)
        s_past = (s_past.reshape(nbp, MOBA_BLOCK, LANES) + sel_bias[:, None, :]).reshape(past, LANES)

    m = jnp.maximum(jnp.max(s_past, axis=0, keepdims=True), jnp.max(s_new, axis=0, keepdims=True))
    p_past = jnp.exp(s_past - m)
    p_new = jnp.exp(s_new - m)
    l = jnp.sum(p_past, axis=0, keepdims=True) + jnp.sum(p_new, axis=0, keepdims=True)
    pt = jnp.concatenate([p_past, p_new], axis=0).astype(BF16)
    o_t = lax.dot_general(vb_sc[...], pt, TN_DIMS, preferred_element_type=F32)
    o = o_t.T
    l_col = jnp.broadcast_to(l, (LANES, LANES)).T
    o = o / jnp.concatenate([l_col] * (width // LANES), axis=1)

    outs = []
    for hd in range(DA_HEADS):
        lo = hd * HEAD_DIM
        r0 = hd * cols_per_head
        if moba:
            outs.append(o[r0:r0 + t_new, lo:lo + HEAD_DIM])
        else:
            dl = dl_ref[...]
            lam = (jnp.exp(jnp.sum(dl[0:1] * dl[1:2], axis=-1, keepdims=True))
                   - jnp.exp(jnp.sum(dl[2:3] * dl[3:4], axis=-1, keepdims=True)) + lam_init)
            oh = o[r0:r0 + t_new, lo:lo + HEAD_DIM] - lam * o[r0 + t_new:r0 + 2 * t_new, lo:lo + HEAD_DIM]
            var = jnp.mean(oh * oh, axis=-1, keepdims=True)
            outs.append(oh * lax.rsqrt(var + RMS_EPS) * nw_ref[...] * (1.0 - lam_init))
    o_ref[...] = jnp.concatenate(outs, axis=1).astype(o_ref.dtype)


def _attn_decode(zs, cache_k, cache_v, page_table, slopes_cols, dl, nw, *, layer, moba, lam_init):
    n_seq, t_new, _ = zs.shape
    page = cache_k.shape[2]
    n_pages = page_table.shape[1]
    width = DA_HEADS * HEAD_DIM
    qblk, kblk, vblk = (QB_BLK, KB_BLK, VB_BLK) if moba else (QA_BLK, KA_BLK, VA_BLK)
    past = n_pages * page

    def page_spec(p):
        return pl.BlockSpec((None, None, page, width), lambda b, pt: (layer, pt[b, p], 0, 0))

    def z_spec(blk):
        return pl.BlockSpec((None, t_new, width), lambda b, pt: (b, 0, blk // DA_HEADS))

    kern = functools.partial(_attn_decode_kernel, moba=moba, n_pages=n_pages, page=page, t_new=t_new,
                             lam_init=lam_init)
    grid_spec = pltpu.PrefetchScalarGridSpec(
        num_scalar_prefetch=1,
        grid=(n_seq,),
        in_specs=[
            pl.BlockSpec((1, LANES), lambda b, pt: (0, 0)),
            pl.BlockSpec((4, DA_QK), lambda b, pt: (0, 0)),
            pl.BlockSpec((1, HEAD_DIM), lambda b, pt: (0, 0)),
            z_spec(qblk), z_spec(kblk), z_spec(vblk),
        ] + [page_spec(p) for p in range(n_pages)] * 2,
        out_specs=pl.BlockSpec((None, t_new, width), lambda b, pt: (b, 0, 0)),
        scratch_shapes=[
            pltpu.VMEM((past + NEW_ROWS, width), BF16), pltpu.VMEM((past + NEW_ROWS, width), BF16),
            pltpu.VMEM((NEW_ROWS, width), F32),
        ],
    )
    return pl.pallas_call(
        kern,
        grid_spec=grid_spec,
        out_shape=jax.ShapeDtypeStruct((n_seq, t_new, width), BF16),
        compiler_params=_cparams(1),
        name="moba_decode" if moba else "diff_decode",
    )(page_table, slopes_cols, dl, nw, zs, zs, zs, *([cache_k] * n_pages), *([cache_v] * n_pages))


def _ssd_decode_kernel(zc_ref, xs_ref, bc_ref, dtx_ref, cst_ref, h0_ref, cw_ref, cb_ref, dtb_ref, alog_ref,
                       dsk_ref, nw_ref, y_ref, hout_ref, cout_ref, xp_sc, *, t_new):
    hpg = SSM_HEADS // SSM_GROUPS
    gw = D_SSM // SSM_GROUPS
    xp_sc[0:CONV_W - 1, :] = cst_ref[...]
    xp_sc[CONV_W - 1:CONV_W - 1 + t_new, :D_SSM] = xs_ref[...]
    xp_sc[CONV_W - 1:CONV_W - 1 + t_new, D_SSM:] = bc_ref[...]
    cout_ref[...] = xp_sc[t_new:t_new + CONV_W - 1, :]
    y = cb_ref[...]
    for j in range(CONV_W):
        y = y + xp_sc[j:j + t_new, :] * cw_ref[j:j + 1, :]
    xbc = _silu(y)
    xs = xbc[:, :D_SSM]
    dt = _softplus(dtx_ref[...] + dtb_ref[...])
    da = dt * (-jnp.exp(alog_ref[...]))
    acs = [da[0:1]]
    for s in range(1, t_new):
        acs.append(acs[-1] + da[s:s + 1])
    xdt = xs * dt
    last = acs[-1]
    wx = jnp.concatenate([xdt[s:s + 1] * jnp.exp(last - acs[s]) for s in range(t_new)], axis=0)
    dec = jnp.exp(last)
    d_hi = dec.astype(BF16).astype(F32)
    d_md = (dec - d_hi).astype(BF16).astype(F32)
    d_lo = (dec - d_hi - d_md).astype(BF16).astype(F32)
    dec3 = jnp.concatenate([d_hi, d_md, d_lo, jnp.zeros((5, D_SSM), F32)], axis=0).astype(BF16)
    ones = jnp.ones((8, D_STATE), BF16)
    zc = zc_ref[...]
    for g in range(SSM_GROUPS):
        sl = slice(g * gw, (g + 1) * gw)
        bg = xbc[:, D_SSM + g * D_STATE:D_SSM + (g + 1) * D_STATE]
        cg = xbc[:, D_SSM + (SSM_GROUPS + g) * D_STATE:D_SSM + (SSM_GROUPS + g + 1) * D_STATE]
        h0 = h0_ref[g * hpg:(g + 1) * hpg].reshape(gw, D_STATE)
        yoff = lax.dot_general(cg.astype(BF16), h0.astype(BF16), NT_DIMS, preferred_element_type=F32)
        rows = []
        for t in range(t_new):
            yt = jnp.exp(acs[t][:, sl]) * yoff[t:t + 1]
            for s in range(t + 1):
                gts = jnp.sum(cg[t:t + 1] * bg[s:s + 1], axis=-1, keepdims=True)
                yt = yt + gts * jnp.exp(acs[t][:, sl] - acs[s][:, sl]) * xdt[s:s + 1, sl]
            rows.append(yt)
        yg = jnp.concatenate(rows, axis=0) + dsk_ref[:, sl] * xs[:, sl]
        yg = yg * _silu(zc[:, sl])
        var = jnp.mean(yg * yg, axis=-1, keepdims=True)
        y_ref[:, sl] = (yg * lax.rsqrt(var + RMS_EPS) * nw_ref[:, sl]).astype(y_ref.dtype)
        wx8 = jnp.concatenate([wx[:, sl], jnp.zeros((8 - t_new, gw), F32)], axis=0).astype(BF16)
        bg8 = jnp.concatenate([bg, jnp.zeros((8 - t_new, D_STATE), F32)], axis=0).astype(BF16)
        upd = lax.dot_general(wx8, bg8, TN_DIMS, preferred_element_type=F32)
        dec_full = lax.dot_general(dec3[:, sl], ones, TN_DIMS, preferred_element_type=F32)
        hout_ref[g * hpg:(g + 1) * hpg] = (dec_full * h0 + upd).reshape(hpg, SSM_HEADDIM, D_STATE)


def _ssd_decode(zs, dtx, state_conv, state_ssm, cw, cb, dtbx, alogx, dskx, nw, *, layer):
    n_seq, t_new, _ = zs.shape
    kern = functools.partial(_ssd_decode_kernel, t_new=t_new)
    full = lambda shape: pl.BlockSpec(shape, lambda b: (0,) * len(shape))
    return pl.pallas_call(
        kern,
        grid=(n_seq,),
        in_specs=[
            pl.BlockSpec((None, t_new, D_SSM), lambda b: (b, 0, ZC_COL // D_SSM)),
            pl.BlockSpec((None, t_new, D_SSM), lambda b: (b, 0, XS_COL // D_SSM)),
            pl.BlockSpec((None, t_new, 512), lambda b: (b, 0, BC_COL // 512)),
            pl.BlockSpec((None, t_new, D_SSM), lambda b: (b, 0, 0)),
            pl.BlockSpec((None, None, CONV_W - 1, D_XBC), lambda b: (layer, b, 0, 0)),
            pl.BlockSpec((None, None, SSM_HEADS, SSM_HEADDIM, D_STATE), lambda b: (layer, b, 0, 0, 0)),
            full((CONV_W, D_XBC)), full((1, D_XBC)), full((1, D_SSM)), full((1, D_SSM)),
            full((1, D_SSM)), full((1, D_SSM)),
        ],
        out_specs=[
            pl.BlockSpec((None, t_new, D_SSM), lambda b: (b, 0, 0)),
            pl.BlockSpec((None, SSM_HEADS, SSM_HEADDIM, D_STATE), lambda b: (b, 0, 0, 0)),
            pl.BlockSpec((None, CONV_W - 1, D_XBC), lambda b: (b, 0, 0)),
        ],
        out_shape=[
            jax.ShapeDtypeStruct((n_seq, t_new, D_SSM), BF16),
            jax.ShapeDtypeStruct((n_seq, SSM_HEADS, SSM_HEADDIM, D_STATE), F32),
            jax.ShapeDtypeStruct((n_seq, CONV_W - 1, D_XBC), F32),
        ],
        scratch_shapes=[pltpu.VMEM((8, D_XBC), F32)],
        compiler_params=_cparams(1),
        name="ssd_decode",
    )(zs, zs, zs, dtx, state_conv, state_ssm, cw, cb, dtbx, alogx, dskx, nw)


def _layer_norm(x, w, b):
    mu = jnp.mean(x, axis=-1, keepdims=True)
    xc = x - mu
    var = jnp.mean(xc * xc, axis=-1, keepdims=True)
    return xc * lax.rsqrt(var + LN_EPS) * w + b


def _top2_of(vals, col, n):
    m1 = jnp.max(vals, axis=-1, keepdims=True)
    i1 = jnp.min(jnp.where(vals == m1, col, n), axis=-1, keepdims=True)
    rest = jnp.where(col == i1, -jnp.inf, vals)
    m2 = jnp.max(rest, axis=-1, keepdims=True)
    i2 = jnp.min(jnp.where(rest == m2, col, n), axis=-1, keepdims=True)
    return m1, i1, m2, i2


def _router_gates(logits, b_router):
    scores = jax.nn.sigmoid(logits)
    biased = scores + b_router
    col = lax.broadcasted_iota(jnp.int32, logits.shape, 1)
    grp = col // EXPERTS_PER_GROUP
    best = jnp.zeros((logits.shape[0], 1), jnp.int32)
    best_v = None
    for g in range(N_EXPERT_GROUPS):
        m1, _, m2, _ = _top2_of(jnp.where(grp == g, biased, -jnp.inf), col, N_EXPERTS)
        gs = m1 + m2
        if g == 0:
            best_v = gs
        else:
            upd = gs > best_v
            best = jnp.where(upd, g, best)
            best_v = jnp.where(upd, gs, best_v)
    _, i1, _, i2 = _top2_of(jnp.where(grp == best, biased, -jnp.inf), col, N_EXPERTS)
    w = jnp.where((col == i1) | (col == i2), scores, 0.0)
    return w / jnp.sum(w, axis=-1, keepdims=True)


def _post1_kernel(x_ref, oa_ref, ob_ref, oc_ref, wa_ref, wb_ref, wc_ref, lw_ref, lb_ref, wr_ref, br_ref,
                  x1_ref, x1b_ref, g_ref, *, alpha):
    mix = (jnp.dot(oa_ref[...], wa_ref[...], preferred_element_type=F32)
           + jnp.dot(ob_ref[...], wb_ref[...], preferred_element_type=F32)
           + jnp.dot(oc_ref[...], wc_ref[...], preferred_element_type=F32))
    x1 = _layer_norm(alpha * x_ref[...] + mix, lw_ref[...], lb_ref[...])
    x1_ref[...] = x1
    x1b_ref[...] = x1.astype(BF16)
    logits = jnp.dot(x1, wr_ref[...], preferred_element_type=F32, precision=lax.Precision.HIGHEST)
    g_ref[...] = _router_gates(logits, br_ref[...])


def _post1(x, oa, ob, oc, w_out_b, lw, lb, w_router, b_router, *, alpha, tm=256):
    n, d = x.shape
    full = lambda shape: pl.BlockSpec(shape, lambda i: (0,) * len(shape))
    row = lambda width: pl.BlockSpec((tm, width), lambda i: (i, 0))
    wa, wb = DA_HEADS * HEAD_DIM, MB_HEADS * HEAD_DIM
    return pl.pallas_call(
        functools.partial(_post1_kernel, alpha=alpha),
        grid=(n // tm,),
        in_specs=[
            row(d), row(wa), row(wb), row(D_SSM),
            pl.BlockSpec((wa, d), lambda i: (0, 0)), pl.BlockSpec((wb, d), lambda i: (1, 0)),
            pl.BlockSpec((D_SSM, d), lambda i: (1, 0)),
            full((1, d)), full((1, d)), full((d, N_EXPERTS)), full((1, N_EXPERTS)),
        ],
        out_specs=[row(d), row(d), row(N_EXPERTS)],
        out_shape=[jax.ShapeDtypeStruct((n, d), F32), jax.ShapeDtypeStruct((n, d), BF16),
                   jax.ShapeDtypeStruct((n, N_EXPERTS), F32)],
        compiler_params=_cparams(1),
        name="out_proj_ln1_router",
    )(x, oa, ob, oc, w_out_b, w_out_b, w_out_b, lw, lb, w_router, b_router)


def _moe_kernel(x_ref, g_ref, wg_ref, wu_ref, wd_ref, o_ref):
    e = pl.program_id(1)
    x = x_ref[...]
    hid = _silu(jnp.dot(x, wg_ref[...], preferred_element_type=F32)) * jnp.dot(x, wu_ref[...],
                                                                                preferred_element_type=F32)
    col = lax.broadcasted_iota(jnp.int32, g_ref.shape, 1)
    gate = jnp.sum(jnp.where(col == e, g_ref[...], 0.0), axis=-1, keepdims=True)
    part = jnp.dot((hid * gate).astype(BF16), wd_ref[...], preferred_element_type=F32)

    @pl.when(e == 0)
    def _():
        o_ref[...] = part

    @pl.when(e > 0)
    def _():
        o_ref[...] += part


def _moe(x1b, gates, wg, wu, wd, *, tm):
    n, d = x1b.shape
    f = wg.shape[-1]
    return pl.pallas_call(
        _moe_kernel,
        grid=(n // tm, N_EXPERTS),
        in_specs=[
            pl.BlockSpec((tm, d), lambda i, e: (i, 0)),
            pl.BlockSpec((tm, N_EXPERTS), lambda i, e: (i, 0)),
            pl.BlockSpec((None, d, f), lambda i, e: (e, 0, 0)),
            pl.BlockSpec((None, d, f), lambda i, e: (e, 0, 0)),
            pl.BlockSpec((None, f, d), lambda i, e: (e, 0, 0)),
        ],
        out_specs=pl.BlockSpec((tm, d), lambda i, e: (i, 0)),
        out_shape=jax.ShapeDtypeStruct((n, d), F32),
        compiler_params=_cparams(2),
        name="moe_dense",
    )(x1b, gates, wg, wu, wd)


def _post2_kernel(x1_ref, x1b_ref, ff_ref, p_ref, wgate_ref, wproj_ref, lw_ref, lb_ref, x2_ref, x2b_ref, *, alpha):
    gate = jax.nn.sigmoid(jnp.dot(x1b_ref[...], wgate_ref[...], preferred_element_type=F32))
    ple = gate * jnp.dot(p_ref[...], wproj_ref[...], preferred_element_type=F32)
    x2 = _layer_norm(alpha * x1_ref[...] + ff_ref[...] + ple, lw_ref[...], lb_ref[...])
    x2_ref[...] = x2
    x2b_ref[...] = x2.astype(BF16)


def _post2(x1, x1b, ff, p, w_gate_b, w_proj_b, lw, lb, *, alpha, tm=256):
    n, d = x1.shape
    dp = p.shape[1]
    full = lambda shape: pl.BlockSpec(shape, lambda i: (0,) * len(shape))
    row = lambda width: pl.BlockSpec((tm, width), lambda i: (i, 0))
    return pl.pallas_call(
        functools.partial(_post2_kernel, alpha=alpha),
        grid=(n // tm,),
        in_specs=[row(d), row(d), row(d), row(dp), full((d, d)), full((dp, d)), full((1, d)), full((1, d))],
        out_specs=[row(d), row(d)],
        out_shape=[jax.ShapeDtypeStruct((n, d), F32), jax.ShapeDtypeStruct((n, d), BF16)],
        compiler_params=_cparams(1),
        name="ple_ln2",
    )(x1, x1b, ff, p, w_gate_b, w_proj_b, lw, lb)


def kernel(x_prompt, x_sample, p_prompt, p_sample, cache_diff_k, cache_diff_v, cache_moba_k, cache_moba_v, state_ssm, state_conv, page_table, w_in, w_out, diff_lambda, diff_norm_w, ssm_conv_w, ssm_conv_b, ssm_dt_bias, ssm_a_log, ssm_d, ssm_norm_w, ln1_w, ln1_b, ln2_w, ln2_b, w_router, b_router, w_exp_gate, w_exp_up, w_exp_down, w_ple_proj, w_ple_gate):
    depth = w_in.shape[0]
    batch, seq, d_model = x_prompt.shape
    n_seq, t_new, _ = x_sample.shape
    n_p, n_s = batch * seq, n_seq * t_new
    alpha = (2 * depth) ** 0.25
    n_heads = DA_HEADS + MB_HEADS
    slopes = jnp.asarray(2.0 ** (-8.0 * np.arange(1, n_heads + 1) / n_heads), dtype=F32)

    def score_col_slopes(sl, cols_per_head):
        v = jnp.repeat(sl, cols_per_head)
        return jnp.pad(v, (0, LANES - v.shape[0])).reshape(1, LANES)

    slopes_diff_cols = score_col_slopes(slopes[:DA_HEADS], 2 * t_new)
    slopes_moba_cols = score_col_slopes(slopes[DA_HEADS:], t_new)
    pad_lanes = lambda v: jnp.pad(v, (0, LANES - v.shape[0])).reshape(1, LANES)
    per_channel = lambda v: jnp.repeat(v, SSM_HEADDIM).reshape(1, D_SSM)

    n_pool, page = cache_diff_k.shape[1], cache_diff_k.shape[2]
    flat_cache = lambda c: c.reshape(depth, n_pool, page, DA_HEADS * HEAD_DIM)
    ck_a, cv_a, ck_b, cv_b = map(flat_cache, (cache_diff_k, cache_diff_v, cache_moba_k, cache_moba_v))

    x = jnp.concatenate([x_prompt.reshape(n_p, d_model), x_sample.reshape(n_s, d_model)], axis=0)
    xb = x.astype(BF16)
    outs_p = [[] for _ in range(6)]
    outs_s = [[] for _ in range(6)]
    for i in range(depth):
        lam_init = 0.8 - 0.6 * math.exp(-0.3 * i)
        w_in_b = jnp.pad(w_in[i].astype(BF16), ((0, 0), (0, D_IN_PAD - D_IN_PROJ)))
        z = _matmul(xb, w_in_b, tm=512, tn=D_IN_PAD // 3)
        zs = z[n_p:].reshape(n_seq, t_new, D_IN_PAD)
        w_dtx = jnp.repeat(w_in[i][:, DT_COL:DT_COL + SSM_HEADS].astype(BF16), SSM_HEADDIM, axis=1)
        dtx = _matmul(xb[n_p:], w_dtx, tm=n_s, tn=D_SSM).reshape(n_seq, t_new, D_SSM)

        dl = diff_lambda[i]
        nw_a = diff_norm_w[i].reshape(1, HEAD_DIM)
        cw, cb = ssm_conv_w[i], ssm_conv_b[i].reshape(1, D_XBC)
        nw_c = ssm_norm_w[i].reshape(1, D_SSM)

        oa_p = _diff_prompt(z, slopes, dl, nw_a, batch=batch, seq=seq, lam_init=lam_init)
        ob_p = _moba_prompt(z, slopes, batch=batch, seq=seq)
        oc_p, h_p = _ssd_prompt(z, cw, cb, pad_lanes(ssm_dt_bias[i]), pad_lanes(ssm_a_log[i]),
                                per_channel(ssm_d[i]), nw_c, batch=batch, seq=seq)
        oa_s = _attn_decode(zs, ck_a, cv_a, page_table, slopes_diff_cols, dl, nw_a, layer=i, moba=False,
                            lam_init=lam_init)
        ob_s = _attn_decode(zs, ck_b, cv_b, page_table, slopes_moba_cols, dl, nw_a, layer=i, moba=True,
                            lam_init=lam_init)
        oc_s, h_s, conv_s = _ssd_decode(zs, dtx, state_conv, state_ssm, cw, cb, per_channel(ssm_dt_bias[i]),
                                        per_channel(ssm_a_log[i]), per_channel(ssm_d[i]), nw_c, layer=i)

        head_shape = lambda a, lead: a.reshape(*lead, DA_HEADS, HEAD_DIM)
        zp = z[:n_p]
        for j, blk in enumerate((KA_BLK, VA_BLK, KB_BLK, VB_BLK)):
            c0 = blk * LANES
            outs_p[j].append(head_shape(zp[:, c0:c0 + DA_HEADS * HEAD_DIM], (batch, seq)))
            outs_s[j].append(head_shape(zs[:, :, c0:c0 + DA_HEADS * HEAD_DIM], (n_seq, t_new)))
        outs_p[4].append(h_p)
        outs_p[5].append(zp.reshape(batch, seq, D_IN_PAD)[:, seq - (CONV_W - 1):, XS_COL:XS_COL + D_XBC])
        outs_s[4].append(h_s)
        outs_s[5].append(conv_s)

        oa = jnp.concatenate([oa_p, oa_s.reshape(n_s, -1)], axis=0)
        ob = jnp.concatenate([ob_p, ob_s.reshape(n_s, -1)], axis=0)
        oc = jnp.concatenate([oc_p, oc_s.reshape(n_s, -1)], axis=0)
        x1, x1b, gates = _post1(x, oa, ob, oc, w_out[i].astype(BF16), ln1_w[i].reshape(1, -1),
                                ln1_b[i].reshape(1, -1), w_router, b_router.reshape(1, -1), alpha=alpha)
        ff = _moe(x1b, gates, w_exp_gate[i].astype(BF16), w_exp_up[i].astype(BF16), w_exp_down[i].astype(BF16),
                  tm=(n_p + n_s) // 8)
        p = jnp.concatenate([p_prompt[i].reshape(n_p, -1), p_sample[i].reshape(n_s, -1)], axis=0).astype(BF16)
        x, xb = _post2(x1, x1b, ff, p, w_ple_gate[i].astype(BF16), w_ple_proj[i].astype(BF16),
                       ln2_w[i].reshape(1, -1), ln2_b[i].reshape(1, -1), alpha=alpha)

    y_p = x[:n_p].reshape(batch, seq, d_model)
    y_s = x[n_p:].reshape(n_seq, t_new, d_model)
    return (y_p, y_s, *[jnp.stack(o) for o in outs_p], *[jnp.stack(o) for o in outs_s])
```

```python
import functools
import math

import jax
import jax.numpy as jnp
import numpy as np
from jax import lax
from jax.experimental import pallas as pl
from jax.experimental.pallas import tpu as pltpu

F32 = jnp.float32
BF16 = jnp.bfloat16

LANES = 128
HEAD_DIM = 128
DA_HEADS = 4
DA_QK = 64
MB_HEADS = 4
MOBA_BLOCK = 256
MOBA_TOPK = 3
D_SSM = 1024
SSM_HEADS = 16
SSM_HEADDIM = 64
SSM_GROUPS = 2
D_STATE = 128
CONV_W = 4
D_XBC = D_SSM + 2 * SSM_GROUPS * D_STATE
N_EXPERTS = 16
N_EXPERT_GROUPS = 4
EXPERTS_PER_GROUP = 4
LN_EPS = 1e-5
RMS_EPS = 1e-6
VMEM_LIMIT = 56 * 1024 * 1024

QA_BLK, KA_BLK, VA_BLK, QB_BLK, KB_BLK, VB_BLK = 0, 4, 8, 12, 16, 20
ZC_COL, XS_COL, BC_COL, DT_COL = 3072, 4096, 5120, 5632
D_IN_PROJ = 5648
D_IN_PAD = 5760

NT_DIMS = (((1,), (1,)), ((), ()))
TN_DIMS = (((0,), (0,)), ((), ()))


def _cparams(n_axes):
    return pltpu.CompilerParams(dimension_semantics=("arbitrary",) * n_axes, vmem_limit_bytes=VMEM_LIMIT)


def _mm_kernel(x_ref, w_ref, o_ref):
    o_ref[...] = jnp.dot(x_ref[...], w_ref[...], preferred_element_type=F32).astype(o_ref.dtype)


def _matmul(x, w, tm, tn, out_dtype=F32):
    m, k = x.shape
    n = w.shape[1]
    return pl.pallas_call(
        _mm_kernel,
        grid=(n // tn, m // tm),
        in_specs=[pl.BlockSpec((tm, k), lambda j, i: (i, 0)), pl.BlockSpec((k, tn), lambda j, i: (0, j))],
        out_specs=pl.BlockSpec((tm, tn), lambda j, i: (i, j)),
        out_shape=jax.ShapeDtypeStruct((m, n), out_dtype),
        compiler_params=_cparams(2),
        name="in_proj",
    )(x, w)


POS_SPLIT = 64
ONEHOT_LANE0 = 8
MASK_BIAS = -1e30


ATT_TILE = 256
N_ATT_HEADS = DA_HEADS + MB_HEADS


def _kv_prep_kernel(k_ref, v_ref, ka_ref, vt_ref, km_ref, *, ck):
    pos = pl.program_id(2) * ck + lax.broadcasted_iota(jnp.int32, (ck, LANES), 0)
    lane = lax.broadcasted_iota(jnp.int32, (ck, LANES), 1)
    feat = jnp.where(lane == 0, pos // POS_SPLIT, jnp.where(lane == 1, pos % POS_SPLIT, jnp.where(lane == 2, 1, 0)))
    feat = jnp.where(lane - ONEHOT_LANE0 == pos // MOBA_BLOCK, 1, feat)
    ka_ref[:, :HEAD_DIM] = k_ref[...].astype(BF16)
    ka_ref[:, HEAD_DIM:] = feat.astype(F32).astype(BF16)
    for j in range(ck // ATT_TILE):
        rows = slice(j * ATT_TILE, (j + 1) * ATT_TILE)
        vt_ref[j] = v_ref[rows, :].T.astype(BF16)
        km_ref[j:j + 1, :] = jnp.mean(k_ref[rows, :], axis=0, keepdims=True)


def _kv_prep(z, *, batch, seq, ck=2048):
    nc = seq // ck
    tiles = ck // ATT_TILE
    k_blk = lambda h: KA_BLK + h + (KB_BLK - KA_BLK - DA_HEADS) * (h // DA_HEADS)
    v_blk = lambda h: VA_BLK + h + (VB_BLK - VA_BLK - DA_HEADS) * (h // DA_HEADS)
    return pl.pallas_call(
        functools.partial(_kv_prep_kernel, ck=ck),
        grid=(batch, N_ATT_HEADS, nc),
        in_specs=[
            pl.BlockSpec((ck, HEAD_DIM), lambda b, h, c: (b * nc + c, k_blk(h))),
            pl.BlockSpec((ck, HEAD_DIM), lambda b, h, c: (b * nc + c, v_blk(h))),
        ],
        out_specs=[
            pl.BlockSpec((None, None, ck, 2 * HEAD_DIM), lambda b, h, c: (b, h, c, 0)),
            pl.BlockSpec((None, None, tiles, HEAD_DIM, ATT_TILE), lambda b, h, c: (b, h, c, 0, 0)),
            pl.BlockSpec((None, None, tiles, HEAD_DIM), lambda b, h, c: (b, h, c, 0)),
        ],
        out_shape=[
            jax.ShapeDtypeStruct((batch, N_ATT_HEADS, seq, 2 * HEAD_DIM), BF16),
            jax.ShapeDtypeStruct((batch, N_ATT_HEADS, seq // ATT_TILE, HEAD_DIM, ATT_TILE), BF16),
            jax.ShapeDtypeStruct((batch, N_ATT_HEADS, seq // ATT_TILE, HEAD_DIM), F32),
        ],
        compiler_params=_cparams(3),
        name="kv_prep",
    )(z, z)


def _query_feature_rows(slope, q0, n_cols):
    row = lax.broadcasted_iota(jnp.int32, (8, n_cols), 0)
    c2 = -(slope * q0.astype(F32))
    return jnp.where(row == 0, slope * POS_SPLIT, jnp.where(row == 1, slope, jnp.where(row == 2, c2, 0.0)))


def _softmax_t_step(n_heads, kj, ka_ref, vt_ref, qa_sc, m_sc, l_sc, acc_sc, bias=None):
    t = ATT_TILE
    rows = pl.ds(pl.multiple_of(kj * t, t), t)
    scores = [jnp.dot(ka_ref[h, rows, :], qa_sc[h], preferred_element_type=F32) for h in range(n_heads)]
    probs = []
    for h in range(n_heads):
        s = scores[h] if bias is None else scores[h] + bias
        m = m_sc[h]
        m_new = jnp.maximum(m, jnp.max(s, axis=0, keepdims=True))
        alpha = jnp.exp(m - m_new)
        p = jnp.exp(s - m_new)
        l_sc[h] = alpha * l_sc[h] + jnp.sum(p, axis=0, keepdims=True)
        m_sc[h] = m_new
        probs.append((alpha, p.astype(BF16)))
    for h in range(n_heads):
        alpha, p = probs[h]
        acc_sc[h] = alpha * acc_sc[h] + jnp.dot(vt_ref[h, kj], p, preferred_element_type=F32)


def _softmax_t_run(n_heads, own, ka_ref, vt_ref, mask_ref, qa_sc, m_sc, l_sc, acc_sc):
    m_sc[...] = jnp.full(m_sc.shape, -jnp.inf, F32)
    l_sc[...] = jnp.zeros(l_sc.shape, F32)
    acc_sc[...] = jnp.zeros(acc_sc.shape, F32)
    _softmax_t_step(n_heads, own, ka_ref, vt_ref, qa_sc, m_sc, l_sc, acc_sc, bias=mask_ref[...])

    def body(kj, carry):
        _softmax_t_step(n_heads, kj, ka_ref, vt_ref, qa_sc, m_sc, l_sc, acc_sc)
        return carry

    lax.fori_loop(0, own, body, 0)


def _causal_mask_bias(n_maps):
    t = ATT_TILE
    k_i = lax.broadcasted_iota(jnp.int32, (t, n_maps * t), 0)
    q_i = lax.broadcasted_iota(jnp.int32, (t, n_maps * t), 1) % t
    return jnp.where(k_i <= q_i, 0.0, -jnp.inf).astype(F32)


def _diff_prompt_kernel(slopes_ref, dl_ref, q_ref, ka_ref, vt_ref, mask_ref, nw_ref, o_ref,
                        qa_sc, m_sc, l_sc, acc_sc, *, lam_init):
    t = ATT_TILE
    qi = pl.program_id(1)
    q0 = qi * t
    n_cols = 2 * t
    for h in range(DA_HEADS):
        qt = q_ref[:, h * HEAD_DIM:(h + 1) * HEAD_DIM].T * (DA_QK ** -0.5)
        row = lax.broadcasted_iota(jnp.int32, qt.shape, 0)
        q_top = jnp.concatenate([jnp.where(row < DA_QK, qt, 0.0), jnp.where(row >= DA_QK, qt, 0.0)], axis=1)
        q_feat = jnp.concatenate([_query_feature_rows(slopes_ref[h], q0, n_cols),
                                  jnp.zeros((LANES - 8, n_cols), F32)], axis=0)
        qa_sc[h] = jnp.concatenate([q_top, q_feat], axis=0).astype(BF16)

    _softmax_t_run(DA_HEADS, qi, ka_ref, vt_ref, mask_ref, qa_sc, m_sc, l_sc, acc_sc)

    dl = dl_ref[...]
    lam = (jnp.exp(jnp.sum(dl[0:1] * dl[1:2], axis=-1, keepdims=True))
           - jnp.exp(jnp.sum(dl[2:3] * dl[3:4], axis=-1, keepdims=True)) + lam_init)
    for h in range(DA_HEADS):
        o2 = acc_sc[h] / l_sc[h]
        ot = o2[:, :t] - lam * o2[:, t:]
        var = jnp.mean(ot * ot, axis=0, keepdims=True)
        o = (ot * lax.rsqrt(var + RMS_EPS)).T * nw_ref[...] * (1.0 - lam_init)
        o_ref[:, h * HEAD_DIM:(h + 1) * HEAD_DIM] = o.astype(o_ref.dtype)


def _diff_prompt(z, ka, vt, slopes, dl, nw, *, batch, seq, lam_init):
    t = ATT_TILE
    nq = seq // t
    width = DA_HEADS * HEAD_DIM
    n_cols = 2 * t
    kern = functools.partial(_diff_prompt_kernel, lam_init=lam_init)
    return pl.pallas_call(
        kern,
        grid=(batch, nq),
        in_specs=[
            pl.BlockSpec(memory_space=pltpu.SMEM),
            pl.BlockSpec((4, DA_QK), lambda b, i: (0, 0)),
            pl.BlockSpec((t, width), lambda b, i: (b * nq + i, QA_BLK // DA_HEADS)),
            pl.BlockSpec((None, DA_HEADS, seq, 2 * HEAD_DIM), lambda b, i: (b, 0, 0, 0)),
            pl.BlockSpec((None, DA_HEADS, nq, HEAD_DIM, t), lambda b, i: (b, 0, 0, 0, 0)),
            pl.BlockSpec((t, n_cols), lambda b, i: (0, 0)),
            pl.BlockSpec((1, HEAD_DIM), lambda b, i: (0, 0)),
        ],
        out_specs=pl.BlockSpec((t, width), lambda b, i: (b * nq + i, 0)),
        out_shape=jax.ShapeDtypeStruct((batch * seq, width), BF16),
        scratch_shapes=[
            pltpu.VMEM((DA_HEADS, 2 * HEAD_DIM, n_cols), BF16),
            pltpu.VMEM((DA_HEADS, 1, n_cols), F32), pltpu.VMEM((DA_HEADS, 1, n_cols), F32),
            pltpu.VMEM((DA_HEADS, HEAD_DIM, n_cols), F32),
        ],
        compiler_params=_cparams(2),
        name="diff_prompt",
    )(slopes, dl, z, ka, vt, _causal_mask_bias(2), nw)


def _top3_mask_rows(gate_t, n_valid, own):
    nb = gate_t.shape[0]
    row = lax.broadcasted_iota(jnp.int32, gate_t.shape, 0)
    g = jnp.where(row < n_valid, gate_t, -jnp.inf)
    sel = row == own
    for _ in range(MOBA_TOPK):
        mx = jnp.max(g, axis=0, keepdims=True)
        idx = jnp.min(jnp.where(g == mx, row, nb), axis=0, keepdims=True)
        pick = (row == idx) & (mx > -jnp.inf)
        sel = sel | pick
        g = jnp.where(pick, -jnp.inf, g)
    return jnp.where(sel, 0.0, MASK_BIAS)


def _moba_prompt_kernel(slopes_ref, q_ref, ka_ref, vt_ref, km_ref, mask_ref, o_ref,
                        qa_sc, m_sc, l_sc, acc_sc, *, nb):
    blk = MOBA_BLOCK
    own = pl.program_id(1)
    q0 = own * blk
    for h in range(MB_HEADS):
        qt = q_ref[:, h * HEAD_DIM:(h + 1) * HEAD_DIM].T
        gate_t = jnp.dot(km_ref[h], qt, preferred_element_type=F32, precision=lax.Precision.HIGHEST)
        q_feat = jnp.concatenate([_query_feature_rows(slopes_ref[DA_HEADS + h], q0, blk),
                                  _top3_mask_rows(gate_t, own, own),
                                  jnp.zeros((LANES - ONEHOT_LANE0 - nb, blk), F32)], axis=0)
        qa_sc[h] = jnp.concatenate([qt * (HEAD_DIM ** -0.5), q_feat], axis=0).astype(BF16)

    _softmax_t_run(MB_HEADS, own, ka_ref, vt_ref, mask_ref, qa_sc, m_sc, l_sc, acc_sc)

    for h in range(MB_HEADS):
        o_ref[:, h * HEAD_DIM:(h + 1) * HEAD_DIM] = (acc_sc[h] / l_sc[h]).T.astype(o_ref.dtype)


def _moba_prompt(z, ka, vt, kmean, slopes, *, batch, seq):
    blk = MOBA_BLOCK
    assert blk == ATT_TILE
    nb = seq // blk
    assert nb <= LANES - ONEHOT_LANE0 and nb % 8 == 0
    width = MB_HEADS * HEAD_DIM
    kern = functools.partial(_moba_prompt_kernel, nb=nb)
    return pl.pallas_call(
        kern,
        grid=(batch, nb),
        in_specs=[
            pl.BlockSpec(memory_space=pltpu.SMEM),
            pl.BlockSpec((blk, width), lambda b, i: (b * nb + i, QB_BLK // MB_HEADS)),
            pl.BlockSpec((None, MB_HEADS, seq, 2 * HEAD_DIM), lambda b, i: (b, 1, 0, 0)),
            pl.BlockSpec((None, MB_HEADS, nb, HEAD_DIM, blk), lambda b, i: (b, 1, 0, 0, 0)),
            pl.BlockSpec((None, MB_HEADS, nb, HEAD_DIM), lambda b, i: (b, 1, 0, 0)),
            pl.BlockSpec((blk, blk), lambda b, i: (0, 0)),
        ],
        out_specs=pl.BlockSpec((blk, width), lambda b, i: (b * nb + i, 0)),
        out_shape=jax.ShapeDtypeStruct((batch * seq, width), BF16),
        scratch_shapes=[
            pltpu.VMEM((MB_HEADS, 2 * HEAD_DIM, blk), BF16),
            pltpu.VMEM((MB_HEADS, 1, blk), F32), pltpu.VMEM((MB_HEADS, 1, blk), F32),
            pltpu.VMEM((MB_HEADS, HEAD_DIM, blk), F32),
        ],
        compiler_params=_cparams(2),
        name="moba_prompt",
    )(slopes, z, ka, vt, kmean, _causal_mask_bias(1))


def _softplus(x):
    return jnp.maximum(x, 0.0) + jnp.log1p(jnp.exp(-jnp.abs(x)))


def _silu(x):
    return x * jax.nn.sigmoid(x)


def _conv_silu(xp_sc, w, b, t):
    y = b
    for j in range(CONV_W):
        y = y + xp_sc[5 + j:5 + j + t, :] * w[j:j + 1, :]
    return _silu(y)


def _ssd_prompt_kernel(zc_ref, x_ref, bc_ref, dt_ref, cw_ref, cb_ref, dtb_ref, alog_ref, dsk_ref, nw_ref,
                       y_ref, hfin_ref, xpx_sc, xpbc_sc, h_sc, y_sc, *, t):
    c = pl.program_id(1)
    nc = pl.num_programs(1)
    hp = LANES // SSM_HEADDIM
    hpg = SSM_HEADS // SSM_GROUPS

    @pl.when(c == 0)
    def _():
        xpx_sc[0:8, :] = jnp.zeros((8, D_SSM), F32)
        xpbc_sc[0:8, :] = jnp.zeros((8, D_XBC - D_SSM), F32)
        h_sc[...] = jnp.zeros(h_sc.shape, F32)

    xpx_sc[8:8 + t, :] = x_ref[...]
    xpbc_sc[8:8 + t, :] = bc_ref[...]
    cw = cw_ref[...]
    cb = cb_ref[...]
    xs = _conv_silu(xpx_sc, cw[:, :D_SSM], cb[:, :D_SSM], t)
    bc = _conv_silu(xpbc_sc, cw[:, D_SSM:], cb[:, D_SSM:], t)
    xpx_sc[5:8, :] = xpx_sc[t + 5:t + 8, :]
    xpbc_sc[5:8, :] = xpbc_sc[t + 5:t + 8, :]

    dt = _softplus(dt_ref[...] + dtb_ref[...])
    da = dt * (-jnp.exp(alog_ref[...]))
    r_i = lax.broadcasted_iota(jnp.int32, (t, t), 0)
    c_i = lax.broadcasted_iota(jnp.int32, (t, t), 1)
    tril = c_i <= r_i
    acs = jnp.dot(tril.astype(F32), da, preferred_element_type=F32, precision=lax.Precision.HIGHEST)
    acs_t = acs.T
    lane = lax.broadcasted_iota(jnp.int32, (t, LANES), 1)
    sub = lax.broadcasted_iota(jnp.int32, (LANES, LANES), 0)
    lo_lane = lane < SSM_HEADDIM
    lo_sub = sub < SSM_HEADDIM

    for g in range(SSM_GROUPS):
        bg = bc[:, g * D_STATE:(g + 1) * D_STATE]
        cg = bc[:, (SSM_GROUPS + g) * D_STATE:(SSM_GROUPS + g + 1) * D_STATE]
        bgb = bg.astype(BF16)
        cgb = cg.astype(BF16)
        gmat = lax.dot_general(cgb, bgb, NT_DIMS, preferred_element_type=F32)
        for pr in range(hpg // hp):
            ha = g * hpg + pr * hp
            cb0 = ha * SSM_HEADDIM
            x2 = xs[:, cb0:cb0 + LANES]
            dt2 = jnp.where(lo_lane, dt[:, ha:ha + 1], dt[:, ha + 1:ha + 2])
            xdt = x2 * dt2
            ydiag = jnp.zeros((t, LANES), F32)
            for u in range(hp):
                hh = ha + u
                seg = acs[:, hh:hh + 1] - acs_t[hh:hh + 1, :]
                lmat = jnp.exp(jnp.where(tril, seg, -jnp.inf))
                xu = jnp.where(lo_lane if u == 0 else ~lo_lane, xdt, 0.0)
                ydiag = ydiag + jnp.dot((gmat * lmat).astype(BF16), xu.astype(BF16), preferred_element_type=F32)
            h2 = h_sc[ha:ha + hp].reshape(LANES, D_STATE)
            yoff = lax.dot_general(cgb, h2.astype(BF16), NT_DIMS, preferred_element_type=F32)
            e2 = jnp.where(lo_lane, jnp.exp(acs[:, ha:ha + 1]), jnp.exp(acs[:, ha + 1:ha + 2]))
            y = ydiag + e2 * yoff + dsk_ref[:, cb0:cb0 + LANES] * x2
            y_sc[:, cb0:cb0 + LANES] = y * _silu(zc_ref[:, cb0:cb0 + LANES])
            last = acs[t - 1:t, :]
            te2 = jnp.where(lo_lane, jnp.exp(last[:, ha:ha + 1] - acs[:, ha:ha + 1]),
                            jnp.exp(last[:, ha + 1:ha + 2] - acs[:, ha + 1:ha + 2]))
            upd = lax.dot_general((xdt * te2).astype(BF16), bgb, TN_DIMS, preferred_element_type=F32)
            dec = jnp.where(lo_sub, jnp.exp(last[:, ha:ha + 1]), jnp.exp(last[:, ha + 1:ha + 2]))
            h_sc[ha:ha + hp] = (dec * h2 + upd).reshape(hp, SSM_HEADDIM, D_STATE)

    gw = D_SSM // SSM_GROUPS
    for g in range(SSM_GROUPS):
        yg = y_sc[:, g * gw:(g + 1) * gw]
        var = jnp.mean(yg * yg, axis=-1, keepdims=True)
        y_ref[:, g * gw:(g + 1) * gw] = (yg * lax.rsqrt(var + RMS_EPS) * nw_ref[:, g * gw:(g + 1) * gw]).astype(y_ref.dtype)

    @pl.when(c == nc - 1)
    def _():
        hfin_ref[0] = h_sc[...]


def _ssd_prompt(z, cw, cb, dtb, alog, dsk, nw, *, batch, seq, t=128):
    nc = seq // t
    kern = functools.partial(_ssd_prompt_kernel, t=t)
    full = lambda shape: pl.BlockSpec(shape, lambda b, c: (0,) * len(shape))
    return pl.pallas_call(
        kern,
        grid=(batch, nc),
        in_specs=[
            pl.BlockSpec((t, D_SSM), lambda b, c: (b * nc + c, ZC_COL // D_SSM)),
            pl.BlockSpec((t, D_SSM), lambda b, c: (b * nc + c, XS_COL // D_SSM)),
            pl.BlockSpec((t, 512), lambda b, c: (b * nc + c, BC_COL // 512)),
            pl.BlockSpec((t, LANES), lambda b, c: (b * nc + c, DT_COL // LANES)),
            full((CONV_W, D_XBC)), full((1, D_XBC)), full((1, LANES)), full((1, LANES)),
            full((1, D_SSM)), full((1, D_SSM)),
        ],
        out_specs=[
            pl.BlockSpec((t, D_SSM), lambda b, c: (b * nc + c, 0)),
            pl.BlockSpec((1, SSM_HEADS, SSM_HEADDIM, D_STATE), lambda b, c: (b, 0, 0, 0)),
        ],
        out_shape=[
            jax.ShapeDtypeStruct((batch * seq, D_SSM), BF16),
            jax.ShapeDtypeStruct((batch, SSM_HEADS, SSM_HEADDIM, D_STATE), F32),
        ],
        scratch_shapes=[
            pltpu.VMEM((t + 8, D_SSM), F32), pltpu.VMEM((t + 8, D_XBC - D_SSM), F32),
            pltpu.VMEM((SSM_HEADS, SSM_HEADDIM, D_STATE), F32), pltpu.VMEM((t, D_SSM), F32),
        ],
        compiler_params=_cparams(2),
        name="ssd_prompt",
    )(z, z, z, z, cw, cb, dtb, alog, dsk, nw)


NEW_ROWS = 16


def _attn_decode_kernel(*refs, moba, n_pages, page, t_new, lam_init):
    pt_ref = refs[0]
    del pt_ref
    slopes_ref, dl_ref, nw_ref, q_ref, kn_ref, vn_ref = refs[1:7]
    kp_refs = refs[7:7 + n_pages]
    vp_refs = refs[7 + n_pages:7 + 2 * n_pages]
    o_ref = refs[7 + 2 * n_pages]
    kb_sc, vb_sc, new_sc = refs[8 + 2 * n_pages:]
    past = n_pages * page
    width = DA_HEADS * HEAD_DIM
    cols_per_head = t_new if moba else 2 * t_new
    lanes_per_col_grp = HEAD_DIM if moba else DA_QK
    scale = HEAD_DIM ** -0.5 if moba else DA_QK ** -0.5

    ksum = []
    for p in range(n_pages):
        ksum_p = []
        for hd in range(DA_HEADS):
            lo = hd * HEAD_DIM
            kh = kp_refs[p][pl.ds(hd, page, stride=DA_HEADS), :]
            kb_sc[p * page:(p + 1) * page, lo:lo + HEAD_DIM] = kh.astype(BF16)
            vb_sc[p * page:(p + 1) * page, lo:lo + HEAD_DIM] = vp_refs[p][pl.ds(hd, page, stride=DA_HEADS), :].astype(BF16)
            if moba:
                ksum_p.append(jnp.sum(kh, axis=0, keepdims=True))
        if moba:
            ksum.append(jnp.concatenate(ksum_p, axis=1))
    for src, dst in ((kn_ref, kb_sc), (vn_ref, vb_sc)):
        new_sc[...] = jnp.zeros(new_sc.shape, F32)
        new_sc[0:t_new, :] = src[...]
        dst[past:past + NEW_ROWS, :] = new_sc[...].astype(BF16)

    q = q_ref[...]
    qt = jnp.concatenate([q] * (LANES // t_new), axis=0)
    r_i = lax.broadcasted_iota(jnp.int32, qt.shape, 0)
    l_i = lax.broadcasted_iota(jnp.int32, qt.shape, 1)
    qmat = jnp.where(l_i // lanes_per_col_grp == r_i // t_new, qt, 0.0)
    qb = (qmat * scale).astype(BF16)

    s = lax.dot_general(kb_sc[...], qb, NT_DIMS, preferred_element_type=F32)
    coli = lax.broadcasted_iota(jnp.int32, (1, LANES), 1)
    slope = slopes_ref[...]
    rowp = lax.broadcasted_iota(jnp.int32, (past, LANES), 0)
    s_past = s[:past] + slope * (rowp - past).astype(F32)
    rown = lax.broadcasted_iota(jnp.int32, (NEW_ROWS, LANES), 0)
    s_new = s[past:] + slope * rown.astype(F32)
    s_new = jnp.where(rown <= coli % t_new, s_new, -jnp.inf)

    if moba:
        nbp = past // MOBA_BLOCK
        ppb = MOBA_BLOCK // page
        gate = jnp.zeros((nbp, LANES), F32)
        kmean = jnp.concatenate(
            [sum(ksum[n * ppb:(n + 1) * ppb]) for n in range(nbp)], axis=0) * (1.0 / MOBA_BLOCK)
        for i in range(MB_HEADS * t_new):
            hd, tok = i // t_new, i % t_new
            lo = hd * HEAD_DIM
            g_i = jnp.sum(kmean[:, lo:lo + HEAD_DIM] * q[tok:tok + 1, lo:lo + HEAD_DIM], axis=-1, keepdims=True)
            gate = jnp.where(coli == i, g_i, gate)
        rowb = lax.broadcasted_iota(jnp.int32, gate.shape, 0)
        sel = jnp.zeros(gate.shape, jnp.bool_)
        g = gate
        for _ in range(MOBA_TOPK):
            mx = jnp.max(g, axis=0, keepdims=True)
            idx = jnp.min(jnp.where(g == mx, rowb, nbp), axis=0, keepdims=True)
            pick = (rowb == idx) & (mx > -jnp.inf)
            sel = sel | pick
            g = jnp.where(pick, -jnp.inf, g)
        sel_bias = jnp.where(sel, 0.0, -jnp.inf)
        s_past = (s_past.reshape(nbp, MOBA_BLOCK, LANES) + sel_bias[:, None, :]).reshape(past, LANES)

    m = jnp.maximum(jnp.max(s_past, axis=0, keepdims=True), jnp.max(s_new, axis=0, keepdims=True))
    p_past = jnp.exp(s_past - m)
    p_new = jnp.exp(s_new - m)
    l = jnp.sum(p_past, axis=0, keepdims=True) + jnp.sum(p_new, axis=0, keepdims=True)
    pt = jnp.concatenate([p_past, p_new], axis=0).astype(BF16)
    o_t = lax.dot_general(vb_sc[...], pt, TN_DIMS, preferred_element_type=F32)
    o = o_t.T
    l_col = jnp.broadcast_to(l, (LANES, LANES)).T
    o = o / jnp.concatenate([l_col] * (width // LANES), axis=1)

    outs = []
    for hd in range(DA_HEADS):
        lo = hd * HEAD_DIM
        r0 = hd * cols_per_head
        if moba:
            outs.append(o[r0:r0 + t_new, lo:lo + HEAD_DIM])
        else:
            dl = dl_ref[...]
            lam = (jnp.exp(jnp.sum(dl[0:1] * dl[1:2], axis=-1, keepdims=True))
                   - jnp.exp(jnp.sum(dl[2:3] * dl[3:4], axis=-1, keepdims=True)) + lam_init)
            oh = o[r0:r0 + t_new, lo:lo + HEAD_DIM] - lam * o[r0 + t_new:r0 + 2 * t_new, lo:lo + HEAD_DIM]
            var = jnp.mean(oh * oh, axis=-1, keepdims=True)
            outs.append(oh * lax.rsqrt(var + RMS_EPS) * nw_ref[...] * (1.0 - lam_init))
    o_ref[...] = jnp.concatenate(outs, axis=1).astype(o_ref.dtype)


def _attn_decode(zs, cache_k, cache_v, page_table, slopes_cols, dl, nw, *, layer, moba, lam_init):
    n_seq, t_new, _ = zs.shape
    page = cache_k.shape[2] // DA_HEADS
    n_pages = page_table.shape[1]
    width = DA_HEADS * HEAD_DIM
    qblk, kblk, vblk = (QB_BLK, KB_BLK, VB_BLK) if moba else (QA_BLK, KA_BLK, VA_BLK)
    past = n_pages * page

    def page_spec(p):
        return pl.BlockSpec((None, None, page * DA_HEADS, HEAD_DIM), lambda b, pt: (layer, pt[b, p], 0, 0))

    def z_spec(blk):
        return pl.BlockSpec((None, t_new, width), lambda b, pt: (b, 0, blk // DA_HEADS))

    kern = functools.partial(_attn_decode_kernel, moba=moba, n_pages=n_pages, page=page, t_new=t_new,
                             lam_init=lam_init)
    grid_spec = pltpu.PrefetchScalarGridSpec(
        num_scalar_prefetch=1,
        grid=(n_seq,),
        in_specs=[
            pl.BlockSpec((1, LANES), lambda b, pt: (0, 0)),
            pl.BlockSpec((4, DA_QK), lambda b, pt: (0, 0)),
            pl.BlockSpec((1, HEAD_DIM), lambda b, pt: (0, 0)),
            z_spec(qblk), z_spec(kblk), z_spec(vblk),
        ] + [page_spec(p) for p in range(n_pages)] * 2,
        out_specs=pl.BlockSpec((None, t_new, width), lambda b, pt: (b, 0, 0)),
        scratch_shapes=[
            pltpu.VMEM((past + NEW_ROWS, width), BF16), pltpu.VMEM((past + NEW_ROWS, width), BF16),
            pltpu.VMEM((NEW_ROWS, width), F32),
        ],
    )
    return pl.pallas_call(
        kern,
        grid_spec=grid_spec,
        out_shape=jax.ShapeDtypeStruct((n_seq, t_new, width), BF16),
        compiler_params=_cparams(1),
        name="moba_decode" if moba else "diff_decode",
    )(page_table, slopes_cols, dl, nw, zs, zs, zs, *([cache_k] * n_pages), *([cache_v] * n_pages))


def _ssd_decode_kernel(zc_ref, xs_ref, bc_ref, dtx_ref, cst_ref, h0_ref, cw_ref, cb_ref, dtb_ref, alog_ref,
                       dsk_ref, nw_ref, y_ref, hout_ref, cout_ref, xp_sc, *, t_new):
    hpg = SSM_HEADS // SSM_GROUPS
    gw = D_SSM // SSM_GROUPS
    xp_sc[0:CONV_W - 1, :] = cst_ref[...]
    xp_sc[CONV_W - 1:CONV_W - 1 + t_new, :D_SSM] = xs_ref[...]
    xp_sc[CONV_W - 1:CONV_W - 1 + t_new, D_SSM:] = bc_ref[...]
    cout_ref[...] = xp_sc[t_new:t_new + CONV_W - 1, :]
    y = cb_ref[...]
    for j in range(CONV_W):
        y = y + xp_sc[j:j + t_new, :] * cw_ref[j:j + 1, :]
    xbc = _silu(y)
    xs = xbc[:, :D_SSM]
    dt = _softplus(dtx_ref[...] + dtb_ref[...])
    da = dt * (-jnp.exp(alog_ref[...]))
    acs = [da[0:1]]
    for s in range(1, t_new):
        acs.append(acs[-1] + da[s:s + 1])
    xdt = xs * dt
    last = acs[-1]
    wx = jnp.concatenate([xdt[s:s + 1] * jnp.exp(last - acs[s]) for s in range(t_new)], axis=0)
    dec = jnp.exp(last)
    d_hi = dec.astype(BF16).astype(F32)
    d_md = (dec - d_hi).astype(BF16).astype(F32)
    d_lo = (dec - d_hi - d_md).astype(BF16).astype(F32)
    dec3 = jnp.concatenate([d_hi, d_md, d_lo, jnp.zeros((5, D_SSM), F32)], axis=0).astype(BF16)
    ones = jnp.ones((8, D_STATE), BF16)
    zc = zc_ref[...]
    for g in range(SSM_GROUPS):
        sl = slice(g * gw, (g + 1) * gw)
        bg = xbc[:, D_SSM + g * D_STATE:D_SSM + (g + 1) * D_STATE]
        cg = xbc[:, D_SSM + (SSM_GROUPS + g) * D_STATE:D_SSM + (SSM_GROUPS + g + 1) * D_STATE]
        h0 = h0_ref[g * hpg:(g + 1) * hpg].reshape(gw, D_STATE)
        yoff = lax.dot_general(cg.astype(BF16), h0.astype(BF16), NT_DIMS, preferred_element_type=F32)
        rows = []
        for t in range(t_new):
            yt = jnp.exp(acs[t][:, sl]) * yoff[t:t + 1]
            for s in range(t + 1):
                gts = jnp.sum(cg[t:t + 1] * bg[s:s + 1], axis=-1, keepdims=True)
                yt = yt + gts * jnp.exp(acs[t][:, sl] - acs[s][:, sl]) * xdt[s:s + 1, sl]
            rows.append(yt)
        yg = jnp.concatenate(rows, axis=0) + dsk_ref[:, sl] * xs[:, sl]
        yg = yg * _silu(zc[:, sl])
        var = jnp.mean(yg * yg, axis=-1, keepdims=True)
        y_ref[:, sl] = (yg * lax.rsqrt(var + RMS_EPS) * nw_ref[:, sl]).astype(y_ref.dtype)
        wx8 = jnp.concatenate([wx[:, sl], jnp.zeros((8 - t_new, gw), F32)], axis=0).astype(BF16)
        bg8 = jnp.concatenate([bg, jnp.zeros((8 - t_new, D_STATE), F32)], axis=0).astype(BF16)
        upd = lax.dot_general(wx8, bg8, TN_DIMS, preferred_element_type=F32)
        dec_full = lax.dot_general(dec3[:, sl], ones, TN_DIMS, preferred_element_type=F32)
        hout_ref[g * hpg:(g + 1) * hpg] = (dec_full * h0 + upd).reshape(hpg, SSM_HEADDIM, D_STATE)


def _ssd_decode(zs, dtx, state_conv, state_ssm, cw, cb, dtbx, alogx, dskx, nw, *, layer):
    n_seq, t_new, _ = zs.shape
    kern = functools.partial(_ssd_decode_kernel, t_new=t_new)
    full = lambda shape: pl.BlockSpec(shape, lambda b: (0,) * len(shape))
    return pl.pallas_call(
        kern,
        grid=(n_seq,),
        in_specs=[
            pl.BlockSpec((None, t_new, D_SSM), lambda b: (b, 0, ZC_COL // D_SSM)),
            pl.BlockSpec((None, t_new, D_SSM), lambda b: (b, 0, XS_COL // D_SSM)),
            pl.BlockSpec((None, t_new, 512), lambda b: (b, 0, BC_COL // 512)),
            pl.BlockSpec((None, t_new, D_SSM), lambda b: (b, 0, 0)),
            pl.BlockSpec((None, None, CONV_W - 1, D_XBC), lambda b: (layer, b, 0, 0)),
            pl.BlockSpec((None, None, SSM_HEADS, SSM_HEADDIM, D_STATE), lambda b: (layer, b, 0, 0, 0)),
            full((CONV_W, D_XBC)), full((1, D_XBC)), full((1, D_SSM)), full((1, D_SSM)),
            full((1, D_SSM)), full((1, D_SSM)),
        ],
        out_specs=[
            pl.BlockSpec((None, t_new, D_SSM), lambda b: (b, 0, 0)),
            pl.BlockSpec((None, SSM_HEADS, SSM_HEADDIM, D_STATE), lambda b: (b, 0, 0, 0)),
            pl.BlockSpec((None, CONV_W - 1, D_XBC), lambda b: (b, 0, 0)),
        ],
        out_shape=[
            jax.ShapeDtypeStruct((n_seq, t_new, D_SSM), BF16),
            jax.ShapeDtypeStruct((n_seq, SSM_HEADS, SSM_HEADDIM, D_STATE), F32),
            jax.ShapeDtypeStruct((n_seq, CONV_W - 1, D_XBC), F32),
        ],
        scratch_shapes=[pltpu.VMEM((8, D_XBC), F32)],
        compiler_params=_cparams(1),
        name="ssd_decode",
    )(zs, zs, zs, dtx, state_conv, state_ssm, cw, cb, dtbx, alogx, dskx, nw)


def _layer_norm(x, w, b):
    mu = jnp.mean(x, axis=-1, keepdims=True)
    xc = x - mu
    var = jnp.mean(xc * xc, axis=-1, keepdims=True)
    return xc * lax.rsqrt(var + LN_EPS) * w + b


def _top2_of(vals, col, n):
    m1 = jnp.max(vals, axis=-1, keepdims=True)
    i1 = jnp.min(jnp.where(vals == m1, col, n), axis=-1, keepdims=True)
    rest = jnp.where(col == i1, -jnp.inf, vals)
    m2 = jnp.max(rest, axis=-1, keepdims=True)
    i2 = jnp.min(jnp.where(rest == m2, col, n), axis=-1, keepdims=True)
    return m1, i1, m2, i2


def _router_gates(logits, b_router):
    scores = jax.nn.sigmoid(logits)
    biased = scores + b_router
    col = lax.broadcasted_iota(jnp.int32, logits.shape, 1)
    grp = col // EXPERTS_PER_GROUP
    best = jnp.zeros((logits.shape[0], 1), jnp.int32)
    best_v = None
    for g in range(N_EXPERT_GROUPS):
        m1, _, m2, _ = _top2_of(jnp.where(grp == g, biased, -jnp.inf), col, N_EXPERTS)
        gs = m1 + m2
        if g == 0:
            best_v = gs
        else:
            upd = gs > best_v
            best = jnp.where(upd, g, best)
            best_v = jnp.where(upd, gs, best_v)
    _, i1, _, i2 = _top2_of(jnp.where(grp == best, biased, -jnp.inf), col, N_EXPERTS)
    w = jnp.where((col == i1) | (col == i2), scores, 0.0)
    return w / jnp.sum(w, axis=-1, keepdims=True)


def _post1_kernel(x_ref, oa_ref, ob_ref, oc_ref, wa_ref, wb_ref, wc_ref, lw_ref, lb_ref, wr_ref, br_ref,
                  x1_ref, x1b_ref, g_ref, *, alpha):
    mix = (jnp.dot(oa_ref[...], wa_ref[...], preferred_element_type=F32)
           + jnp.dot(ob_ref[...], wb_ref[...], preferred_element_type=F32)
           + jnp.dot(oc_ref[...], wc_ref[...], preferred_element_type=F32))
    x1 = _layer_norm(alpha * x_ref[...] + mix, lw_ref[...], lb_ref[...])
    x1_ref[...] = x1
    x1b_ref[...] = x1.astype(BF16)
    logits = jnp.dot(x1, wr_ref[...], preferred_element_type=F32, precision=lax.Precision.HIGHEST)
    g_ref[...] = _router_gates(logits, br_ref[...])


def _post1(x, oa, ob, oc, w_out_b, lw, lb, w_router, b_router, *, alpha, tm=256):
    n, d = x.shape
    full = lambda shape: pl.BlockSpec(shape, lambda i: (0,) * len(shape))
    row = lambda width: pl.BlockSpec((tm, width), lambda i: (i, 0))
    wa, wb = DA_HEADS * HEAD_DIM, MB_HEADS * HEAD_DIM
    return pl.pallas_call(
        functools.partial(_post1_kernel, alpha=alpha),
        grid=(n // tm,),
        in_specs=[
            row(d), row(wa), row(wb), row(D_SSM),
            pl.BlockSpec((wa, d), lambda i: (0, 0)), pl.BlockSpec((wb, d), lambda i: (1, 0)),
            pl.BlockSpec((D_SSM, d), lambda i: (1, 0)),
            full((1, d)), full((1, d)), full((d, N_EXPERTS)), full((1, N_EXPERTS)),
        ],
        out_specs=[row(d), row(d), row(N_EXPERTS)],
        out_shape=[jax.ShapeDtypeStruct((n, d), F32), jax.ShapeDtypeStruct((n, d), BF16),
                   jax.ShapeDtypeStruct((n, N_EXPERTS), F32)],
        compiler_params=_cparams(1),
        name="out_proj_ln1_router",
    )(x, oa, ob, oc, w_out_b, w_out_b, w_out_b, lw, lb, w_router, b_router)


def _moe_kernel(x_ref, g_ref, wg_ref, wu_ref, wd_ref, o_ref):
    e = pl.program_id(1)
    x = x_ref[...]
    hid = _silu(jnp.dot(x, wg_ref[...], preferred_element_type=F32)) * jnp.dot(x, wu_ref[...],
                                                                                preferred_element_type=F32)
    col = lax.broadcasted_iota(jnp.int32, g_ref.shape, 1)
    gate = jnp.sum(jnp.where(col == e, g_ref[...], 0.0), axis=-1, keepdims=True)
    part = jnp.dot((hid * gate).astype(BF16), wd_ref[...], preferred_element_type=F32)

    @pl.when(e == 0)
    def _():
        o_ref[...] = part

    @pl.when(e > 0)
    def _():
        o_ref[...] += part


def _moe(x1b, gates, wg, wu, wd, *, tm):
    n, d = x1b.shape
    f = wg.shape[-1]
    return pl.pallas_call(
        _moe_kernel,
        grid=(n // tm, N_EXPERTS),
        in_specs=[
            pl.BlockSpec((tm, d), lambda i, e: (i, 0)),
            pl.BlockSpec((tm, N_EXPERTS), lambda i, e: (i, 0)),
            pl.BlockSpec((None, d, f), lambda i, e: (e, 0, 0)),
            pl.BlockSpec((None, d, f), lambda i, e: (e, 0, 0)),
            pl.BlockSpec((None, f, d), lambda i, e: (e, 0, 0)),
        ],
        out_specs=pl.BlockSpec((tm, d), lambda i, e: (i, 0)),
        out_shape=jax.ShapeDtypeStruct((n, d), F32),
        compiler_params=_cparams(2),
        name="moe_dense",
    )(x1b, gates, wg, wu, wd)


def _post2_kernel(x1_ref, x1b_ref, ff_ref, p_ref, wgate_ref, wproj_ref, lw_ref, lb_ref, x2_ref, x2b_ref, *, alpha):
    gate = jax.nn.sigmoid(jnp.dot(x1b_ref[...], wgate_ref[...], preferred_element_type=F32))
    ple = gate * jnp.dot(p_ref[...], wproj_ref[...], preferred_element_type=F32)
    x2 = _layer_norm(alpha * x1_ref[...] + ff_ref[...] + ple, lw_ref[...], lb_ref[...])
    x2_ref[...] = x2
    x2b_ref[...] = x2.astype(BF16)


def _post2(x1, x1b, ff, p, w_gate_b, w_proj_b, lw, lb, *, alpha, tm=256):
    n, d = x1.shape
    dp = p.shape[1]
    full = lambda shape: pl.BlockSpec(shape, lambda i: (0,) * len(shape))
    row = lambda width: pl.BlockSpec((tm, width), lambda i: (i, 0))
    return pl.pallas_call(
        functools.partial(_post2_kernel, alpha=alpha),
        grid=(n // tm,),
        in_specs=[row(d), row(d), row(d), row(dp), full((d, d)), full((dp, d)), full((1, d)), full((1, d))],
        out_specs=[row(d), row(d)],
        out_shape=[jax.ShapeDtypeStruct((n, d), F32), jax.ShapeDtypeStruct((n, d), BF16)],
        compiler_params=_cparams(1),
        name="ple_ln2",
    )(x1, x1b, ff, p, w_gate_b, w_proj_b, lw, lb)


def kernel(x_prompt, x_sample, p_prompt, p_sample, cache_diff_k, cache_diff_v, cache_moba_k, cache_moba_v, state_ssm, state_conv, page_table, w_in, w_out, diff_lambda, diff_norm_w, ssm_conv_w, ssm_conv_b, ssm_dt_bias, ssm_a_log, ssm_d, ssm_norm_w, ln1_w, ln1_b, ln2_w, ln2_b, w_router, b_router, w_exp_gate, w_exp_up, w_exp_down, w_ple_proj, w_ple_gate):
    depth = w_in.shape[0]
    batch, seq, d_model = x_prompt.shape
    n_seq, t_new, _ = x_sample.shape
    n_p, n_s = batch * seq, n_seq * t_new
    alpha = (2 * depth) ** 0.25
    n_heads = DA_HEADS + MB_HEADS
    slopes = jnp.asarray(2.0 ** (-8.0 * np.arange(1, n_heads + 1) / n_heads), dtype=F32)

    def score_col_slopes(sl, cols_per_head):
        v = jnp.repeat(sl, cols_per_head)
        return jnp.pad(v, (0, LANES - v.shape[0])).reshape(1, LANES)

    slopes_diff_cols = score_col_slopes(slopes[:DA_HEADS], 2 * t_new)
    slopes_moba_cols = score_col_slopes(slopes[DA_HEADS:], t_new)
    pad_lanes = lambda v: jnp.pad(v, (0, LANES - v.shape[0])).reshape(1, LANES)
    per_channel = lambda v: jnp.repeat(v, SSM_HEADDIM).reshape(1, D_SSM)

    n_pool, page = cache_diff_k.shape[1], cache_diff_k.shape[2]
    flat_cache = lambda c: c.reshape(depth, n_pool, page * DA_HEADS, HEAD_DIM)
    ck_a, cv_a, ck_b, cv_b = map(flat_cache, (cache_diff_k, cache_diff_v, cache_moba_k, cache_moba_v))

    x = jnp.concatenate([x_prompt.reshape(n_p, d_model), x_sample.reshape(n_s, d_model)], axis=0)
    xb = x.astype(BF16)
    outs_p = [[] for _ in range(6)]
    outs_s = [[] for _ in range(6)]
    for i in range(depth):
        lam_init = 0.8 - 0.6 * math.exp(-0.3 * i)
        w_in_b = jnp.pad(w_in[i].astype(BF16), ((0, 0), (0, D_IN_PAD - D_IN_PROJ)))
        z = _matmul(xb, w_in_b, tm=512, tn=D_IN_PAD // 3)
        zs = z[n_p:].reshape(n_seq, t_new, D_IN_PAD)
        w_dtx = jnp.repeat(w_in[i][:, DT_COL:DT_COL + SSM_HEADS].astype(BF16), SSM_HEADDIM, axis=1)
        dtx = _matmul(xb[n_p:], w_dtx, tm=n_s, tn=D_SSM).reshape(n_seq, t_new, D_SSM)

        dl = diff_lambda[i]
        nw_a = diff_norm_w[i].reshape(1, HEAD_DIM)
        cw, cb = ssm_conv_w[i], ssm_conv_b[i].reshape(1, D_XBC)
        nw_c = ssm_norm_w[i].reshape(1, D_SSM)

        ka, vt, kmean = _kv_prep(z, batch=batch, seq=seq)
        oa_p = _diff_prompt(z, ka, vt, slopes, dl, nw_a, batch=batch, seq=seq, lam_init=lam_init)
        ob_p = _moba_prompt(z, ka, vt, kmean, slopes, batch=batch, seq=seq)
        oc_p, h_p = _ssd_prompt(z, cw, cb, pad_lanes(ssm_dt_bias[i]), pad_lanes(ssm_a_log[i]),
                                per_channel(ssm_d[i]), nw_c, batch=batch, seq=seq)
        oa_s = _attn_decode(zs, ck_a, cv_a, page_table, slopes_diff_cols, dl, nw_a, layer=i, moba=False,
                            lam_init=lam_init)
        ob_s = _attn_decode(zs, ck_b, cv_b, page_table, slopes_moba_cols, dl, nw_a, layer=i, moba=True,
                            lam_init=lam_init)
        oc_s, h_s, conv_s = _ssd_decode(zs, dtx, state_conv, state_ssm, cw, cb, per_channel(ssm_dt_bias[i]),
                                        per_channel(ssm_a_log[i]), per_channel(ssm_d[i]), nw_c, layer=i)

        head_shape = lambda a, lead: a.reshape(*lead, DA_HEADS, HEAD_DIM)
        zp = z[:n_p]
        for j, blk in enumerate((KA_BLK, VA_BLK, KB_BLK, VB_BLK)):
            c0 = blk * LANES
            outs_p[j].append(head_shape(zp[:, c0:c0 + DA_HEADS * HEAD_DIM], (batch, seq)))
            outs_s[j].append(head_shape(zs[:, :, c0:c0 + DA_HEADS * HEAD_DIM], (n_seq, t_new)))
        outs_p[4].append(h_p)
        outs_p[5].append(zp.reshape(batch, seq, D_IN_PAD)[:, seq - (CONV_W - 1):, XS_COL:XS_COL + D_XBC])
        outs_s[4].append(h_s)
        outs_s[5].append(conv_s)

        oa = jnp.concatenate([oa_p, oa_s.reshape(n_s, -1)], axis=0)
        ob = jnp.concatenate([ob_p, ob_s.reshape(n_s, -1)], axis=0)
        oc = jnp.concatenate([oc_p, oc_s.reshape(n_s, -1)], axis=0)
        x1, x1b, gates = _post1(x, oa, ob, oc, w_out[i].astype(BF16), ln1_w[i].reshape(1, -1),
                                ln1_b[i].reshape(1, -1), w_router, b_router.reshape(1, -1), alpha=alpha)
        ff = _moe(x1b, gates, w_exp_gate[i].astype(BF16), w_exp_up[i].astype(BF16), w_exp_down[i].astype(BF16),
                  tm=(n_p + n_s) // 8)
        p = jnp.concatenate([p_prompt[i].reshape(n_p, -1), p_sample[i].reshape(n_s, -1)], axis=0).astype(BF16)
        x, xb = _post2(x1, x1b, ff, p, w_ple_gate[i].astype(BF16), w_ple_proj[i].astype(BF16),
                       ln2_w[i].reshape(1, -1), ln2_b[i].reshape(1, -1), alpha=alpha)

    y_p = x[:n_p].reshape(batch, seq, d_model)
    y_s = x[n_p:].reshape(n_seq, t_new, d_model)
    return (y_p, y_s, *[jnp.stack(o) for o in outs_p], *[jnp.stack(o) for o in outs_s])
```

```python
import functools
import math

import jax
import jax.numpy as jnp
import numpy as np
from jax import lax
from jax.experimental import pallas as pl
from jax.experimental.pallas import tpu as pltpu

F32 = jnp.float32
BF16 = jnp.bfloat16

LANES = 128
HEAD_DIM = 128
DA_HEADS = 4
DA_QK = 64
MB_HEADS = 4
MOBA_BLOCK = 256
MOBA_TOPK = 3
D_SSM = 1024
SSM_HEADS = 16
SSM_HEADDIM = 64
SSM_GROUPS = 2
D_STATE = 128
CONV_W = 4
D_XBC = D_SSM + 2 * SSM_GROUPS * D_STATE
N_EXPERTS = 16
N_EXPERT_GROUPS = 4
EXPERTS_PER_GROUP = 4
LN_EPS = 1e-5
RMS_EPS = 1e-6
VMEM_LIMIT = 56 * 1024 * 1024

QA_BLK, KA_BLK, VA_BLK, QB_BLK, KB_BLK, VB_BLK = 0, 4, 8, 12, 16, 20
ZC_COL, XS_COL, BC_COL, DT_COL = 3072, 4096, 5120, 5632
D_IN_PROJ = 5648
D_IN_PAD = 5760

NT_DIMS = (((1,), (1,)), ((), ()))
TN_DIMS = (((0,), (0,)), ((), ()))


def _cparams(n_axes):
    return pltpu.CompilerParams(dimension_semantics=("arbitrary",) * n_axes, vmem_limit_bytes=VMEM_LIMIT)


def _mm_kernel(x_ref, w_ref, o_ref):
    o_ref[...] = jnp.dot(x_ref[...], w_ref[...], preferred_element_type=F32).astype(o_ref.dtype)


def _matmul(x, w, tm, tn, out_dtype=F32):
    m, k = x.shape
    n = w.shape[1]
    return pl.pallas_call(
        _mm_kernel,
        grid=(n // tn, m // tm),
        in_specs=[pl.BlockSpec((tm, k), lambda j, i: (i, 0)), pl.BlockSpec((k, tn), lambda j, i: (0, j))],
        out_specs=pl.BlockSpec((tm, tn), lambda j, i: (i, j)),
        out_shape=jax.ShapeDtypeStruct((m, n), out_dtype),
        compiler_params=_cparams(2),
        name="in_proj",
    )(x, w)


POS_SPLIT = 64
ONEHOT_LANE0 = 8
MASK_BIAS = -1e30


ATT_TILE = 256
N_ATT_HEADS = DA_HEADS + MB_HEADS


def _kv_prep_kernel(k_ref, v_ref, ka_ref, vt_ref, km_ref, *, ck):
    pos = pl.program_id(2) * ck + lax.broadcasted_iota(jnp.int32, (ck, LANES), 0)
    lane = lax.broadcasted_iota(jnp.int32, (ck, LANES), 1)
    feat = jnp.where(lane == 0, pos // POS_SPLIT, jnp.where(lane == 1, pos % POS_SPLIT, jnp.where(lane == 2, 1, 0)))
    feat = jnp.where(lane - ONEHOT_LANE0 == pos // MOBA_BLOCK, 1, feat)
    ka_ref[:, :HEAD_DIM] = k_ref[...].astype(BF16)
    ka_ref[:, HEAD_DIM:] = feat.astype(F32).astype(BF16)
    for j in range(ck // ATT_TILE):
        rows = slice(j * ATT_TILE, (j + 1) * ATT_TILE)
        vt_ref[j] = v_ref[rows, :].T.astype(BF16)
        km_ref[j:j + 1, :] = jnp.mean(k_ref[rows, :], axis=0, keepdims=True)


def _kv_prep(z, *, batch, seq, ck=2048):
    nc = seq // ck
    tiles = ck // ATT_TILE
    k_blk = lambda h: KA_BLK + h + (KB_BLK - KA_BLK - DA_HEADS) * (h // DA_HEADS)
    v_blk = lambda h: VA_BLK + h + (VB_BLK - VA_BLK - DA_HEADS) * (h // DA_HEADS)
    return pl.pallas_call(
        functools.partial(_kv_prep_kernel, ck=ck),
        grid=(batch, N_ATT_HEADS, nc),
        in_specs=[
            pl.BlockSpec((ck, HEAD_DIM), lambda b, h, c: (b * nc + c, k_blk(h))),
            pl.BlockSpec((ck, HEAD_DIM), lambda b, h, c: (b * nc + c, v_blk(h))),
        ],
        out_specs=[
            pl.BlockSpec((None, None, ck, 2 * HEAD_DIM), lambda b, h, c: (b, h, c, 0)),
            pl.BlockSpec((None, None, tiles, HEAD_DIM, ATT_TILE), lambda b, h, c: (b, h, c, 0, 0)),
            pl.BlockSpec((None, None, tiles, HEAD_DIM), lambda b, h, c: (b, h, c, 0)),
        ],
        out_shape=[
            jax.ShapeDtypeStruct((batch, N_ATT_HEADS, seq, 2 * HEAD_DIM), BF16),
            jax.ShapeDtypeStruct((batch, N_ATT_HEADS, seq // ATT_TILE, HEAD_DIM, ATT_TILE), BF16),
            jax.ShapeDtypeStruct((batch, N_ATT_HEADS, seq // ATT_TILE, HEAD_DIM), F32),
        ],
        compiler_params=_cparams(3),
        name="kv_prep",
    )(z, z)


def _query_feature_rows(slope, q0, n_cols):
    row = lax.broadcasted_iota(jnp.int32, (8, n_cols), 0)
    c2 = -(slope * q0.astype(F32))
    return jnp.where(row == 0, slope * POS_SPLIT, jnp.where(row == 1, slope, jnp.where(row == 2, c2, 0.0)))


def _softmax_t_step(n_heads, kj, ka_ref, vt_ref, qa_sc, m_sc, l_sc, acc_sc, bias=None):
    t = ATT_TILE
    rows = pl.ds(pl.multiple_of(kj * t, t), t)
    scores = [jnp.dot(ka_ref[h, rows, :], qa_sc[h], preferred_element_type=F32) for h in range(n_heads)]
    probs = []
    for h in range(n_heads):
        s = scores[h] if bias is None else scores[h] + bias
        m = m_sc[h]
        m_new = jnp.maximum(m, jnp.max(s, axis=0, keepdims=True))
        alpha = jnp.exp(m - m_new)
        p = jnp.exp(s - m_new)
        l_sc[h] = alpha * l_sc[h] + jnp.sum(p, axis=0, keepdims=True)
        m_sc[h] = m_new
        probs.append((alpha, p.astype(BF16)))
    for h in range(n_heads):
        alpha, p = probs[h]
        acc_sc[h] = alpha * acc_sc[h] + jnp.dot(vt_ref[h, kj], p, preferred_element_type=F32)


def _softmax_t_run(n_heads, own, ka_ref, vt_ref, mask_ref, qa_sc, m_sc, l_sc, acc_sc):
    m_sc[...] = jnp.full(m_sc.shape, -jnp.inf, F32)
    l_sc[...] = jnp.zeros(l_sc.shape, F32)
    acc_sc[...] = jnp.zeros(acc_sc.shape, F32)
    _softmax_t_step(n_heads, own, ka_ref, vt_ref, qa_sc, m_sc, l_sc, acc_sc, bias=mask_ref[...])

    def body(kj, carry):
        _softmax_t_step(n_heads, kj, ka_ref, vt_ref, qa_sc, m_sc, l_sc, acc_sc)
        return carry

    lax.fori_loop(0, own, body, 0)


def _causal_mask_bias(n_maps):
    t = ATT_TILE
    k_i = lax.broadcasted_iota(jnp.int32, (t, n_maps * t), 0)
    q_i = lax.broadcasted_iota(jnp.int32, (t, n_maps * t), 1) % t
    return jnp.where(k_i <= q_i, 0.0, -jnp.inf).astype(F32)


def _diff_prompt_kernel(slopes_ref, dl_ref, q_ref, ka_ref, vt_ref, mask_ref, nw_ref, o_ref,
                        qa_sc, m_sc, l_sc, acc_sc, *, lam_init):
    t = ATT_TILE
    qi = pl.program_id(1)
    q0 = qi * t
    n_cols = 2 * t
    for h in range(DA_HEADS):
        qt = q_ref[:, h * HEAD_DIM:(h + 1) * HEAD_DIM].T * (DA_QK ** -0.5)
        row = lax.broadcasted_iota(jnp.int32, qt.shape, 0)
        q_top = jnp.concatenate([jnp.where(row < DA_QK, qt, 0.0), jnp.where(row >= DA_QK, qt, 0.0)], axis=1)
        q_feat = jnp.concatenate([_query_feature_rows(slopes_ref[h], q0, n_cols),
                                  jnp.zeros((LANES - 8, n_cols), F32)], axis=0)
        qa_sc[h] = jnp.concatenate([q_top, q_feat], axis=0).astype(BF16)

    _softmax_t_run(DA_HEADS, qi, ka_ref, vt_ref, mask_ref, qa_sc, m_sc, l_sc, acc_sc)

    dl = dl_ref[...]
    lam = (jnp.exp(jnp.sum(dl[0:1] * dl[1:2], axis=-1, keepdims=True))
           - jnp.exp(jnp.sum(dl[2:3] * dl[3:4], axis=-1, keepdims=True)) + lam_init)
    for h in range(DA_HEADS):
        o2 = acc_sc[h] / l_sc[h]
        ot = o2[:, :t] - lam * o2[:, t:]
        var = jnp.mean(ot * ot, axis=0, keepdims=True)
        o = (ot * lax.rsqrt(var + RMS_EPS)).T * nw_ref[...] * (1.0 - lam_init)
        o_ref[:, h * HEAD_DIM:(h + 1) * HEAD_DIM] = o.astype(o_ref.dtype)


def _diff_prompt(z, ka, vt, slopes, dl, nw, *, batch, seq, lam_init):
    t = ATT_TILE
    nq = seq // t
    width = DA_HEADS * HEAD_DIM
    n_cols = 2 * t
    kern = functools.partial(_diff_prompt_kernel, lam_init=lam_init)
    return pl.pallas_call(
        kern,
        grid=(batch, nq),
        in_specs=[
            pl.BlockSpec(memory_space=pltpu.SMEM),
            pl.BlockSpec((4, DA_QK), lambda b, i: (0, 0)),
            pl.BlockSpec((t, width), lambda b, i: (b * nq + i, QA_BLK // DA_HEADS)),
            pl.BlockSpec((None, DA_HEADS, seq, 2 * HEAD_DIM), lambda b, i: (b, 0, 0, 0)),
            pl.BlockSpec((None, DA_HEADS, nq, HEAD_DIM, t), lambda b, i: (b, 0, 0, 0, 0)),
            pl.BlockSpec((t, n_cols), lambda b, i: (0, 0)),
            pl.BlockSpec((1, HEAD_DIM), lambda b, i: (0, 0)),
        ],
        out_specs=pl.BlockSpec((t, width), lambda b, i: (b * nq + i, 0)),
        out_shape=jax.ShapeDtypeStruct((batch * seq, width), BF16),
        scratch_shapes=[
            pltpu.VMEM((DA_HEADS, 2 * HEAD_DIM, n_cols), BF16),
            pltpu.VMEM((DA_HEADS, 1, n_cols), F32), pltpu.VMEM((DA_HEADS, 1, n_cols), F32),
            pltpu.VMEM((DA_HEADS, HEAD_DIM, n_cols), F32),
        ],
        compiler_params=_cparams(2),
        name="diff_prompt",
    )(slopes, dl, z, ka, vt, _causal_mask_bias(2), nw)


def _top3_mask_rows(gate_t, n_valid, own):
    nb = gate_t.shape[0]
    row = lax.broadcasted_iota(jnp.int32, gate_t.shape, 0)
    g = jnp.where(row < n_valid, gate_t, -jnp.inf)
    sel = row == own
    for _ in range(MOBA_TOPK):
        mx = jnp.max(g, axis=0, keepdims=True)
        idx = jnp.min(jnp.where(g == mx, row, nb), axis=0, keepdims=True)
        pick = (row == idx) & (mx > -jnp.inf)
        sel = sel | pick
        g = jnp.where(pick, -jnp.inf, g)
    return jnp.where(sel, 0.0, MASK_BIAS)


def _moba_prompt_kernel(slopes_ref, q_ref, ka_ref, vt_ref, km_ref, mask_ref, o_ref,
                        qa_sc, m_sc, l_sc, acc_sc, *, nb):
    blk = MOBA_BLOCK
    own = pl.program_id(1)
    q0 = own * blk
    for h in range(MB_HEADS):
        qt = q_ref[:, h * HEAD_DIM:(h + 1) * HEAD_DIM].T
        gate_t = jnp.dot(km_ref[h], qt, preferred_element_type=F32, precision=lax.Precision.HIGHEST)
        q_feat = jnp.concatenate([_query_feature_rows(slopes_ref[DA_HEADS + h], q0, blk),
                                  _top3_mask_rows(gate_t, own, own),
                                  jnp.zeros((LANES - ONEHOT_LANE0 - nb, blk), F32)], axis=0)
        qa_sc[h] = jnp.concatenate([qt * (HEAD_DIM ** -0.5), q_feat], axis=0).astype(BF16)

    _softmax_t_run(MB_HEADS, own, ka_ref, vt_ref, mask_ref, qa_sc, m_sc, l_sc, acc_sc)

    for h in range(MB_HEADS):
        o_ref[:, h * HEAD_DIM:(h + 1) * HEAD_DIM] = (acc_sc[h] / l_sc[h]).T.astype(o_ref.dtype)


def _moba_prompt(z, ka, vt, kmean, slopes, *, batch, seq):
    blk = MOBA_BLOCK
    assert blk == ATT_TILE
    nb = seq // blk
    assert nb <= LANES - ONEHOT_LANE0 and nb % 8 == 0
    width = MB_HEADS * HEAD_DIM
    kern = functools.partial(_moba_prompt_kernel, nb=nb)
    return pl.pallas_call(
        kern,
        grid=(batch, nb),
        in_specs=[
            pl.BlockSpec(memory_space=pltpu.SMEM),
            pl.BlockSpec((blk, width), lambda b, i: (b * nb + i, QB_BLK // MB_HEADS)),
            pl.BlockSpec((None, MB_HEADS, seq, 2 * HEAD_DIM), lambda b, i: (b, 1, 0, 0)),
            pl.BlockSpec((None, MB_HEADS, nb, HEAD_DIM, blk), lambda b, i: (b, 1, 0, 0, 0)),
            pl.BlockSpec((None, MB_HEADS, nb, HEAD_DIM), lambda b, i: (b, 1, 0, 0)),
            pl.BlockSpec((blk, blk), lambda b, i: (0, 0)),
        ],
        out_specs=pl.BlockSpec((blk, width), lambda b, i: (b * nb + i, 0)),
        out_shape=jax.ShapeDtypeStruct((batch * seq, width), BF16),
        scratch_shapes=[
            pltpu.VMEM((MB_HEADS, 2 * HEAD_DIM, blk), BF16),
            pltpu.VMEM((MB_HEADS, 1, blk), F32), pltpu.VMEM((MB_HEADS, 1, blk), F32),
            pltpu.VMEM((MB_HEADS, HEAD_DIM, blk), F32),
        ],
        compiler_params=_cparams(2),
        name="moba_prompt",
    )(slopes, z, ka, vt, kmean, _causal_mask_bias(1))


def _softplus(x):
    return jnp.maximum(x, 0.0) + jnp.log1p(jnp.exp(-jnp.abs(x)))


def _silu(x):
    return x * jax.nn.sigmoid(x)


def _conv_silu(xp_sc, w, b, t):
    y = b
    for j in range(CONV_W):
        y = y + xp_sc[5 + j:5 + j + t, :] * w[j:j + 1, :]
    return _silu(y)


def _ssd_prompt_kernel(zc_ref, x_ref, bc_ref, dt_ref, cw_ref, cb_ref, dtb_ref, alog_ref, dsk_ref, nw_ref,
                       y_ref, hfin_ref, xpx_sc, xpbc_sc, h_sc, y_sc, *, t):
    c = pl.program_id(1)
    nc = pl.num_programs(1)
    hp = LANES // SSM_HEADDIM
    hpg = SSM_HEADS // SSM_GROUPS

    @pl.when(c == 0)
    def _():
        xpx_sc[0:8, :] = jnp.zeros((8, D_SSM), F32)
        xpbc_sc[0:8, :] = jnp.zeros((8, D_XBC - D_SSM), F32)
        h_sc[...] = jnp.zeros(h_sc.shape, F32)

    xpx_sc[8:8 + t, :] = x_ref[...]
    xpbc_sc[8:8 + t, :] = bc_ref[...]
    cw = cw_ref[...]
    cb = cb_ref[...]
    xs = _conv_silu(xpx_sc, cw[:, :D_SSM], cb[:, :D_SSM], t)
    bc = _conv_silu(xpbc_sc, cw[:, D_SSM:], cb[:, D_SSM:], t)
    xpx_sc[5:8, :] = xpx_sc[t + 5:t + 8, :]
    xpbc_sc[5:8, :] = xpbc_sc[t + 5:t + 8, :]

    dt = _softplus(dt_ref[...] + dtb_ref[...])
    da = dt * (-jnp.exp(alog_ref[...]))
    r_i = lax.broadcasted_iota(jnp.int32, (t, t), 0)
    c_i = lax.broadcasted_iota(jnp.int32, (t, t), 1)
    tril = c_i <= r_i
    acs = jnp.dot(tril.astype(F32), da, preferred_element_type=F32, precision=lax.Precision.HIGHEST)
    acs_t = acs.T
    lane = lax.broadcasted_iota(jnp.int32, (t, LANES), 1)
    sub = lax.broadcasted_iota(jnp.int32, (LANES, LANES), 0)
    lo_lane = lane < SSM_HEADDIM
    lo_sub = sub < SSM_HEADDIM

    for g in range(SSM_GROUPS):
        bg = bc[:, g * D_STATE:(g + 1) * D_STATE]
        cg = bc[:, (SSM_GROUPS + g) * D_STATE:(SSM_GROUPS + g + 1) * D_STATE]
        bgb = bg.astype(BF16)
        cgb = cg.astype(BF16)
        gmat = lax.dot_general(cgb, bgb, NT_DIMS, preferred_element_type=F32)
        for pr in range(hpg // hp):
            ha = g * hpg + pr * hp
            cb0 = ha * SSM_HEADDIM
            x2 = xs[:, cb0:cb0 + LANES]
            dt2 = jnp.where(lo_lane, dt[:, ha:ha + 1], dt[:, ha + 1:ha + 2])
            xdt = x2 * dt2
            ydiag = jnp.zeros((t, LANES), F32)
            for u in range(hp):
                hh = ha + u
                seg = acs[:, hh:hh + 1] - acs_t[hh:hh + 1, :]
                lmat = jnp.exp(jnp.where(tril, seg, -jnp.inf))
                xu = jnp.where(lo_lane if u == 0 else ~lo_lane, xdt, 0.0)
                ydiag = ydiag + jnp.dot((gmat * lmat).astype(BF16), xu.astype(BF16), preferred_element_type=F32)
            h2 = h_sc[ha:ha + hp].reshape(LANES, D_STATE)
            yoff = lax.dot_general(cgb, h2.astype(BF16), NT_DIMS, preferred_element_type=F32)
            e2 = jnp.where(lo_lane, jnp.exp(acs[:, ha:ha + 1]), jnp.exp(acs[:, ha + 1:ha + 2]))
            y = ydiag + e2 * yoff + dsk_ref[:, cb0:cb0 + LANES] * x2
            y_sc[:, cb0:cb0 + LANES] = y * _silu(zc_ref[:, cb0:cb0 + LANES])
            last = acs[t - 1:t, :]
            te2 = jnp.where(lo_lane, jnp.exp(last[:, ha:ha + 1] - acs[:, ha:ha + 1]),
                            jnp.exp(last[:, ha + 1:ha + 2] - acs[:, ha + 1:ha + 2]))
            upd = lax.dot_general((xdt * te2).astype(BF16), bgb, TN_DIMS, preferred_element_type=F32)
            dec = jnp.where(lo_sub, jnp.exp(last[:, ha:ha + 1]), jnp.exp(last[:, ha + 1:ha + 2]))
            h_sc[ha:ha + hp] = (dec * h2 + upd).reshape(hp, SSM_HEADDIM, D_STATE)

    gw = D_SSM // SSM_GROUPS
    for g in range(SSM_GROUPS):
        yg = y_sc[:, g * gw:(g + 1) * gw]
        var = jnp.mean(yg * yg, axis=-1, keepdims=True)
        y_ref[:, g * gw:(g + 1) * gw] = (yg * lax.rsqrt(var + RMS_EPS) * nw_ref[:, g * gw:(g + 1) * gw]).astype(y_ref.dtype)

    @pl.when(c == nc - 1)
    def _():
        hfin_ref[0] = h_sc[...]


def _ssd_prompt(z, cw, cb, dtb, alog, dsk, nw, *, batch, seq, t=128):
    nc = seq // t
    kern = functools.partial(_ssd_prompt_kernel, t=t)
    full = lambda shape: pl.BlockSpec(shape, lambda b, c: (0,) * len(shape))
    return pl.pallas_call(
        kern,
        grid=(batch, nc),
        in_specs=[
            pl.BlockSpec((t, D_SSM), lambda b, c: (b * nc + c, ZC_COL // D_SSM)),
            pl.BlockSpec((t, D_SSM), lambda b, c: (b * nc + c, XS_COL // D_SSM)),
            pl.BlockSpec((t, 512), lambda b, c: (b * nc + c, BC_COL // 512)),
            pl.BlockSpec((t, LANES), lambda b, c: (b * nc + c, DT_COL // LANES)),
            full((CONV_W, D_XBC)), full((1, D_XBC)), full((1, LANES)), full((1, LANES)),
            full((1, D_SSM)), full((1, D_SSM)),
        ],
        out_specs=[
            pl.BlockSpec((t, D_SSM), lambda b, c: (b * nc + c, 0)),
            pl.BlockSpec((1, SSM_HEADS, SSM_HEADDIM, D_STATE), lambda b, c: (b, 0, 0, 0)),
        ],
        out_shape=[
            jax.ShapeDtypeStruct((batch * seq, D_SSM), BF16),
            jax.ShapeDtypeStruct((batch, SSM_HEADS, SSM_HEADDIM, D_STATE), F32),
        ],
        scratch_shapes=[
            pltpu.VMEM((t + 8, D_SSM), F32), pltpu.VMEM((t + 8, D_XBC - D_SSM), F32),
            pltpu.VMEM((SSM_HEADS, SSM_HEADDIM, D_STATE), F32), pltpu.VMEM((t, D_SSM), F32),
        ],
        compiler_params=_cparams(2),
        name="ssd_prompt",
    )(z, z, z, z, cw, cb, dtb, alog, dsk, nw)


NEW_ROWS = 16


SEQS_PER_STEP = 2


def _attn_decode_kernel(*refs, moba, n_pages, page, t_new, lam_init, spb):
    pt_ref = refs[0]
    del pt_ref
    slopes_ref, dl_ref, nw_ref, q_ref, kn_ref, vn_ref = refs[1:7]
    n_pg = spb * n_pages
    kp_refs = refs[7:7 + n_pg]
    vp_refs = refs[7 + n_pg:7 + 2 * n_pg]
    o_ref = refs[7 + 2 * n_pg]
    kb_sc, vb_sc, new_sc = refs[8 + 2 * n_pg:]
    past = n_pages * page
    width = DA_HEADS * HEAD_DIM
    cols_per_head = t_new if moba else 2 * t_new
    lanes_per_col_grp = HEAD_DIM if moba else DA_QK
    scale = HEAD_DIM ** -0.5 if moba else DA_QK ** -0.5
    seqs = range(spb)
    coli = lax.broadcasted_iota(jnp.int32, (1, LANES), 1)
    slope = slopes_ref[...]

    ksums = []
    for s in seqs:
        ksum = []
        for p in range(n_pages):
            ksum_p = []
            kp, vp = kp_refs[s * n_pages + p], vp_refs[s * n_pages + p]
            for hd in range(DA_HEADS):
                lo = hd * HEAD_DIM
                kh = kp[pl.ds(hd, page, stride=DA_HEADS), :]
                kb_sc[s, p * page:(p + 1) * page, lo:lo + HEAD_DIM] = kh.astype(BF16)
                vb_sc[s, p * page:(p + 1) * page, lo:lo + HEAD_DIM] = vp[pl.ds(hd, page, stride=DA_HEADS), :].astype(BF16)
                if moba:
                    ksum_p.append(jnp.sum(kh, axis=0, keepdims=True))
            if moba:
                ksum.append(jnp.concatenate(ksum_p, axis=1))
        ksums.append(ksum)
        for j, (src, dst) in enumerate(((kn_ref, kb_sc), (vn_ref, vb_sc))):
            new_sc[s, j] = jnp.zeros(new_sc.shape[2:], F32)
            new_sc[s, j, 0:t_new, :] = src[s]
            dst[s, past:past + NEW_ROWS, :] = new_sc[s, j].astype(BF16)

    qs, scores = [], []
    for s in seqs:
        q = q_ref[s]
        qt = jnp.concatenate([q] * (LANES // t_new), axis=0)
        r_i = lax.broadcasted_iota(jnp.int32, qt.shape, 0)
        l_i = lax.broadcasted_iota(jnp.int32, qt.shape, 1)
        qmat = jnp.where(l_i // lanes_per_col_grp == r_i // t_new, qt, 0.0)
        qs.append(q)
        scores.append(lax.dot_general(kb_sc[s], (qmat * scale).astype(BF16), NT_DIMS,
                                      preferred_element_type=F32))

    probs = []
    for s in seqs:
        rowp = lax.broadcasted_iota(jnp.int32, (past, LANES), 0)
        s_past = scores[s][:past] + slope * (rowp - past).astype(F32)
        rown = lax.broadcasted_iota(jnp.int32, (NEW_ROWS, LANES), 0)
        s_new = scores[s][past:] + slope * rown.astype(F32)
        s_new = jnp.where(rown <= coli % t_new, s_new, -jnp.inf)

        if moba:
            nbp = past // MOBA_BLOCK
            ppb = MOBA_BLOCK // page
            gate = jnp.zeros((nbp, LANES), F32)
            kmean = jnp.concatenate(
                [sum(ksums[s][n * ppb:(n + 1) * ppb]) for n in range(nbp)], axis=0) * (1.0 / MOBA_BLOCK)
            for i in range(MB_HEADS * t_new):
                hd, tok = i // t_new, i % t_new
                lo = hd * HEAD_DIM
                g_i = jnp.sum(kmean[:, lo:lo + HEAD_DIM] * qs[s][tok:tok + 1, lo:lo + HEAD_DIM], axis=-1,
                              keepdims=True)
                gate = jnp.where(coli == i, g_i, gate)
            rowb = lax.broadcasted_iota(jnp.int32, gate.shape, 0)
            sel = jnp.zeros(gate.shape, jnp.bool_)
            g = gate
            for _ in range(MOBA_TOPK):
                mx = jnp.max(g, axis=0, keepdims=True)
                idx = jnp.min(jnp.where(g == mx, rowb, nbp), axis=0, keepdims=True)
                pick = (rowb == idx) & (mx > -jnp.inf)
                sel = sel | pick
                g = jnp.where(pick, -jnp.inf, g)
            sel_bias = jnp.where(sel, 0.0, -jnp.inf)
            s_past = (s_past.reshape(nbp, MOBA_BLOCK, LANES) + sel_bias[:, None, :]).reshape(past, LANES)

        m = jnp.maximum(jnp.max(s_past, axis=0, keepdims=True), jnp.max(s_new, axis=0, keepdims=True))
        p_past = jnp.exp(s_past - m)
        p_new = jnp.exp(s_new - m)
        l = jnp.sum(p_past, axis=0, keepdims=True) + jnp.sum(p_new, axis=0, keepdims=True)
        probs.append((jnp.concatenate([p_past, p_new], axis=0).astype(BF16), l))

    outs_t = [lax.dot_general(vb_sc[s], probs[s][0], TN_DIMS, preferred_element_type=F32) for s in seqs]

    for s in seqs:
        o = outs_t[s].T
        l_col = jnp.broadcast_to(probs[s][1], (LANES, LANES)).T
        o = o / jnp.concatenate([l_col] * (width // LANES), axis=1)
        outs = []
        for hd in range(DA_HEADS):
            lo = hd * HEAD_DIM
            r0 = hd * cols_per_head
            if moba:
                outs.append(o[r0:r0 + t_new, lo:lo + HEAD_DIM])
            else:
                dl = dl_ref[...]
                lam = (jnp.exp(jnp.sum(dl[0:1] * dl[1:2], axis=-1, keepdims=True))
                       - jnp.exp(jnp.sum(dl[2:3] * dl[3:4], axis=-1, keepdims=True)) + lam_init)
                oh = o[r0:r0 + t_new, lo:lo + HEAD_DIM] - lam * o[r0 + t_new:r0 + 2 * t_new, lo:lo + HEAD_DIM]
                var = jnp.mean(oh * oh, axis=-1, keepdims=True)
                outs.append(oh * lax.rsqrt(var + RMS_EPS) * nw_ref[...] * (1.0 - lam_init))
        o_ref[s] = jnp.concatenate(outs, axis=1).astype(o_ref.dtype)


def _attn_decode(zs, cache_k, cache_v, page_table, slopes_cols, dl, nw, *, layer, moba, lam_init, spb=SEQS_PER_STEP):
    n_seq, t_new, _ = zs.shape
    page = cache_k.shape[2] // DA_HEADS
    n_pages = page_table.shape[1]
    width = DA_HEADS * HEAD_DIM
    qblk, kblk, vblk = (QB_BLK, KB_BLK, VB_BLK) if moba else (QA_BLK, KA_BLK, VA_BLK)
    past = n_pages * page
    spb = spb if n_seq % spb == 0 else 1

    def page_spec(s, p):
        return pl.BlockSpec((None, None, page * DA_HEADS, HEAD_DIM), lambda b, pt: (layer, pt[b * spb + s, p], 0, 0))

    def z_spec(blk):
        return pl.BlockSpec((spb, t_new, width), lambda b, pt: (b, 0, blk // DA_HEADS))

    kern = functools.partial(_attn_decode_kernel, moba=moba, n_pages=n_pages, page=page, t_new=t_new,
                             lam_init=lam_init, spb=spb)
    page_specs = [page_spec(s, p) for s in range(spb) for p in range(n_pages)]
    grid_spec = pltpu.PrefetchScalarGridSpec(
        num_scalar_prefetch=1,
        grid=(n_seq // spb,),
        in_specs=[
            pl.BlockSpec((1, LANES), lambda b, pt: (0, 0)),
            pl.BlockSpec((4, DA_QK), lambda b, pt: (0, 0)),
            pl.BlockSpec((1, HEAD_DIM), lambda b, pt: (0, 0)),
            z_spec(qblk), z_spec(kblk), z_spec(vblk),
        ] + page_specs * 2,
        out_specs=pl.BlockSpec((spb, t_new, width), lambda b, pt: (b, 0, 0)),
        scratch_shapes=[
            pltpu.VMEM((spb, past + NEW_ROWS, width), BF16), pltpu.VMEM((spb, past + NEW_ROWS, width), BF16),
            pltpu.VMEM((spb, 2, NEW_ROWS, width), F32),
        ],
    )
    return pl.pallas_call(
        kern,
        grid_spec=grid_spec,
        out_shape=jax.ShapeDtypeStruct((n_seq, t_new, width), BF16),
        compiler_params=_cparams(1),
        name="moba_decode" if moba else "diff_decode",
    )(page_table, slopes_cols, dl, nw, zs, zs, zs, *([cache_k] * len(page_specs)), *([cache_v] * len(page_specs)))


def _ssd_decode_kernel(zc_ref, xs_ref, bc_ref, dtx_ref, cst_ref, h0_ref, cw_ref, cb_ref, dtb_ref, alog_ref,
                       dsk_ref, nw_ref, y_ref, hout_ref, cout_ref, xp_sc, *, t_new):
    hpg = SSM_HEADS // SSM_GROUPS
    gw = D_SSM // SSM_GROUPS
    xp_sc[0:CONV_W - 1, :] = cst_ref[...]
    xp_sc[CONV_W - 1:CONV_W - 1 + t_new, :D_SSM] = xs_ref[...]
    xp_sc[CONV_W - 1:CONV_W - 1 + t_new, D_SSM:] = bc_ref[...]
    cout_ref[...] = xp_sc[t_new:t_new + CONV_W - 1, :]
    y = cb_ref[...]
    for j in range(CONV_W):
        y = y + xp_sc[j:j + t_new, :] * cw_ref[j:j + 1, :]
    xbc = _silu(y)
    xs = xbc[:, :D_SSM]
    dt = _softplus(dtx_ref[...] + dtb_ref[...])
    da = dt * (-jnp.exp(alog_ref[...]))
    acs = [da[0:1]]
    for s in range(1, t_new):
        acs.append(acs[-1] + da[s:s + 1])
    xdt = xs * dt
    last = acs[-1]
    wx = jnp.concatenate([xdt[s:s + 1] * jnp.exp(last - acs[s]) for s in range(t_new)], axis=0)
    dec = jnp.exp(last)
    d_hi = dec.astype(BF16).astype(F32)
    d_md = (dec - d_hi).astype(BF16).astype(F32)
    d_lo = (dec - d_hi - d_md).astype(BF16).astype(F32)
    dec3 = jnp.concatenate([d_hi, d_md, d_lo, jnp.zeros((5, D_SSM), F32)], axis=0).astype(BF16)
    ones = jnp.ones((8, D_STATE), BF16)
    zc = zc_ref[...]
    for g in range(SSM_GROUPS):
        sl = slice(g * gw, (g + 1) * gw)
        bg = xbc[:, D_SSM + g * D_STATE:D_SSM + (g + 1) * D_STATE]
        cg = xbc[:, D_SSM + (SSM_GROUPS + g) * D_STATE:D_SSM + (SSM_GROUPS + g + 1) * D_STATE]
        h0 = h0_ref[g * hpg:(g + 1) * hpg].reshape(gw, D_STATE)
        yoff = lax.dot_general(cg.astype(BF16), h0.astype(BF16), NT_DIMS, preferred_element_type=F32)
        rows = []
        for t in range(t_new):
            yt = jnp.exp(acs[t][:, sl]) * yoff[t:t + 1]
            for s in range(t + 1):
                gts = jnp.sum(cg[t:t + 1] * bg[s:s + 1], axis=-1, keepdims=True)
                yt = yt + gts * jnp.exp(acs[t][:, sl] - acs[s][:, sl]) * xdt[s:s + 1, sl]
            rows.append(yt)
        yg = jnp.concatenate(rows, axis=0) + dsk_ref[:, sl] * xs[:, sl]
        yg = yg * _silu(zc[:, sl])
        var = jnp.mean(yg * yg, axis=-1, keepdims=True)
        y_ref[:, sl] = (yg * lax.rsqrt(var + RMS_EPS) * nw_ref[:, sl]).astype(y_ref.dtype)
        wxg = wx[:, sl]
        wx_hi = wxg.astype(BF16).astype(F32)
        bg_hi = bg.astype(BF16).astype(F32)
        pad_rows = 16 - 3 * t_new
        lhs = jnp.concatenate([wx_hi, wxg - wx_hi, wx_hi, jnp.zeros((pad_rows, gw), F32)], axis=0).astype(BF16)
        rhs = jnp.concatenate([bg_hi, bg_hi, bg - bg_hi, jnp.zeros((pad_rows, D_STATE), F32)], axis=0).astype(BF16)
        upd = lax.dot_general(lhs, rhs, TN_DIMS, preferred_element_type=F32)
        dec_full = lax.dot_general(dec3[:, sl], ones, TN_DIMS, preferred_element_type=F32)
        hout_ref[g * hpg:(g + 1) * hpg] = (dec_full * h0 + upd).reshape(hpg, SSM_HEADDIM, D_STATE)


def _ssd_decode(zs, dtx, state_conv, state_ssm, cw, cb, dtbx, alogx, dskx, nw, *, layer):
    n_seq, t_new, _ = zs.shape
    kern = functools.partial(_ssd_decode_kernel, t_new=t_new)
    full = lambda shape: pl.BlockSpec(shape, lambda b: (0,) * len(shape))
    return pl.pallas_call(
        kern,
        grid=(n_seq,),
        in_specs=[
            pl.BlockSpec((None, t_new, D_SSM), lambda b: (b, 0, ZC_COL // D_SSM)),
            pl.BlockSpec((None, t_new, D_SSM), lambda b: (b, 0, XS_COL // D_SSM)),
            pl.BlockSpec((None, t_new, 512), lambda b: (b, 0, BC_COL // 512)),
            pl.BlockSpec((None, t_new, D_SSM), lambda b: (b, 0, 0)),
            pl.BlockSpec((None, None, CONV_W - 1, D_XBC), lambda b: (layer, b, 0, 0)),
            pl.BlockSpec((None, None, SSM_HEADS, SSM_HEADDIM, D_STATE), lambda b: (layer, b, 0, 0, 0)),
            full((CONV_W, D_XBC)), full((1, D_XBC)), full((1, D_SSM)), full((1, D_SSM)),
            full((1, D_SSM)), full((1, D_SSM)),
        ],
        out_specs=[
            pl.BlockSpec((None, t_new, D_SSM), lambda b: (b, 0, 0)),
            pl.BlockSpec((None, SSM_HEADS, SSM_HEADDIM, D_STATE), lambda b: (b, 0, 0, 0)),
            pl.BlockSpec((None, CONV_W - 1, D_XBC), lambda b: (b, 0, 0)),
        ],
        out_shape=[
            jax.ShapeDtypeStruct((n_seq, t_new, D_SSM), BF16),
            jax.ShapeDtypeStruct((n_seq, SSM_HEADS, SSM_HEADDIM, D_STATE), F32),
            jax.ShapeDtypeStruct((n_seq, CONV_W - 1, D_XBC), F32),
        ],
        scratch_shapes=[pltpu.VMEM((8, D_XBC), F32)],
        compiler_params=_cparams(1),
        name="ssd_decode",
    )(zs, zs, zs, dtx, state_conv, state_ssm, cw, cb, dtbx, alogx, dskx, nw)


def _layer_norm(x, w, b):
    mu = jnp.mean(x, axis=-1, keepdims=True)
    xc = x - mu
    var = jnp.mean(xc * xc, axis=-1, keepdims=True)
    return xc * lax.rsqrt(var + LN_EPS) * w + b


def _top2_of(vals, col, n):
    m1 = jnp.max(vals, axis=-1, keepdims=True)
    i1 = jnp.min(jnp.where(vals == m1, col, n), axis=-1, keepdims=True)
    rest = jnp.where(col == i1, -jnp.inf, vals)
    m2 = jnp.max(rest, axis=-1, keepdims=True)
    i2 = jnp.min(jnp.where(rest == m2, col, n), axis=-1, keepdims=True)
    return m1, i1, m2, i2


def _router_gates(logits, b_router):
    scores = jax.nn.sigmoid(logits)
    biased = scores + b_router
    col = lax.broadcasted_iota(jnp.int32, logits.shape, 1)
    grp = col // EXPERTS_PER_GROUP
    best = jnp.zeros((logits.shape[0], 1), jnp.int32)
    best_v = None
    for g in range(N_EXPERT_GROUPS):
        m1, _, m2, _ = _top2_of(jnp.where(grp == g, biased, -jnp.inf), col, N_EXPERTS)
        gs = m1 + m2
        if g == 0:
            best_v = gs
        else:
            upd = gs > best_v
            best = jnp.where(upd, g, best)
            best_v = jnp.where(upd, gs, best_v)
    _, i1, _, i2 = _top2_of(jnp.where(grp == best, biased, -jnp.inf), col, N_EXPERTS)
    w = jnp.where((col == i1) | (col == i2), scores, 0.0)
    return w / jnp.sum(w, axis=-1, keepdims=True)


ROW_CHUNKS = 16


def _store_token_rows(ref, val):
    rows = val.shape[0]
    for c in range(ROW_CHUNKS):
        ref[pl.ds(c, rows, stride=ROW_CHUNKS), :] = val[:, c * LANES:(c + 1) * LANES]


def _load_token_rows(ref, rows):
    return jnp.concatenate([ref[pl.ds(c, rows, stride=ROW_CHUNKS), :] for c in range(ROW_CHUNKS)], axis=1)


def _post1_kernel(x_ref, oa_ref, ob_ref, oc_ref, wa_ref, wb_ref, wc_ref, lw_ref, lb_ref, wr_ref, br_ref,
                  x1r_ref, x1b_ref, g_ref, *, alpha):
    mix = (jnp.dot(oa_ref[...], wa_ref[...], preferred_element_type=F32)
           + jnp.dot(ob_ref[...], wb_ref[...], preferred_element_type=F32)
           + jnp.dot(oc_ref[...], wc_ref[...], preferred_element_type=F32))
    x1 = _layer_norm(alpha * x_ref[...] + mix, lw_ref[...], lb_ref[...])
    _store_token_rows(x1r_ref, x1)
    x1b_ref[...] = x1.astype(BF16)
    logits = jnp.dot(x1, wr_ref[...], preferred_element_type=F32, precision=lax.Precision.HIGHEST)
    g_ref[...] = _router_gates(logits, br_ref[...])


def _post1(x, oa, ob, oc, w_out_b, lw, lb, w_router, b_router, *, alpha, tm=256):
    n, d = x.shape
    assert d == ROW_CHUNKS * LANES
    full = lambda shape: pl.BlockSpec(shape, lambda i: (0,) * len(shape))
    row = lambda width: pl.BlockSpec((tm, width), lambda i: (i, 0))
    wa, wb = DA_HEADS * HEAD_DIM, MB_HEADS * HEAD_DIM
    return pl.pallas_call(
        functools.partial(_post1_kernel, alpha=alpha),
        grid=(n // tm,),
        in_specs=[
            row(d), row(wa), row(wb), row(D_SSM),
            pl.BlockSpec((wa, d), lambda i: (0, 0)), pl.BlockSpec((wb, d), lambda i: (1, 0)),
            pl.BlockSpec((D_SSM, d), lambda i: (1, 0)),
            full((1, d)), full((1, d)), full((d, N_EXPERTS)), full((1, N_EXPERTS)),
        ],
        out_specs=[pl.BlockSpec((tm * ROW_CHUNKS, LANES), lambda i: (i, 0)), row(d), row(N_EXPERTS)],
        out_shape=[jax.ShapeDtypeStruct((n * ROW_CHUNKS, LANES), F32), jax.ShapeDtypeStruct((n, d), BF16),
                   jax.ShapeDtypeStruct((n, N_EXPERTS), F32)],
        compiler_params=_cparams(1),
        name="out_proj_ln1_router",
    )(x, oa, ob, oc, w_out_b, w_out_b, w_out_b, lw, lb, w_router, b_router)


MOE_TILE = 256


def _moe_routing(gates, tile):
    n = gates.shape[0]
    n_tiles = n // tile + N_EXPERT_GROUPS
    n_slots = n_tiles * tile
    gid = jnp.argmax(gates > 0, axis=1).astype(jnp.int32) // EXPERTS_PER_GROUP
    onehot = (gid[:, None] == jnp.arange(N_EXPERT_GROUPS, dtype=jnp.int32)[None, :]).astype(jnp.int32)
    rank = jnp.sum((jnp.cumsum(onehot, axis=0) - onehot) * onehot, axis=1)
    counts = jnp.sum(onehot, axis=0)
    padded = (counts + tile - 1) // tile * tile
    seg_end = jnp.cumsum(padded)
    slot = (seg_end - padded)[gid] + rank
    tok = jnp.arange(n, dtype=jnp.int32)
    slot_ids = jnp.arange(n_slots + tile, dtype=jnp.int32)
    slot_src = jnp.zeros((n_slots + tile,), jnp.int32).at[slot].set(tok)
    slot_dst = (n + (slot_ids // tile % 2) * tile + slot_ids % tile).at[slot].set(tok)
    grp_gates = jnp.take_along_axis(gates.reshape(n, N_EXPERT_GROUPS, EXPERTS_PER_GROUP),
                                    gid[:, None, None], axis=1)[:, 0]
    slot_gates = jnp.zeros((n_slots, EXPERTS_PER_GROUP), F32).at[slot].set(grp_gates)
    tile_grp = jnp.minimum(jnp.searchsorted(seg_end, jnp.arange(n_tiles, dtype=jnp.int32) * tile, side="right"),
                           N_EXPERT_GROUPS - 1).astype(jnp.int32)
    return tile_grp, slot_src, slot_dst, slot_gates


def _moe_kernel(tile_grp_ref, src_ref, dst_ref, x_hbm, g_ref, wg_ref, wu_ref, wd_ref, ff_hbm,
                xg_sc, x_sc, acc_sc, ys_sc, gsem, ssem, *, tile, n_tok):
    del tile_grp_ref
    i = pl.program_id(0)
    e = pl.program_id(1)
    n_tiles = pl.num_programs(0)
    n_e = EXPERTS_PER_GROUP
    slab = ROW_CHUNKS

    def gather_copy(slot, row, buf):
        tok = src_ref[slot]
        return pltpu.make_async_copy(x_hbm.at[pl.ds(tok * slab, slab), :],
                                     xg_sc.at[buf, pl.ds(row * slab, slab), :], gsem.at[buf])

    def scatter_copy(slot, row, buf):
        tok = dst_ref[slot]
        return pltpu.make_async_copy(ys_sc.at[buf, pl.ds(row * slab, slab), :],
                                     ff_hbm.at[pl.ds(tok * slab, slab), :], ssem.at[buf])

    def wait_all(sc, sem, buf):
        pltpu.make_async_copy(sc.at[buf], sc.at[buf], sem.at[buf]).wait()

    cur = i % 2
    nxt = (i + 1) % 2

    @pl.when((i == 0) & (e == 0))
    def _():
        for j in range(tile):
            gather_copy(j, j, 0).start()
        ys_sc[...] = jnp.zeros(ys_sc.shape, F32)
        for b in range(2):
            fill = pltpu.make_async_copy(
                ys_sc.at[b], ff_hbm.at[pl.ds((n_tok + b * tile) * slab, tile * slab), :], ssem.at[b])
            fill.start()
            fill.wait()

    part_rows = tile // n_e
    for j in range(part_rows):
        row = e * part_rows + j
        gather_copy((i + 1) * tile + row, row, nxt).start()

    @pl.when(e == 0)
    def _():
        wait_all(xg_sc, gsem, cur)
        for c in range(ROW_CHUNKS):
            x_sc[:, c * LANES:(c + 1) * LANES] = xg_sc[cur, pl.ds(c, tile, stride=ROW_CHUNKS), :].astype(BF16)

    x = x_sc[...]
    hid = _silu(jnp.dot(x, wg_ref[...], preferred_element_type=F32)) * jnp.dot(x, wu_ref[...],
                                                                                preferred_element_type=F32)
    col = lax.broadcasted_iota(jnp.int32, g_ref.shape, 1)
    gate = jnp.sum(jnp.where(col == e, g_ref[...], 0.0), axis=-1, keepdims=True)
    part = jnp.dot((hid * gate).astype(BF16), wd_ref[...], preferred_element_type=F32)

    @pl.when(e == 0)
    def _():
        acc_sc[...] = part

    @pl.when(e > 0)
    def _():
        acc_sc[...] += part

    @pl.when(e == n_e - 1)
    def _():
        @pl.when(i >= 2)
        def _():
            wait_all(ys_sc, ssem, cur)

        y = acc_sc[...]
        for c in range(ROW_CHUNKS):
            ys_sc[cur, pl.ds(c, tile, stride=ROW_CHUNKS), :] = y[:, c * LANES:(c + 1) * LANES]
        for j in range(tile):
            scatter_copy(i * tile + j, j, cur).start()

        @pl.when(i == n_tiles - 1)
        def _():
            wait_all(ys_sc, ssem, cur)

            @pl.when(n_tiles >= 2)
            def _():
                wait_all(ys_sc, ssem, nxt)

            wait_all(xg_sc, gsem, nxt)


def _moe(x1r, gates, wg, wu, wd, *, tile=MOE_TILE):
    n = gates.shape[0]
    d, f = wg.shape[1], wg.shape[2]
    tile_grp, slot_src, slot_dst, slot_gates = _moe_routing(gates, tile)
    n_tiles = tile_grp.shape[0]
    n_rows = n + 2 * tile
    n_e = EXPERTS_PER_GROUP
    grid_spec = pltpu.PrefetchScalarGridSpec(
        num_scalar_prefetch=3,
        grid=(n_tiles, n_e),
        in_specs=[
            pl.BlockSpec(memory_space=pl.ANY),
            pl.BlockSpec((tile, n_e), lambda i, e, tg, s, t: (i, 0)),
            pl.BlockSpec((None, d, f), lambda i, e, tg, s, t: (tg[i] * n_e + e, 0, 0)),
            pl.BlockSpec((None, d, f), lambda i, e, tg, s, t: (tg[i] * n_e + e, 0, 0)),
            pl.BlockSpec((None, f, d), lambda i, e, tg, s, t: (tg[i] * n_e + e, 0, 0)),
        ],
        out_specs=pl.BlockSpec(memory_space=pl.ANY),
        scratch_shapes=[
            pltpu.VMEM((2, tile * ROW_CHUNKS, LANES), F32), pltpu.VMEM((tile, d), BF16),
            pltpu.VMEM((tile, d), F32), pltpu.VMEM((2, tile * ROW_CHUNKS, LANES), F32),
            pltpu.SemaphoreType.DMA((2,)), pltpu.SemaphoreType.DMA((2,)),
        ],
    )
    return pl.pallas_call(
        functools.partial(_moe_kernel, tile=tile, n_tok=n),
        grid_spec=grid_spec,
        out_shape=jax.ShapeDtypeStruct((n_rows * ROW_CHUNKS, LANES), F32),
        compiler_params=_cparams(2),
        name="moe_grouped",
    )(tile_grp, slot_src, slot_dst, x1r, slot_gates, wg, wu, wd)


def _post2_kernel(x1r_ref, x1b_ref, ffr_ref, p_ref, wgate_ref, wproj_ref, lw_ref, lb_ref, x2_ref, x2b_ref, *, alpha):
    tm = x1b_ref.shape[0]
    gate = jax.nn.sigmoid(jnp.dot(x1b_ref[...], wgate_ref[...], preferred_element_type=F32))
    ple = gate * jnp.dot(p_ref[...], wproj_ref[...], preferred_element_type=F32)
    x2 = _layer_norm(alpha * _load_token_rows(x1r_ref, tm) + _load_token_rows(ffr_ref, tm) + ple,
                     lw_ref[...], lb_ref[...])
    x2_ref[...] = x2
    x2b_ref[...] = x2.astype(BF16)


def _post2(x1r, x1b, ffr, p, w_gate_b, w_proj_b, lw, lb, *, alpha, tm=256):
    n, d = x1b.shape
    dp = p.shape[1]
    full = lambda shape: pl.BlockSpec(shape, lambda i: (0,) * len(shape))
    row = lambda width: pl.BlockSpec((tm, width), lambda i: (i, 0))
    slab = pl.BlockSpec((tm * ROW_CHUNKS, LANES), lambda i: (i, 0))
    return pl.pallas_call(
        functools.partial(_post2_kernel, alpha=alpha),
        grid=(n // tm,),
        in_specs=[slab, row(d), slab, row(dp), full((d, d)), full((dp, d)), full((1, d)), full((1, d))],
        out_specs=[row(d), row(d)],
        out_shape=[jax.ShapeDtypeStruct((n, d), F32), jax.ShapeDtypeStruct((n, d), BF16)],
        compiler_params=_cparams(1),
        name="ple_ln2",
    )(x1r, x1b, ffr, p, w_gate_b, w_proj_b, lw, lb)


def kernel(x_prompt, x_sample, p_prompt, p_sample, cache_diff_k, cache_diff_v, cache_moba_k, cache_moba_v, state_ssm, state_conv, page_table, w_in, w_out, diff_lambda, diff_norm_w, ssm_conv_w, ssm_conv_b, ssm_dt_bias, ssm_a_log, ssm_d, ssm_norm_w, ln1_w, ln1_b, ln2_w, ln2_b, w_router, b_router, w_exp_gate, w_exp_up, w_exp_down, w_ple_proj, w_ple_gate):
    depth = w_in.shape[0]
    batch, seq, d_model = x_prompt.shape
    n_seq, t_new, _ = x_sample.shape
    n_p, n_s = batch * seq, n_seq * t_new
    alpha = (2 * depth) ** 0.25
    n_heads = DA_HEADS + MB_HEADS
    slopes = jnp.asarray(2.0 ** (-8.0 * np.arange(1, n_heads + 1) / n_heads), dtype=F32)

    def score_col_slopes(sl, cols_per_head):
        v = jnp.repeat(sl, cols_per_head)
        return jnp.pad(v, (0, LANES - v.shape[0])).reshape(1, LANES)

    slopes_diff_cols = score_col_slopes(slopes[:DA_HEADS], 2 * t_new)
    slopes_moba_cols = score_col_slopes(slopes[DA_HEADS:], t_new)
    pad_lanes = lambda v: jnp.pad(v, (0, LANES - v.shape[0])).reshape(1, LANES)
    per_channel = lambda v: jnp.repeat(v, SSM_HEADDIM).reshape(1, D_SSM)

    n_pool, page = cache_diff_k.shape[1], cache_diff_k.shape[2]
    flat_cache = lambda c: c.reshape(depth, n_pool, page * DA_HEADS, HEAD_DIM)
    ck_a, cv_a, ck_b, cv_b = map(flat_cache, (cache_diff_k, cache_diff_v, cache_moba_k, cache_moba_v))

    x = jnp.concatenate([x_prompt.reshape(n_p, d_model), x_sample.reshape(n_s, d_model)], axis=0)
    xb = x.astype(BF16)
    outs_p = [[] for _ in range(6)]
    outs_s = [[] for _ in range(6)]
    for i in range(depth):
        lam_init = 0.8 - 0.6 * math.exp(-0.3 * i)
        w_in_b = jnp.pad(w_in[i].astype(BF16), ((0, 0), (0, D_IN_PAD - D_IN_PROJ)))
        z = _matmul(xb, w_in_b, tm=512, tn=D_IN_PAD // 3)
        zs = z[n_p:].reshape(n_seq, t_new, D_IN_PAD)
        w_dtx = jnp.repeat(w_in[i][:, DT_COL:DT_COL + SSM_HEADS].astype(BF16), SSM_HEADDIM, axis=1)
        dtx = _matmul(xb[n_p:], w_dtx, tm=n_s, tn=D_SSM).reshape(n_seq, t_new, D_SSM)

        dl = diff_lambda[i]
        nw_a = diff_norm_w[i].reshape(1, HEAD_DIM)
        cw, cb = ssm_conv_w[i], ssm_conv_b[i].reshape(1, D_XBC)
        nw_c = ssm_norm_w[i].reshape(1, D_SSM)

        ka, vt, kmean = _kv_prep(z, batch=batch, seq=seq)
        oa_p = _diff_prompt(z, ka, vt, slopes, dl, nw_a, batch=batch, seq=seq, lam_init=lam_init)
        ob_p = _moba_prompt(z, ka, vt, kmean, slopes, batch=batch, seq=seq)
        oc_p, h_p = _ssd_prompt(z, cw, cb, pad_lanes(ssm_dt_bias[i]), pad_lanes(ssm_a_log[i]),
                                per_channel(ssm_d[i]), nw_c, batch=batch, seq=seq)
        oa_s = _attn_decode(zs, ck_a, cv_a, page_table, slopes_diff_cols, dl, nw_a, layer=i, moba=False,
                            lam_init=lam_init)
        ob_s = _attn_decode(zs, ck_b, cv_b, page_table, slopes_moba_cols, dl, nw_a, layer=i, moba=True,
                            lam_init=lam_init)
        oc_s, h_s, conv_s = _ssd_decode(zs, dtx, state_conv, state_ssm, cw, cb, per_channel(ssm_dt_bias[i]),
                                        per_channel(ssm_a_log[i]), per_channel(ssm_d[i]), nw_c, layer=i)

        head_shape = lambda a, lead: a.reshape(*lead, DA_HEADS, HEAD_DIM)
        for j, blk in enumerate((KA_BLK, VA_BLK, KB_BLK, VB_BLK)):
            c0 = blk * LANES
            outs_p[j].append(head_shape(z[:n_p, c0:c0 + DA_HEADS * HEAD_DIM], (batch, seq)))
            outs_s[j].append(head_shape(zs[:, :, c0:c0 + DA_HEADS * HEAD_DIM], (n_seq, t_new)))
        outs_p[4].append(h_p)
        outs_p[5].append(jnp.stack([z[(b + 1) * seq - (CONV_W - 1):(b + 1) * seq, XS_COL:XS_COL + D_XBC]
                                    for b in range(batch)]))
        outs_s[4].append(h_s)
        outs_s[5].append(conv_s)

        oa = jnp.concatenate([oa_p, oa_s.reshape(n_s, -1)], axis=0)
        ob = jnp.concatenate([ob_p, ob_s.reshape(n_s, -1)], axis=0)
        oc = jnp.concatenate([oc_p, oc_s.reshape(n_s, -1)], axis=0)
        x1r, x1b, gates = _post1(x, oa, ob, oc, w_out[i].astype(BF16), ln1_w[i].reshape(1, -1),
                                 ln1_b[i].reshape(1, -1), w_router, b_router.reshape(1, -1), alpha=alpha)
        ffr = _moe(x1r, gates, w_exp_gate[i].astype(BF16), w_exp_up[i].astype(BF16), w_exp_down[i].astype(BF16))
        p = jnp.concatenate([p_prompt[i].reshape(n_p, -1), p_sample[i].reshape(n_s, -1)], axis=0).astype(BF16)
        x, xb = _post2(x1r, x1b, ffr, p, w_ple_gate[i].astype(BF16), w_ple_proj[i].astype(BF16),
                       ln2_w[i].reshape(1, -1), ln2_b[i].reshape(1, -1), alpha=alpha)

    y_p = x[:n_p].reshape(batch, seq, d_model)
    y_s = x[n_p:].reshape(n_seq, t_new, d_model)
    return (y_p, y_s, *[jnp.stack(o) for o in outs_p], *[jnp.stack(o) for o in outs_s])
```

```python
import functools
import math

import jax
import jax.numpy as jnp
import numpy as np
from jax import lax
from jax.experimental import pallas as pl
from jax.experimental.pallas import tpu as pltpu

F32 = jnp.float32
BF16 = jnp.bfloat16

LANES = 128
HEAD_DIM = 128
DA_HEADS = 4
DA_QK = 64
MB_HEADS = 4
MOBA_BLOCK = 256
MOBA_TOPK = 3
D_SSM = 1024
SSM_HEADS = 16
SSM_HEADDIM = 64
SSM_GROUPS = 2
D_STATE = 128
CONV_W = 4
D_XBC = D_SSM + 2 * SSM_GROUPS * D_STATE
N_EXPERTS = 16
N_EXPERT_GROUPS = 4
EXPERTS_PER_GROUP = 4
LN_EPS = 1e-5
RMS_EPS = 1e-6
VMEM_LIMIT = 56 * 1024 * 1024

QA_BLK, KA_BLK, VA_BLK, QB_BLK, KB_BLK, VB_BLK = 0, 4, 8, 12, 16, 20
ZC_COL, XS_COL, BC_COL, DT_COL = 3072, 4096, 5120, 5632
D_IN_PROJ = 5648
D_IN_PAD = 5760

NT_DIMS = (((1,), (1,)), ((), ()))
TN_DIMS = (((0,), (0,)), ((), ()))


def _cparams(n_axes):
    return pltpu.CompilerParams(dimension_semantics=("arbitrary",) * n_axes, vmem_limit_bytes=VMEM_LIMIT)


def _mm_kernel(x_ref, w_ref, o_ref):
    o_ref[...] = jnp.dot(x_ref[...], w_ref[...], preferred_element_type=F32).astype(o_ref.dtype)


def _matmul(x, w, tm, tn, out_dtype=F32):
    m, k = x.shape
    n = w.shape[1]
    return pl.pallas_call(
        _mm_kernel,
        grid=(n // tn, m // tm),
        in_specs=[pl.BlockSpec((tm, k), lambda j, i: (i, 0)), pl.BlockSpec((k, tn), lambda j, i: (0, j))],
        out_specs=pl.BlockSpec((tm, tn), lambda j, i: (i, j)),
        out_shape=jax.ShapeDtypeStruct((m, n), out_dtype),
        compiler_params=_cparams(2),
        name="in_proj",
    )(x, w)


POS_SPLIT = 64
ONEHOT_LANE0 = 8
MASK_BIAS = -1e30


ATT_TILE = 256
N_ATT_HEADS = DA_HEADS + MB_HEADS
HEADS_PER_PHASE = 4


def _kv_prep_kernel(k_ref, v_ref, ka_ref, vt_ref, km_ref, ko_ref, vo_ref, *, ck):
    pos = pl.program_id(1) * ck + lax.broadcasted_iota(jnp.int32, (ck, LANES), 0)
    lane = lax.broadcasted_iota(jnp.int32, (ck, LANES), 1)
    feat = jnp.where(lane == 0, pos // POS_SPLIT, jnp.where(lane == 1, pos % POS_SPLIT, jnp.where(lane == 2, 1, 0)))
    feat = jnp.where(lane - ONEHOT_LANE0 == pos // MOBA_BLOCK, 1, feat).astype(F32).astype(BF16)
    for hd in range(DA_HEADS):
        lanes = slice(hd * HEAD_DIM, (hd + 1) * HEAD_DIM)
        ka_ref[hd, :, :HEAD_DIM] = k_ref[:, lanes].astype(BF16)
        ka_ref[hd, :, HEAD_DIM:] = feat
        ko_ref[:, hd, :] = k_ref[:, lanes]
        vo_ref[:, hd, :] = v_ref[:, lanes]
        for j in range(ck // ATT_TILE):
            rows = slice(j * ATT_TILE, (j + 1) * ATT_TILE)
            vt_ref[hd, j] = v_ref[rows, lanes].T.astype(BF16)
            km_ref[hd, j:j + 1, :] = jnp.mean(k_ref[rows, lanes], axis=0, keepdims=True)


def _kv_prep(z, *, batch, seq, ck=2048):
    nc = seq // ck
    tiles = ck // ATT_TILE
    width = DA_HEADS * HEAD_DIM
    k_blk = lambda t: (KA_BLK + (KB_BLK - KA_BLK) * t) // DA_HEADS
    v_blk = lambda t: (VA_BLK + (VB_BLK - VA_BLK) * t) // DA_HEADS
    return pl.pallas_call(
        functools.partial(_kv_prep_kernel, ck=ck),
        grid=(batch, nc, 2),
        in_specs=[
            pl.BlockSpec((ck, width), lambda b, c, t: (b * nc + c, k_blk(t))),
            pl.BlockSpec((ck, width), lambda b, c, t: (b * nc + c, v_blk(t))),
        ],
        out_specs=[
            pl.BlockSpec((None, DA_HEADS, ck, 2 * HEAD_DIM), lambda b, c, t: (b, t, c, 0)),
            pl.BlockSpec((None, DA_HEADS, tiles, HEAD_DIM, ATT_TILE), lambda b, c, t: (b, t, c, 0, 0)),
            pl.BlockSpec((None, DA_HEADS, tiles, HEAD_DIM), lambda b, c, t: (b, t, c, 0)),
            pl.BlockSpec((None, None, ck, DA_HEADS, HEAD_DIM), lambda b, c, t: (t, b, c, 0, 0)),
            pl.BlockSpec((None, None, ck, DA_HEADS, HEAD_DIM), lambda b, c, t: (t, b, c, 0, 0)),
        ],
        out_shape=[
            jax.ShapeDtypeStruct((batch, N_ATT_HEADS, seq, 2 * HEAD_DIM), BF16),
            jax.ShapeDtypeStruct((batch, N_ATT_HEADS, seq // ATT_TILE, HEAD_DIM, ATT_TILE), BF16),
            jax.ShapeDtypeStruct((batch, N_ATT_HEADS, seq // ATT_TILE, HEAD_DIM), F32),
            jax.ShapeDtypeStruct((2, batch, seq, DA_HEADS, HEAD_DIM), F32),
            jax.ShapeDtypeStruct((2, batch, seq, DA_HEADS, HEAD_DIM), F32),
        ],
        compiler_params=_cparams(3),
        name="kv_prep",
    )(z, z)


def _query_feature_rows(slope, q0, n_cols):
    row = lax.broadcasted_iota(jnp.int32, (8, n_cols), 0)
    c2 = -(slope * q0.astype(F32))
    return jnp.where(row == 0, slope * POS_SPLIT, jnp.where(row == 1, slope, jnp.where(row == 2, c2, 0.0)))


def _softmax_t_step(n_heads, kj, ka_ref, vt_ref, qa_sc, m_sc, l_sc, acc_sc, bias=None):
    t = ATT_TILE
    rows = pl.ds(pl.multiple_of(kj * t, t), t)
    for h0 in range(0, n_heads, HEADS_PER_PHASE):
        heads = range(h0, min(h0 + HEADS_PER_PHASE, n_heads))
        scores = {h: jnp.dot(ka_ref[h, rows, :], qa_sc[h], preferred_element_type=F32) for h in heads}
        probs = {}
        for h in heads:
            s = scores[h] if bias is None else scores[h] + bias
            m = m_sc[h]
            m_new = jnp.maximum(m, jnp.max(s, axis=0, keepdims=True))
            alpha = jnp.exp(m - m_new)
            p = jnp.exp(s - m_new)
            l_sc[h] = alpha * l_sc[h] + jnp.sum(p, axis=0, keepdims=True)
            m_sc[h] = m_new
            probs[h] = (alpha, p.astype(BF16))
        for h in heads:
            alpha, p = probs[h]
            acc_sc[h] = alpha * acc_sc[h] + jnp.dot(vt_ref[h, kj], p, preferred_element_type=F32)


def _softmax_t_run(n_heads, own, ka_ref, vt_ref, mask_ref, qa_sc, m_sc, l_sc, acc_sc):
    m_sc[...] = jnp.full(m_sc.shape, -jnp.inf, F32)
    l_sc[...] = jnp.zeros(l_sc.shape, F32)
    acc_sc[...] = jnp.zeros(acc_sc.shape, F32)
    _softmax_t_step(n_heads, own, ka_ref, vt_ref, qa_sc, m_sc, l_sc, acc_sc, bias=mask_ref[...])

    def body(kj, carry):
        _softmax_t_step(n_heads, kj, ka_ref, vt_ref, qa_sc, m_sc, l_sc, acc_sc)
        return carry

    lax.fori_loop(0, own, body, 0)


def _causal_mask_bias(n_maps):
    t = ATT_TILE
    k_i = lax.broadcasted_iota(jnp.int32, (t, n_maps * t), 0)
    q_i = lax.broadcasted_iota(jnp.int32, (t, n_maps * t), 1) % t
    return jnp.where(k_i <= q_i, 0.0, -jnp.inf).astype(F32)


def _diff_prompt_kernel(slopes_ref, dl_ref, q_ref, ka_ref, vt_ref, mask_ref, nw_ref, o_ref,
                        qa_sc, m_sc, l_sc, acc_sc, *, lam_init):
    t = ATT_TILE
    qi = pl.program_id(1)
    q0 = qi * t
    n_cols = 2 * t
    for h in range(DA_HEADS):
        qt = q_ref[:, h * HEAD_DIM:(h + 1) * HEAD_DIM].T * (DA_QK ** -0.5)
        row = lax.broadcasted_iota(jnp.int32, qt.shape, 0)
        q_top = jnp.concatenate([jnp.where(row < DA_QK, qt, 0.0), jnp.where(row >= DA_QK, qt, 0.0)], axis=1)
        q_feat = jnp.concatenate([_query_feature_rows(slopes_ref[h], q0, n_cols),
                                  jnp.zeros((LANES - 8, n_cols), F32)], axis=0)
        qa_sc[h] = jnp.concatenate([q_top, q_feat], axis=0).astype(BF16)

    _softmax_t_run(DA_HEADS, qi, ka_ref, vt_ref, mask_ref, qa_sc, m_sc, l_sc, acc_sc)

    dl = dl_ref[...]
    lam = (jnp.exp(jnp.sum(dl[0:1] * dl[1:2], axis=-1, keepdims=True))
           - jnp.exp(jnp.sum(dl[2:3] * dl[3:4], axis=-1, keepdims=True)) + lam_init)
    for h in range(DA_HEADS):
        o2 = acc_sc[h] / l_sc[h]
        ot = o2[:, :t] - lam * o2[:, t:]
        var = jnp.mean(ot * ot, axis=0, keepdims=True)
        o = (ot * lax.rsqrt(var + RMS_EPS)).T * nw_ref[...] * (1.0 - lam_init)
        o_ref[:, h * HEAD_DIM:(h + 1) * HEAD_DIM] = o.astype(o_ref.dtype)


def _diff_prompt(z, ka, vt, slopes, dl, nw, *, batch, seq, lam_init):
    t = ATT_TILE
    nq = seq // t
    width = DA_HEADS * HEAD_DIM
    n_cols = 2 * t
    kern = functools.partial(_diff_prompt_kernel, lam_init=lam_init)
    return pl.pallas_call(
        kern,
        grid=(batch, nq),
        in_specs=[
            pl.BlockSpec(memory_space=pltpu.SMEM),
            pl.BlockSpec((4, DA_QK), lambda b, i: (0, 0)),
            pl.BlockSpec((t, width), lambda b, i: (b * nq + i, QA_BLK // DA_HEADS)),
            pl.BlockSpec((None, DA_HEADS, seq, 2 * HEAD_DIM), lambda b, i: (b, 0, 0, 0)),
            pl.BlockSpec((None, DA_HEADS, nq, HEAD_DIM, t), lambda b, i: (b, 0, 0, 0, 0)),
            pl.BlockSpec((t, n_cols), lambda b, i: (0, 0)),
            pl.BlockSpec((1, HEAD_DIM), lambda b, i: (0, 0)),
        ],
        out_specs=pl.BlockSpec((t, width), lambda b, i: (b * nq + i, 0)),
        out_shape=jax.ShapeDtypeStruct((batch * seq, width), BF16),
        scratch_shapes=[
            pltpu.VMEM((DA_HEADS, 2 * HEAD_DIM, n_cols), BF16),
            pltpu.VMEM((DA_HEADS, 1, n_cols), F32), pltpu.VMEM((DA_HEADS, 1, n_cols), F32),
            pltpu.VMEM((DA_HEADS, HEAD_DIM, n_cols), F32),
        ],
        compiler_params=_cparams(2),
        name="diff_prompt",
    )(slopes, dl, z, ka, vt, _causal_mask_bias(2), nw)


def _top3_mask_rows(gate_t, n_valid, own):
    nb = gate_t.shape[0]
    row = lax.broadcasted_iota(jnp.int32, gate_t.shape, 0)
    g = jnp.where(row < n_valid, gate_t, -jnp.inf)
    sel = row == own
    for _ in range(MOBA_TOPK):
        mx = jnp.max(g, axis=0, keepdims=True)
        idx = jnp.min(jnp.where(g == mx, row, nb), axis=0, keepdims=True)
        pick = (row == idx) & (mx > -jnp.inf)
        sel = sel | pick
        g = jnp.where(pick, -jnp.inf, g)
    return jnp.where(sel, 0.0, MASK_BIAS)


def _moba_prompt_kernel(slopes_ref, q_ref, ka_ref, vt_ref, km_ref, mask_ref, o_ref,
                        qa_sc, m_sc, l_sc, acc_sc, *, nb):
    blk = MOBA_BLOCK
    own = pl.program_id(1)
    q0 = own * blk
    for h in range(MB_HEADS):
        qt = q_ref[:, h * HEAD_DIM:(h + 1) * HEAD_DIM].T
        gate_t = jnp.dot(km_ref[h], qt, preferred_element_type=F32, precision=lax.Precision.HIGHEST)
        q_feat = jnp.concatenate([_query_feature_rows(slopes_ref[DA_HEADS + h], q0, blk),
                                  _top3_mask_rows(gate_t, own, own),
                                  jnp.zeros((LANES - ONEHOT_LANE0 - nb, blk), F32)], axis=0)
        qa_sc[h] = jnp.concatenate([qt * (HEAD_DIM ** -0.5), q_feat], axis=0).astype(BF16)

    _softmax_t_run(MB_HEADS, own, ka_ref, vt_ref, mask_ref, qa_sc, m_sc, l_sc, acc_sc)

    for h in range(MB_HEADS):
        o_ref[:, h * HEAD_DIM:(h + 1) * HEAD_DIM] = (acc_sc[h] / l_sc[h]).T.astype(o_ref.dtype)


def _moba_prompt(z, ka, vt, kmean, slopes, *, batch, seq):
    blk = MOBA_BLOCK
    assert blk == ATT_TILE
    nb = seq // blk
    assert nb <= LANES - ONEHOT_LANE0 and nb % 8 == 0
    width = MB_HEADS * HEAD_DIM
    kern = functools.partial(_moba_prompt_kernel, nb=nb)
    return pl.pallas_call(
        kern,
        grid=(batch, nb),
        in_specs=[
            pl.BlockSpec(memory_space=pltpu.SMEM),
            pl.BlockSpec((blk, width), lambda b, i: (b * nb + i, QB_BLK // MB_HEADS)),
            pl.BlockSpec((None, MB_HEADS, seq, 2 * HEAD_DIM), lambda b, i: (b, 1, 0, 0)),
            pl.BlockSpec((None, MB_HEADS, nb, HEAD_DIM, blk), lambda b, i: (b, 1, 0, 0, 0)),
            pl.BlockSpec((None, MB_HEADS, nb, HEAD_DIM), lambda b, i: (b, 1, 0, 0)),
            pl.BlockSpec((blk, blk), lambda b, i: (0, 0)),
        ],
        out_specs=pl.BlockSpec((blk, width), lambda b, i: (b * nb + i, 0)),
        out_shape=jax.ShapeDtypeStruct((batch * seq, width), BF16),
        scratch_shapes=[
            pltpu.VMEM((MB_HEADS, 2 * HEAD_DIM, blk), BF16),
            pltpu.VMEM((MB_HEADS, 1, blk), F32), pltpu.VMEM((MB_HEADS, 1, blk), F32),
            pltpu.VMEM((MB_HEADS, HEAD_DIM, blk), F32),
        ],
        compiler_params=_cparams(2),
        name="moba_prompt",
    )(slopes, z, ka, vt, kmean, _causal_mask_bias(1))


def _softplus(x):
    return jnp.maximum(x, 0.0) + jnp.log1p(jnp.exp(-jnp.abs(x)))


def _silu(x):
    return x * jax.nn.sigmoid(x)


def _conv_silu(xp_sc, w, b, t):
    y = b
    for j in range(CONV_W):
        y = y + xp_sc[5 + j:5 + j + t, :] * w[j:j + 1, :]
    return _silu(y)


def _ssd_prompt_kernel(zc_ref, x_ref, bc_ref, dt_ref, cw_ref, cb_ref, dtb_ref, alog_ref, dsk_ref, nw_ref,
                       y_ref, hfin_ref, xpx_sc, xpbc_sc, h_sc, y_sc, *, t):
    c = pl.program_id(1)
    nc = pl.num_programs(1)
    hp = LANES // SSM_HEADDIM
    hpg = SSM_HEADS // SSM_GROUPS

    @pl.when(c == 0)
    def _():
        xpx_sc[0:8, :] = jnp.zeros((8, D_SSM), F32)
        xpbc_sc[0:8, :] = jnp.zeros((8, D_XBC - D_SSM), F32)
        h_sc[...] = jnp.zeros(h_sc.shape, F32)

    xpx_sc[8:8 + t, :] = x_ref[...]
    xpbc_sc[8:8 + t, :] = bc_ref[...]
    cw = cw_ref[...]
    cb = cb_ref[...]
    xs = _conv_silu(xpx_sc, cw[:, :D_SSM], cb[:, :D_SSM], t)
    bc = _conv_silu(xpbc_sc, cw[:, D_SSM:], cb[:, D_SSM:], t)
    xpx_sc[5:8, :] = xpx_sc[t + 5:t + 8, :]
    xpbc_sc[5:8, :] = xpbc_sc[t + 5:t + 8, :]

    dt = _softplus(dt_ref[...] + dtb_ref[...])
    da = dt * (-jnp.exp(alog_ref[...]))
    r_i = lax.broadcasted_iota(jnp.int32, (t, t), 0)
    c_i = lax.broadcasted_iota(jnp.int32, (t, t), 1)
    tril = c_i <= r_i
    acs = jnp.dot(tril.astype(F32), da, preferred_element_type=F32, precision=lax.Precision.HIGHEST)
    acs_t = acs.T
    lane = lax.broadcasted_iota(jnp.int32, (t, LANES), 1)
    sub = lax.broadcasted_iota(jnp.int32, (LANES, LANES), 0)
    lo_lane = lane < SSM_HEADDIM
    lo_sub = sub < SSM_HEADDIM

    for g in range(SSM_GROUPS):
        bg = bc[:, g * D_STATE:(g + 1) * D_STATE]
        cg = bc[:, (SSM_GROUPS + g) * D_STATE:(SSM_GROUPS + g + 1) * D_STATE]
        bgb = bg.astype(BF16)
        cgb = cg.astype(BF16)
        gmat = lax.dot_general(cgb, bgb, NT_DIMS, preferred_element_type=F32)
        for pr in range(hpg // hp):
            ha = g * hpg + pr * hp
            cb0 = ha * SSM_HEADDIM
            x2 = xs[:, cb0:cb0 + LANES]
            dt2 = jnp.where(lo_lane, dt[:, ha:ha + 1], dt[:, ha + 1:ha + 2])
            xdt = x2 * dt2
            ydiag = jnp.zeros((t, LANES), F32)
            for u in range(hp):
                hh = ha + u
                seg = acs[:, hh:hh + 1] - acs_t[hh:hh + 1, :]
                lmat = jnp.exp(jnp.where(tril, seg, -jnp.inf))
                xu = jnp.where(lo_lane if u == 0 else ~lo_lane, xdt, 0.0)
                ydiag = ydiag + jnp.dot((gmat * lmat).astype(BF16), xu.astype(BF16), preferred_element_type=F32)
            h2 = h_sc[ha:ha + hp].reshape(LANES, D_STATE)
            yoff = lax.dot_general(cgb, h2.astype(BF16), NT_DIMS, preferred_element_type=F32)
            e2 = jnp.where(lo_lane, jnp.exp(acs[:, ha:ha + 1]), jnp.exp(acs[:, ha + 1:ha + 2]))
            y = ydiag + e2 * yoff + dsk_ref[:, cb0:cb0 + LANES] * x2
            y_sc[:, cb0:cb0 + LANES] = y * _silu(zc_ref[:, cb0:cb0 + LANES])
            last = acs[t - 1:t, :]
            te2 = jnp.where(lo_lane, jnp.exp(last[:, ha:ha + 1] - acs[:, ha:ha + 1]),
                            jnp.exp(last[:, ha + 1:ha + 2] - acs[:, ha + 1:ha + 2]))
            upd = lax.dot_general((xdt * te2).astype(BF16), bgb, TN_DIMS, preferred_element_type=F32)
            dec = jnp.where(lo_sub, jnp.exp(last[:, ha:ha + 1]), jnp.exp(last[:, ha + 1:ha + 2]))
            h_sc[ha:ha + hp] = (dec * h2 + upd).reshape(hp, SSM_HEADDIM, D_STATE)

    gw = D_SSM // SSM_GROUPS
    for g in range(SSM_GROUPS):
        yg = y_sc[:, g * gw:(g + 1) * gw]
        var = jnp.mean(yg * yg, axis=-1, keepdims=True)
        y_ref[:, g * gw:(g + 1) * gw] = (yg * lax.rsqrt(var + RMS_EPS) * nw_ref[:, g * gw:(g + 1) * gw]).astype(y_ref.dtype)

    @pl.when(c == nc - 1)
    def _():
        hfin_ref[0] = h_sc[...]


def _ssd_prompt(z, cw, cb, dtb, alog, dsk, nw, *, batch, seq, t=128):
    nc = seq // t
    kern = functools.partial(_ssd_prompt_kernel, t=t)
    full = lambda shape: pl.BlockSpec(shape, lambda b, c: (0,) * len(shape))
    return pl.pallas_call(
        kern,
        grid=(batch, nc),
        in_specs=[
            pl.BlockSpec((t, D_SSM), lambda b, c: (b * nc + c, ZC_COL // D_SSM)),
            pl.BlockSpec((t, D_SSM), lambda b, c: (b * nc + c, XS_COL // D_SSM)),
            pl.BlockSpec((t, 512), lambda b, c: (b * nc + c, BC_COL // 512)),
            pl.BlockSpec((t, LANES), lambda b, c: (b * nc + c, DT_COL // LANES)),
            full((CONV_W, D_XBC)), full((1, D_XBC)), full((1, LANES)), full((1, LANES)),
            full((1, D_SSM)), full((1, D_SSM)),
        ],
        out_specs=[
            pl.BlockSpec((t, D_SSM), lambda b, c: (b * nc + c, 0)),
            pl.BlockSpec((1, SSM_HEADS, SSM_HEADDIM, D_STATE), lambda b, c: (b, 0, 0, 0)),
        ],
        out_shape=[
            jax.ShapeDtypeStruct((batch * seq, D_SSM), BF16),
            jax.ShapeDtypeStruct((batch, SSM_HEADS, SSM_HEADDIM, D_STATE), F32),
        ],
        scratch_shapes=[
            pltpu.VMEM((t + 8, D_SSM), F32), pltpu.VMEM((t + 8, D_XBC - D_SSM), F32),
            pltpu.VMEM((SSM_HEADS, SSM_HEADDIM, D_STATE), F32), pltpu.VMEM((t, D_SSM), F32),
        ],
        compiler_params=_cparams(2),
        name="ssd_prompt",
    )(z, z, z, z, cw, cb, dtb, alog, dsk, nw)


NEW_ROWS = 16


SEQS_PER_STEP = 2


def _attn_decode_kernel(*refs, moba, n_pages, page, t_new, lam_init, spb):
    pt_ref = refs[0]
    del pt_ref
    slopes_ref, dl_ref, nw_ref, q_ref, kn_ref, vn_ref = refs[1:7]
    n_pg = spb * n_pages
    kp_refs = refs[7:7 + n_pg]
    vp_refs = refs[7 + n_pg:7 + 2 * n_pg]
    o_ref = refs[7 + 2 * n_pg]
    kb_sc, vb_sc, new_sc = refs[8 + 2 * n_pg:]
    past = n_pages * page
    width = DA_HEADS * HEAD_DIM
    cols_per_head = t_new if moba else 2 * t_new
    lanes_per_col_grp = HEAD_DIM if moba else DA_QK
    scale = HEAD_DIM ** -0.5 if moba else DA_QK ** -0.5
    seqs = range(spb)
    coli = lax.broadcasted_iota(jnp.int32, (1, LANES), 1)
    slope = slopes_ref[...]

    ksums = []
    for s in seqs:
        ksum = []
        for p in range(n_pages):
            ksum_p = []
            kp, vp = kp_refs[s * n_pages + p], vp_refs[s * n_pages + p]
            for hd in range(DA_HEADS):
                lo = hd * HEAD_DIM
                kh = kp[pl.ds(hd, page, stride=DA_HEADS), :]
                kb_sc[s, p * page:(p + 1) * page, lo:lo + HEAD_DIM] = kh.astype(BF16)
                vb_sc[s, p * page:(p + 1) * page, lo:lo + HEAD_DIM] = vp[pl.ds(hd, page, stride=DA_HEADS), :].astype(BF16)
                if moba:
                    ksum_p.append(jnp.sum(kh, axis=0, keepdims=True))
            if moba:
                ksum.append(jnp.concatenate(ksum_p, axis=1))
        ksums.append(ksum)
        for j, (src, dst) in enumerate(((kn_ref, kb_sc), (vn_ref, vb_sc))):
            new_sc[s, j] = jnp.zeros(new_sc.shape[2:], F32)
            new_sc[s, j, 0:t_new, :] = src[s]
            dst[s, past:past + NEW_ROWS, :] = new_sc[s, j].astype(BF16)

    qs, scores = [], []
    for s in seqs:
        q = q_ref[s]
        qt = jnp.concatenate([q] * (LANES // t_new), axis=0)
        r_i = lax.broadcasted_iota(jnp.int32, qt.shape, 0)
        l_i = lax.broadcasted_iota(jnp.int32, qt.shape, 1)
        qmat = jnp.where(l_i // lanes_per_col_grp == r_i // t_new, qt, 0.0)
        qs.append(q)
        scores.append(lax.dot_general(kb_sc[s], (qmat * scale).astype(BF16), NT_DIMS,
                                      preferred_element_type=F32))

    probs = []
    for s in seqs:
        rowp = lax.broadcasted_iota(jnp.int32, (past, LANES), 0)
        s_past = scores[s][:past] + slope * (rowp - past).astype(F32)
        rown = lax.broadcasted_iota(jnp.int32, (NEW_ROWS, LANES), 0)
        s_new = scores[s][past:] + slope * rown.astype(F32)
        s_new = jnp.where(rown <= coli % t_new, s_new, -jnp.inf)

        if moba:
            nbp = past // MOBA_BLOCK
            ppb = MOBA_BLOCK // page
            gate = jnp.zeros((nbp, LANES), F32)
            kmean = jnp.concatenate(
                [sum(ksums[s][n * ppb:(n + 1) * ppb]) for n in range(nbp)], axis=0) * (1.0 / MOBA_BLOCK)
            for i in range(MB_HEADS * t_new):
                hd, tok = i // t_new, i % t_new
                lo = hd * HEAD_DIM
                g_i = jnp.sum(kmean[:, lo:lo + HEAD_DIM] * qs[s][tok:tok + 1, lo:lo + HEAD_DIM], axis=-1,
                              keepdims=True)
                gate = jnp.where(coli == i, g_i, gate)
            rowb = lax.broadcasted_iota(jnp.int32, gate.shape, 0)
            sel = jnp.zeros(gate.shape, jnp.bool_)
            g = gate
            for _ in range(MOBA_TOPK):
                mx = jnp.max(g, axis=0, keepdims=True)
                idx = jnp.min(jnp.where(g == mx, rowb, nbp), axis=0, keepdims=True)
                pick = (rowb == idx) & (mx > -jnp.inf)
                sel = sel | pick
                g = jnp.where(pick, -jnp.inf, g)
            sel_bias = jnp.where(sel, 0.0, -jnp.inf)
            s_past = (s_past.reshape(nbp, MOBA_BLOCK, LANES) + sel_bias[:, None, :]).reshape(past, LANES)

        m = jnp.maximum(jnp.max(s_past, axis=0, keepdims=True), jnp.max(s_new, axis=0, keepdims=True))
        p_past = jnp.exp(s_past - m)
        p_new = jnp.exp(s_new - m)
        l = jnp.sum(p_past, axis=0, keepdims=True) + jnp.sum(p_new, axis=0, keepdims=True)
        probs.append((jnp.concatenate([p_past, p_new], axis=0).astype(BF16), l))

    outs_t = [lax.dot_general(vb_sc[s], probs[s][0], TN_DIMS, preferred_element_type=F32) for s in seqs]

    for s in seqs:
        o = outs_t[s].T
        l_col = jnp.broadcast_to(probs[s][1], (LANES, LANES)).T
        o = o / jnp.concatenate([l_col] * (width // LANES), axis=1)
        outs = []
        for hd in range(DA_HEADS):
            lo = hd * HEAD_DIM
            r0 = hd * cols_per_head
            if moba:
                outs.append(o[r0:r0 + t_new, lo:lo + HEAD_DIM])
            else:
                dl = dl_ref[...]
                lam = (jnp.exp(jnp.sum(dl[0:1] * dl[1:2], axis=-1, keepdims=True))
                       - jnp.exp(jnp.sum(dl[2:3] * dl[3:4], axis=-1, keepdims=True)) + lam_init)
                oh = o[r0:r0 + t_new, lo:lo + HEAD_DIM] - lam * o[r0 + t_new:r0 + 2 * t_new, lo:lo + HEAD_DIM]
                var = jnp.mean(oh * oh, axis=-1, keepdims=True)
                outs.append(oh * lax.rsqrt(var + RMS_EPS) * nw_ref[...] * (1.0 - lam_init))
        o_ref[s] = jnp.concatenate(outs, axis=1).astype(o_ref.dtype)


def _attn_decode(zs, cache_k, cache_v, page_table, slopes_cols, dl, nw, *, layer, moba, lam_init, spb=SEQS_PER_STEP):
    n_seq, t_new, _ = zs.shape
    page = cache_k.shape[2] // DA_HEADS
    n_pages = page_table.shape[1]
    width = DA_HEADS * HEAD_DIM
    qblk, kblk, vblk = (QB_BLK, KB_BLK, VB_BLK) if moba else (QA_BLK, KA_BLK, VA_BLK)
    past = n_pages * page
    spb = spb if n_seq % spb == 0 else 1

    def page_spec(s, p):
        return pl.BlockSpec((None, None, page * DA_HEADS, HEAD_DIM), lambda b, pt: (layer, pt[b * spb + s, p], 0, 0))

    def z_spec(blk):
        return pl.BlockSpec((spb, t_new, width), lambda b, pt: (b, 0, blk // DA_HEADS))

    kern = functools.partial(_attn_decode_kernel, moba=moba, n_pages=n_pages, page=page, t_new=t_new,
                             lam_init=lam_init, spb=spb)
    page_specs = [page_spec(s, p) for s in range(spb) for p in range(n_pages)]
    grid_spec = pltpu.PrefetchScalarGridSpec(
        num_scalar_prefetch=1,
        grid=(n_seq // spb,),
        in_specs=[
            pl.BlockSpec((1, LANES), lambda b, pt: (0, 0)),
            pl.BlockSpec((4, DA_QK), lambda b, pt: (0, 0)),
            pl.BlockSpec((1, HEAD_DIM), lambda b, pt: (0, 0)),
            z_spec(qblk), z_spec(kblk), z_spec(vblk),
        ] + page_specs * 2,
        out_specs=pl.BlockSpec((spb, t_new, width), lambda b, pt: (b, 0, 0)),
        scratch_shapes=[
            pltpu.VMEM((spb, past + NEW_ROWS, width), BF16), pltpu.VMEM((spb, past + NEW_ROWS, width), BF16),
            pltpu.VMEM((spb, 2, NEW_ROWS, width), F32),
        ],
    )
    return pl.pallas_call(
        kern,
        grid_spec=grid_spec,
        out_shape=jax.ShapeDtypeStruct((n_seq, t_new, width), BF16),
        compiler_params=_cparams(1),
        name="moba_decode" if moba else "diff_decode",
    )(page_table, slopes_cols, dl, nw, zs, zs, zs, *([cache_k] * len(page_specs)), *([cache_v] * len(page_specs)))


def _ssd_decode_kernel(zc_ref, xs_ref, bc_ref, dtx_ref, cst_ref, h0_ref, cw_ref, cb_ref, dtb_ref, alog_ref,
                       dsk_ref, nw_ref, y_ref, hout_ref, cout_ref, xp_sc, *, t_new):
    hpg = SSM_HEADS // SSM_GROUPS
    gw = D_SSM // SSM_GROUPS
    xp_sc[0:CONV_W - 1, :] = cst_ref[...]
    xp_sc[CONV_W - 1:CONV_W - 1 + t_new, :D_SSM] = xs_ref[...]
    xp_sc[CONV_W - 1:CONV_W - 1 + t_new, D_SSM:] = bc_ref[...]
    cout_ref[...] = xp_sc[t_new:t_new + CONV_W - 1, :]
    y = cb_ref[...]
    for j in range(CONV_W):
        y = y + xp_sc[j:j + t_new, :] * cw_ref[j:j + 1, :]
    xbc = _silu(y)
    xs = xbc[:, :D_SSM]
    dt = _softplus(dtx_ref[...] + dtb_ref[...])
    da = dt * (-jnp.exp(alog_ref[...]))
    acs = [da[0:1]]
    for s in range(1, t_new):
        acs.append(acs[-1] + da[s:s + 1])
    xdt = xs * dt
    last = acs[-1]
    wx = jnp.concatenate([xdt[s:s + 1] * jnp.exp(last - acs[s]) for s in range(t_new)], axis=0)
    dec = jnp.exp(last)
    d_hi = dec.astype(BF16).astype(F32)
    d_md = (dec - d_hi).astype(BF16).astype(F32)
    d_lo = (dec - d_hi - d_md).astype(BF16).astype(F32)
    dec3 = jnp.concatenate([d_hi, d_md, d_lo, jnp.zeros((5, D_SSM), F32)], axis=0).astype(BF16)
    ones = jnp.ones((8, D_STATE), BF16)
    zc = zc_ref[...]
    for g in range(SSM_GROUPS):
        sl = slice(g * gw, (g + 1) * gw)
        bg = xbc[:, D_SSM + g * D_STATE:D_SSM + (g + 1) * D_STATE]
        cg = xbc[:, D_SSM + (SSM_GROUPS + g) * D_STATE:D_SSM + (SSM_GROUPS + g + 1) * D_STATE]
        h0 = h0_ref[g * hpg:(g + 1) * hpg].reshape(gw, D_STATE)
        yoff = lax.dot_general(cg.astype(BF16), h0.astype(BF16), NT_DIMS, preferred_element_type=F32)
        rows = []
        for t in range(t_new):
            yt = jnp.exp(acs[t][:, sl]) * yoff[t:t + 1]
            for s in range(t + 1):
                gts = jnp.sum(cg[t:t + 1] * bg[s:s + 1], axis=-1, keepdims=True)
                yt = yt + gts * jnp.exp(acs[t][:, sl] - acs[s][:, sl]) * xdt[s:s + 1, sl]
            rows.append(yt)
        yg = jnp.concatenate(rows, axis=0) + dsk_ref[:, sl] * xs[:, sl]
        yg = yg * _silu(zc[:, sl])
        var = jnp.mean(yg * yg, axis=-1, keepdims=True)
        y_ref[:, sl] = (yg * lax.rsqrt(var + RMS_EPS) * nw_ref[:, sl]).astype(y_ref.dtype)
        wxg = wx[:, sl]
        wx_hi = wxg.astype(BF16).astype(F32)
        bg_hi = bg.astype(BF16).astype(F32)
        pad_rows = 16 - 3 * t_new
        lhs = jnp.concatenate([wx_hi, wxg - wx_hi, wx_hi, jnp.zeros((pad_rows, gw), F32)], axis=0).astype(BF16)
        rhs = jnp.concatenate([bg_hi, bg_hi, bg - bg_hi, jnp.zeros((pad_rows, D_STATE), F32)], axis=0).astype(BF16)
        upd = lax.dot_general(lhs, rhs, TN_DIMS, preferred_element_type=F32)
        dec_full = lax.dot_general(dec3[:, sl], ones, TN_DIMS, preferred_element_type=F32)
        hout_ref[g * hpg:(g + 1) * hpg] = (dec_full * h0 + upd).reshape(hpg, SSM_HEADDIM, D_STATE)


def _ssd_decode(zs, dtx, state_conv, state_ssm, cw, cb, dtbx, alogx, dskx, nw, *, layer):
    n_seq, t_new, _ = zs.shape
    kern = functools.partial(_ssd_decode_kernel, t_new=t_new)
    full = lambda shape: pl.BlockSpec(shape, lambda b: (0,) * len(shape))
    return pl.pallas_call(
        kern,
        grid=(n_seq,),
        in_specs=[
            pl.BlockSpec((None, t_new, D_SSM), lambda b: (b, 0, ZC_COL // D_SSM)),
            pl.BlockSpec((None, t_new, D_SSM), lambda b: (b, 0, XS_COL // D_SSM)),
            pl.BlockSpec((None, t_new, 512), lambda b: (b, 0, BC_COL // 512)),
            pl.BlockSpec((None, t_new, D_SSM), lambda b: (b, 0, 0)),
            pl.BlockSpec((None, None, CONV_W - 1, D_XBC), lambda b: (layer, b, 0, 0)),
            pl.BlockSpec((None, None, SSM_HEADS, SSM_HEADDIM, D_STATE), lambda b: (layer, b, 0, 0, 0)),
            full((CONV_W, D_XBC)), full((1, D_XBC)), full((1, D_SSM)), full((1, D_SSM)),
            full((1, D_SSM)), full((1, D_SSM)),
        ],
        out_specs=[
            pl.BlockSpec((None, t_new, D_SSM), lambda b: (b, 0, 0)),
            pl.BlockSpec((None, SSM_HEADS, SSM_HEADDIM, D_STATE), lambda b: (b, 0, 0, 0)),
            pl.BlockSpec((None, CONV_W - 1, D_XBC), lambda b: (b, 0, 0)),
        ],
        out_shape=[
            jax.ShapeDtypeStruct((n_seq, t_new, D_SSM), BF16),
            jax.ShapeDtypeStruct((n_seq, SSM_HEADS, SSM_HEADDIM, D_STATE), F32),
            jax.ShapeDtypeStruct((n_seq, CONV_W - 1, D_XBC), F32),
        ],
        scratch_shapes=[pltpu.VMEM((8, D_XBC), F32)],
        compiler_params=_cparams(1),
        name="ssd_decode",
    )(zs, zs, zs, dtx, state_conv, state_ssm, cw, cb, dtbx, alogx, dskx, nw)


def _layer_norm(x, w, b):
    mu = jnp.mean(x, axis=-1, keepdims=True)
    xc = x - mu
    var = jnp.mean(xc * xc, axis=-1, keepdims=True)
    return xc * lax.rsqrt(var + LN_EPS) * w + b


def _top2_of(vals, col, n):
    m1 = jnp.max(vals, axis=-1, keepdims=True)
    i1 = jnp.min(jnp.where(vals == m1, col, n), axis=-1, keepdims=True)
    rest = jnp.where(col == i1, -jnp.inf, vals)
    m2 = jnp.max(rest, axis=-1, keepdims=True)
    i2 = jnp.min(jnp.where(rest == m2, col, n), axis=-1, keepdims=True)
    return m1, i1, m2, i2


def _router_gates(logits, b_router):
    scores = jax.nn.sigmoid(logits)
    biased = scores + b_router
    col = lax.broadcasted_iota(jnp.int32, logits.shape, 1)
    grp = col // EXPERTS_PER_GROUP
    best = jnp.zeros((logits.shape[0], 1), jnp.int32)
    best_v = None
    for g in range(N_EXPERT_GROUPS):
        m1, _, m2, _ = _top2_of(jnp.where(grp == g, biased, -jnp.inf), col, N_EXPERTS)
        gs = m1 + m2
        if g == 0:
            best_v = gs
        else:
            upd = gs > best_v
            best = jnp.where(upd, g, best)
            best_v = jnp.where(upd, gs, best_v)
    _, i1, _, i2 = _top2_of(jnp.where(grp == best, biased, -jnp.inf), col, N_EXPERTS)
    w = jnp.where((col == i1) | (col == i2), scores, 0.0)
    return w / jnp.sum(w, axis=-1, keepdims=True)


ROW_CHUNKS = 16


def _store_token_rows(ref, val):
    rows = val.shape[0]
    for c in range(ROW_CHUNKS):
        ref[pl.ds(c, rows, stride=ROW_CHUNKS), :] = val[:, c * LANES:(c + 1) * LANES]


def _load_token_rows(ref, rows):
    return jnp.concatenate([ref[pl.ds(c, rows, stride=ROW_CHUNKS), :] for c in range(ROW_CHUNKS)], axis=1)


def _post1_kernel(x_ref, oa_ref, ob_ref, oc_ref, wa_ref, wb_ref, wc_ref, lw_ref, lb_ref, wr_ref, br_ref,
                  x1r_ref, x1b_ref, g_ref, *, alpha):
    tm = x_ref.shape[0]
    sub = 128
    mixes = []
    for r in range(tm // sub):
        rows = slice(r * sub, (r + 1) * sub)
        mixes.append(jnp.dot(oa_ref[rows, :], wa_ref[...], preferred_element_type=F32)
                     + jnp.dot(ob_ref[rows, :], wb_ref[...], preferred_element_type=F32)
                     + jnp.dot(oc_ref[rows, :], wc_ref[...], preferred_element_type=F32))
    for r in range(tm // sub):
        rows = slice(r * sub, (r + 1) * sub)
        x1 = _layer_norm(alpha * x_ref[rows, :] + mixes[r], lw_ref[...], lb_ref[...])
        _store_token_rows(x1r_ref.at[pl.ds(r * sub * ROW_CHUNKS, sub * ROW_CHUNKS), :], x1)
        x1b_ref[rows, :] = x1.astype(BF16)
        x1_hi = x1.astype(BF16)
        x1_lo = (x1 - x1_hi.astype(F32)).astype(BF16)
        logits = jnp.dot(jnp.concatenate([x1_hi, x1_lo, x1_hi], axis=1), wr_ref[...], preferred_element_type=F32)
        g_ref[rows, :] = _router_gates(logits, br_ref[...])


def _post1(x, oa, ob, oc, w_out_b, lw, lb, w_router, b_router, *, alpha, tm=512):
    n, d = x.shape
    assert d == ROW_CHUNKS * LANES
    full = lambda shape: pl.BlockSpec(shape, lambda i: (0,) * len(shape))
    row = lambda width: pl.BlockSpec((tm, width), lambda i: (i, 0))
    wa, wb = DA_HEADS * HEAD_DIM, MB_HEADS * HEAD_DIM
    wr_hi = w_router.astype(BF16)
    wr_lo = (w_router - wr_hi.astype(F32)).astype(BF16)
    w_router3 = jnp.concatenate([wr_hi, wr_hi, wr_lo], axis=0)
    return pl.pallas_call(
        functools.partial(_post1_kernel, alpha=alpha),
        grid=(n // tm,),
        in_specs=[
            row(d), row(wa), row(wb), row(D_SSM),
            pl.BlockSpec((wa, d), lambda i: (0, 0)), pl.BlockSpec((wb, d), lambda i: (1, 0)),
            pl.BlockSpec((D_SSM, d), lambda i: (1, 0)),
            full((1, d)), full((1, d)), full((3 * d, N_EXPERTS)), full((1, N_EXPERTS)),
        ],
        out_specs=[pl.BlockSpec((tm * ROW_CHUNKS, LANES), lambda i: (i, 0)), row(d), row(N_EXPERTS)],
        out_shape=[jax.ShapeDtypeStruct((n * ROW_CHUNKS, LANES), F32), jax.ShapeDtypeStruct((n, d), BF16),
                   jax.ShapeDtypeStruct((n, N_EXPERTS), F32)],
        compiler_params=_cparams(1),
        name="out_proj_ln1_router",
    )(x, oa, ob, oc, w_out_b, w_out_b, w_out_b, lw, lb, w_router3, b_router)


MOE_TILE = 256


def _moe_routing(gates, tile):
    n = gates.shape[0]
    n_tiles = n // tile + N_EXPERT_GROUPS
    n_slots = n_tiles * tile
    expert_ids = jnp.arange(N_EXPERTS, dtype=jnp.int32)[None, :]
    gid = jnp.min(jnp.where(gates > 0, expert_ids, N_EXPERTS), axis=1) // EXPERTS_PER_GROUP
    onehot = (gid[:, None] == jnp.arange(N_EXPERT_GROUPS, dtype=jnp.int32)[None, :]).astype(jnp.int32)
    rank = jnp.sum((jnp.cumsum(onehot, axis=0) - onehot) * onehot, axis=1)
    counts = jnp.sum(onehot, axis=0)
    padded = (counts + tile - 1) // tile * tile
    seg_end = jnp.cumsum(padded)
    slot = (seg_end - padded)[gid] + rank
    tok = jnp.arange(n, dtype=jnp.int32)
    slot_ids = jnp.arange(n_slots + tile, dtype=jnp.int32)
    slot_src = jnp.zeros((n_slots + tile,), jnp.int32).at[slot].set(tok)
    slot_dst = (n + (slot_ids // tile % 2) * tile + slot_ids % tile).at[slot].set(tok)
    grp_gates = jnp.take_along_axis(gates.reshape(n, N_EXPERT_GROUPS, EXPERTS_PER_GROUP),
                                    gid[:, None, None], axis=1)[:, 0]
    is_real = slot_dst[:n_slots] < n
    slot_gates = jnp.where(is_real[:, None], grp_gates[slot_src[:n_slots]], 0.0)
    tile_start = jnp.arange(n_tiles, dtype=jnp.int32) * tile
    tile_grp = jnp.minimum(jnp.sum((tile_start[:, None] >= seg_end[None, :]).astype(jnp.int32), axis=1),
                           N_EXPERT_GROUPS - 1)
    return tile_grp, slot_src, slot_dst, slot_gates


def _moe_kernel(tile_grp_ref, src_ref, dst_ref, x_hbm, g_ref, wg_ref, wu_ref, wd_ref, ff_hbm,
                xg_sc, x_sc, acc_sc, ys_sc, gsem, ssem, *, tile, n_tok):
    del tile_grp_ref
    i = pl.program_id(0)
    e = pl.program_id(1)
    n_tiles = pl.num_programs(0)
    n_e = EXPERTS_PER_GROUP
    slab = ROW_CHUNKS

    def gather_copy(slot, row, buf):
        tok = src_ref[slot]
        return pltpu.make_async_copy(x_hbm.at[pl.ds(tok * slab, slab), :],
                                     xg_sc.at[buf, pl.ds(row * slab, slab), :], gsem.at[buf])

    def scatter_copy(slot, row, buf):
        tok = dst_ref[slot]
        return pltpu.make_async_copy(ys_sc.at[buf, pl.ds(row * slab, slab), :],
                                     ff_hbm.at[pl.ds(tok * slab, slab), :], ssem.at[buf])

    def wait_all(sc, sem, buf):
        pltpu.make_async_copy(sc.at[buf], sc.at[buf], sem.at[buf]).wait()

    cur = i % 2
    nxt = (i + 1) % 2

    @pl.when((i == 0) & (e == 0))
    def _():
        for j in range(tile):
            gather_copy(j, j, 0).start()
        ys_sc[...] = jnp.zeros(ys_sc.shape, F32)
        for b in range(2):
            fill = pltpu.make_async_copy(
                ys_sc.at[b], ff_hbm.at[pl.ds((n_tok + b * tile) * slab, tile * slab), :], ssem.at[b])
            fill.start()
            fill.wait()

    part_rows = tile // n_e
    for j in range(part_rows):
        row = e * part_rows + j
        gather_copy((i + 1) * tile + row, row, nxt).start()

    @pl.when(e == 0)
    def _():
        wait_all(xg_sc, gsem, cur)
        for c in range(ROW_CHUNKS):
            x_sc[:, c * LANES:(c + 1) * LANES] = xg_sc[cur, pl.ds(c, tile, stride=ROW_CHUNKS), :].astype(BF16)

    x = x_sc[...]
    hid = _silu(jnp.dot(x, wg_ref[...], preferred_element_type=F32)) * jnp.dot(x, wu_ref[...],
                                                                                preferred_element_type=F32)
    col = lax.broadcasted_iota(jnp.int32, g_ref.shape, 1)
    gate = jnp.sum(jnp.where(col == e, g_ref[...], 0.0), axis=-1, keepdims=True)
    part = jnp.dot((hid * gate).astype(BF16), wd_ref[...], preferred_element_type=F32)

    @pl.when(e == 0)
    def _():
        acc_sc[...] = part

    @pl.when(e > 0)
    def _():
        acc_sc[...] += part

    @pl.when(e == n_e - 1)
    def _():
        @pl.when(i >= 2)
        def _():
            wait_all(ys_sc, ssem, cur)

        y = acc_sc[...]
        for c in range(ROW_CHUNKS):
            ys_sc[cur, pl.ds(c, tile, stride=ROW_CHUNKS), :] = y[:, c * LANES:(c + 1) * LANES]
        for j in range(tile):
            scatter_copy(i * tile + j, j, cur).start()

        @pl.when(i == n_tiles - 1)
        def _():
            wait_all(ys_sc, ssem, cur)

            @pl.when(n_tiles >= 2)
            def _():
                wait_all(ys_sc, ssem, nxt)

            wait_all(xg_sc, gsem, nxt)


def _moe(x1r, gates, wg, wu, wd, *, tile=MOE_TILE):
    n = gates.shape[0]
    d, f = wg.shape[1], wg.shape[2]
    tile_grp, slot_src, slot_dst, slot_gates = _moe_routing(gates, tile)
    n_tiles = tile_grp.shape[0]
    n_rows = n + 2 * tile
    n_e = EXPERTS_PER_GROUP
    grid_spec = pltpu.PrefetchScalarGridSpec(
        num_scalar_prefetch=3,
        grid=(n_tiles, n_e),
        in_specs=[
            pl.BlockSpec(memory_space=pl.ANY),
            pl.BlockSpec((tile, n_e), lambda i, e, tg, s, t: (i, 0)),
            pl.BlockSpec((None, d, f), lambda i, e, tg, s, t: (tg[i] * n_e + e, 0, 0)),
            pl.BlockSpec((None, d, f), lambda i, e, tg, s, t: (tg[i] * n_e + e, 0, 0)),
            pl.BlockSpec((None, f, d), lambda i, e, tg, s, t: (tg[i] * n_e + e, 0, 0)),
        ],
        out_specs=pl.BlockSpec(memory_space=pl.ANY),
        scratch_shapes=[
            pltpu.VMEM((2, tile * ROW_CHUNKS, LANES), F32), pltpu.VMEM((tile, d), BF16),
            pltpu.VMEM((tile, d), F32), pltpu.VMEM((2, tile * ROW_CHUNKS, LANES), F32),
            pltpu.SemaphoreType.DMA((2,)), pltpu.SemaphoreType.DMA((2,)),
        ],
    )
    return pl.pallas_call(
        functools.partial(_moe_kernel, tile=tile, n_tok=n),
        grid_spec=grid_spec,
        out_shape=jax.ShapeDtypeStruct((n_rows * ROW_CHUNKS, LANES), F32),
        compiler_params=_cparams(2),
        name="moe_grouped",
    )(tile_grp, slot_src, slot_dst, x1r, slot_gates, wg, wu, wd)


def _post2_kernel(x1r_ref, x1b_ref, ffr_ref, p_ref, wgate_ref, wproj_ref, lw_ref, lb_ref, x2_ref, x2b_ref, *, alpha):
    tm = x1b_ref.shape[0]
    sub = 128
    gates, projs = [], []
    for r in range(tm // sub):
        rows = slice(r * sub, (r + 1) * sub)
        gates.append(jnp.dot(x1b_ref[rows, :], wgate_ref[...], preferred_element_type=F32))
        projs.append(jnp.dot(p_ref[rows, :], wproj_ref[...], preferred_element_type=F32))
    for r in range(tm // sub):
        rows = slice(r * sub, (r + 1) * sub)
        slab = pl.ds(r * sub * ROW_CHUNKS, sub * ROW_CHUNKS)
        ple = jax.nn.sigmoid(gates[r]) * projs[r]
        x2 = _layer_norm(alpha * _load_token_rows(x1r_ref.at[slab, :], sub)
                         + _load_token_rows(ffr_ref.at[slab, :], sub) + ple, lw_ref[...], lb_ref[...])
        x2_ref[rows, :] = x2
        x2b_ref[rows, :] = x2.astype(BF16)


def _post2(x1r, x1b, ffr, p, w_gate_b, w_proj_b, lw, lb, *, alpha, tm=512):
    n, d = x1b.shape
    dp = p.shape[1]
    full = lambda shape: pl.BlockSpec(shape, lambda i: (0,) * len(shape))
    row = lambda width: pl.BlockSpec((tm, width), lambda i: (i, 0))
    slab = pl.BlockSpec((tm * ROW_CHUNKS, LANES), lambda i: (i, 0))
    return pl.pallas_call(
        functools.partial(_post2_kernel, alpha=alpha),
        grid=(n // tm,),
        in_specs=[slab, row(d), slab, row(dp), full((d, d)), full((dp, d)), full((1, d)), full((1, d))],
        out_specs=[row(d), row(d)],
        out_shape=[jax.ShapeDtypeStruct((n, d), F32), jax.ShapeDtypeStruct((n, d), BF16)],
        compiler_params=_cparams(1),
        name="ple_ln2",
    )(x1r, x1b, ffr, p, w_gate_b, w_proj_b, lw, lb)


def kernel(x_prompt, x_sample, p_prompt, p_sample, cache_diff_k, cache_diff_v, cache_moba_k, cache_moba_v, state_ssm, state_conv, page_table, w_in, w_out, diff_lambda, diff_norm_w, ssm_conv_w, ssm_conv_b, ssm_dt_bias, ssm_a_log, ssm_d, ssm_norm_w, ln1_w, ln1_b, ln2_w, ln2_b, w_router, b_router, w_exp_gate, w_exp_up, w_exp_down, w_ple_proj, w_ple_gate):
    depth = w_in.shape[0]
    batch, seq, d_model = x_prompt.shape
    n_seq, t_new, _ = x_sample.shape
    n_p, n_s = batch * seq, n_seq * t_new
    alpha = (2 * depth) ** 0.25
    n_heads = DA_HEADS + MB_HEADS
    slopes = jnp.asarray(2.0 ** (-8.0 * np.arange(1, n_heads + 1) / n_heads), dtype=F32)

    def score_col_slopes(sl, cols_per_head):
        v = jnp.repeat(sl, cols_per_head)
        return jnp.pad(v, (0, LANES - v.shape[0])).reshape(1, LANES)

    slopes_diff_cols = score_col_slopes(slopes[:DA_HEADS], 2 * t_new)
    slopes_moba_cols = score_col_slopes(slopes[DA_HEADS:], t_new)
    pad_lanes = lambda v: jnp.pad(v, (0, LANES - v.shape[0])).reshape(1, LANES)
    per_channel = lambda v: jnp.repeat(v, SSM_HEADDIM).reshape(1, D_SSM)

    n_pool, page = cache_diff_k.shape[1], cache_diff_k.shape[2]
    flat_cache = lambda c: c.reshape(depth, n_pool, page * DA_HEADS, HEAD_DIM)
    ck_a, cv_a, ck_b, cv_b = map(flat_cache, (cache_diff_k, cache_diff_v, cache_moba_k, cache_moba_v))

    x = jnp.concatenate([x_prompt.reshape(n_p, d_model), x_sample.reshape(n_s, d_model)], axis=0)
    xb = x.astype(BF16)
    outs_p = [[] for _ in range(6)]
    outs_s = [[] for _ in range(6)]
    for i in range(depth):
        lam_init = 0.8 - 0.6 * math.exp(-0.3 * i)
        w_in_b = jnp.pad(w_in[i].astype(BF16), ((0, 0), (0, D_IN_PAD - D_IN_PROJ)))
        z = _matmul(xb, w_in_b, tm=512, tn=D_IN_PAD // 3)
        zs = z[n_p:].reshape(n_seq, t_new, D_IN_PAD)
        w_dtx = jnp.repeat(w_in[i][:, DT_COL:DT_COL + SSM_HEADS].astype(BF16), SSM_HEADDIM, axis=1)
        dtx = _matmul(xb[n_p:], w_dtx, tm=n_s, tn=D_SSM).reshape(n_seq, t_new, D_SSM)

        dl = diff_lambda[i]
        nw_a = diff_norm_w[i].reshape(1, HEAD_DIM)
        cw, cb = ssm_conv_w[i], ssm_conv_b[i].reshape(1, D_XBC)
        nw_c = ssm_norm_w[i].reshape(1, D_SSM)

        ka, vt, kmean, k_out, v_out = _kv_prep(z, batch=batch, seq=seq)
        oa_p = _diff_prompt(z, ka, vt, slopes, dl, nw_a, batch=batch, seq=seq, lam_init=lam_init)
        ob_p = _moba_prompt(z, ka, vt, kmean, slopes, batch=batch, seq=seq)
        oc_p, h_p = _ssd_prompt(z, cw, cb, pad_lanes(ssm_dt_bias[i]), pad_lanes(ssm_a_log[i]),
                                per_channel(ssm_d[i]), nw_c, batch=batch, seq=seq)
        oa_s = _attn_decode(zs, ck_a, cv_a, page_table, slopes_diff_cols, dl, nw_a, layer=i, moba=False,
                            lam_init=lam_init)
        ob_s = _attn_decode(zs, ck_b, cv_b, page_table, slopes_moba_cols, dl, nw_a, layer=i, moba=True,
                            lam_init=lam_init)
        oc_s, h_s, conv_s = _ssd_decode(zs, dtx, state_conv, state_ssm, cw, cb, per_channel(ssm_dt_bias[i]),
                                        per_channel(ssm_a_log[i]), per_channel(ssm_d[i]), nw_c, layer=i)

        head_shape = lambda a, lead: a.reshape(*lead, DA_HEADS, HEAD_DIM)
        for j, (blk, kv, typ) in enumerate(((KA_BLK, k_out, 0), (VA_BLK, v_out, 0), (KB_BLK, k_out, 1),
                                            (VB_BLK, v_out, 1))):
            c0 = blk * LANES
            outs_p[j].append(kv[typ])
            outs_s[j].append(head_shape(zs[:, :, c0:c0 + DA_HEADS * HEAD_DIM], (n_seq, t_new)))
        outs_p[4].append(h_p)
        outs_p[5].append(jnp.stack([z[(b + 1) * seq - (CONV_W - 1):(b + 1) * seq, XS_COL:XS_COL + D_XBC]
                                    for b in range(batch)]))
        outs_s[4].append(h_s)
        outs_s[5].append(conv_s)

        oa = jnp.concatenate([oa_p, oa_s.reshape(n_s, -1)], axis=0)
        ob = jnp.concatenate([ob_p, ob_s.reshape(n_s, -1)], axis=0)
        oc = jnp.concatenate([oc_p, oc_s.reshape(n_s, -1)], axis=0)
        x1r, x1b, gates = _post1(x, oa, ob, oc, w_out[i].astype(BF16), ln1_w[i].reshape(1, -1),
                                 ln1_b[i].reshape(1, -1), w_router, b_router.reshape(1, -1), alpha=alpha)
        ffr = _moe(x1r, gates, w_exp_gate[i].astype(BF16), w_exp_up[i].astype(BF16), w_exp_down[i].astype(BF16))
        p = jnp.concatenate([p_prompt[i].reshape(n_p, -1), p_sample[i].reshape(n_s, -1)], axis=0).astype(BF16)
        x, xb = _post2(x1r, x1b, ffr, p, w_ple_gate[i].astype(BF16), w_ple_proj[i].astype(BF16),
                       ln2_w[i].reshape(1, -1), ln2_b[i].reshape(1, -1), alpha=alpha)

    y_p = x[:n_p].reshape(batch, seq, d_model)
    y_s = x[n_p:].reshape(n_seq, t_new, d_model)
    return (y_p, y_s, *[jnp.stack(o) for o in outs_p], *[jnp.stack(o) for o in outs_s])
```

```python
import functools
import math

import jax
import jax.numpy as jnp
import numpy as np
from jax import lax
from jax.experimental import pallas as pl
from jax.experimental.pallas import tpu as pltpu

F32 = jnp.float32
BF16 = jnp.bfloat16

LANES = 128
HEAD_DIM = 128
DA_HEADS = 4
DA_QK = 64
MB_HEADS = 4
MOBA_BLOCK = 256
MOBA_TOPK = 3
D_SSM = 1024
SSM_HEADS = 16
SSM_HEADDIM = 64
SSM_GROUPS = 2
D_STATE = 128
CONV_W = 4
D_XBC = D_SSM + 2 * SSM_GROUPS * D_STATE
N_EXPERTS = 16
N_EXPERT_GROUPS = 4
EXPERTS_PER_GROUP = 4
LN_EPS = 1e-5
RMS_EPS = 1e-6
VMEM_LIMIT = 56 * 1024 * 1024

QA_BLK, KA_BLK, VA_BLK, QB_BLK, KB_BLK, VB_BLK = 0, 4, 8, 12, 16, 20
ZC_COL, XS_COL, BC_COL, DT_COL = 3072, 4096, 5120, 5632
D_IN_PROJ = 5648
D_IN_PAD = 5760

NT_DIMS = (((1,), (1,)), ((), ()))
TN_DIMS = (((0,), (0,)), ((), ()))


def _cparams(n_axes):
    return pltpu.CompilerParams(dimension_semantics=("arbitrary",) * n_axes, vmem_limit_bytes=VMEM_LIMIT)


def _mm_kernel(x_ref, w_ref, o_ref):
    o_ref[...] = jnp.dot(x_ref[...], w_ref[...], preferred_element_type=F32).astype(o_ref.dtype)


def _matmul(x, w, layer, tm, tn, out_dtype=F32):
    m, k = x.shape
    n = w.shape[2]
    return pl.pallas_call(
        _mm_kernel,
        grid=(n // tn, m // tm),
        in_specs=[pl.BlockSpec((tm, k), lambda j, i: (i, 0)), pl.BlockSpec((None, k, tn), lambda j, i: (layer, 0, j))],
        out_specs=pl.BlockSpec((tm, tn), lambda j, i: (i, j)),
        out_shape=jax.ShapeDtypeStruct((m, n), out_dtype),
        compiler_params=_cparams(2),
        name="in_proj",
    )(x, w)


POS_SPLIT = 64
ONEHOT_LANE0 = 8
MASK_BIAS = -1e30


ATT_TILE = 256
N_ATT_HEADS = DA_HEADS + MB_HEADS
HEADS_PER_PHASE = 4


def _kv_prep_kernel(k_ref, v_ref, ka_ref, vt_ref, km_ref, ko_a_ref, vo_a_ref, ko_b_ref, vo_b_ref, *, ck):
    typ = pl.program_id(1)
    c = pl.program_id(2)
    tiles = ck // ATT_TILE
    pos = c * ck + lax.broadcasted_iota(jnp.int32, (ck, LANES), 0)
    lane = lax.broadcasted_iota(jnp.int32, (ck, LANES), 1)
    feat = jnp.where(lane == 0, pos // POS_SPLIT, jnp.where(lane == 1, pos % POS_SPLIT, jnp.where(lane == 2, 1, 0)))
    feat = jnp.where(lane - ONEHOT_LANE0 == pos // MOBA_BLOCK, 1, feat).astype(F32).astype(BF16)
    for hd in range(DA_HEADS):
        lanes = slice(hd * HEAD_DIM, (hd + 1) * HEAD_DIM)
        ka_ref[hd, :, :HEAD_DIM] = k_ref[:, lanes].astype(BF16)
        ka_ref[hd, :, HEAD_DIM:] = feat
        for j in range(tiles):
            rows = slice(j * ATT_TILE, (j + 1) * ATT_TILE)
            vt_ref[hd, j] = v_ref[rows, lanes].T.astype(BF16)
            km_ref[hd, pl.ds(c * tiles + j, 1), :] = jnp.mean(k_ref[rows, lanes], axis=0, keepdims=True)
    for t, (ko_ref, vo_ref) in enumerate(((ko_a_ref, vo_a_ref), (ko_b_ref, vo_b_ref))):
        @pl.when(typ == t)
        def _():
            for hd in range(DA_HEADS):
                lanes = slice(hd * HEAD_DIM, (hd + 1) * HEAD_DIM)
                ko_ref[:, hd, :] = k_ref[:, lanes]
                vo_ref[:, hd, :] = v_ref[:, lanes]


def _kv_prep(z, *, batch, seq, ck=1024):
    nc = seq // ck
    nq = seq // ATT_TILE
    tiles = ck // ATT_TILE
    width = DA_HEADS * HEAD_DIM
    k_blk = lambda t: (KA_BLK + (KB_BLK - KA_BLK) * t) // DA_HEADS
    v_blk = lambda t: (VA_BLK + (VB_BLK - VA_BLK) * t) // DA_HEADS
    new_a = pl.BlockSpec((None, ck, DA_HEADS, HEAD_DIM), lambda b, t, c: (b, jnp.where(t == 0, c, nc - 1), 0, 0))
    new_b = pl.BlockSpec((None, ck, DA_HEADS, HEAD_DIM), lambda b, t, c: (b, jnp.where(t == 1, c, 0), 0, 0))
    return pl.pallas_call(
        functools.partial(_kv_prep_kernel, ck=ck),
        grid=(batch, 2, nc),
        in_specs=[
            pl.BlockSpec((ck, width), lambda b, t, c: (b * nc + c, k_blk(t))),
            pl.BlockSpec((ck, width), lambda b, t, c: (b * nc + c, v_blk(t))),
        ],
        out_specs=[
            pl.BlockSpec((None, DA_HEADS, ck, 2 * HEAD_DIM), lambda b, t, c: (b, t, c, 0)),
            pl.BlockSpec((None, DA_HEADS, tiles, HEAD_DIM, ATT_TILE), lambda b, t, c: (b, t, c, 0, 0)),
            pl.BlockSpec((None, DA_HEADS, nq, HEAD_DIM), lambda b, t, c: (b, t, 0, 0)),
            new_a, new_a, new_b, new_b,
        ],
        out_shape=[
            jax.ShapeDtypeStruct((batch, N_ATT_HEADS, seq, 2 * HEAD_DIM), BF16),
            jax.ShapeDtypeStruct((batch, N_ATT_HEADS, seq // ATT_TILE, HEAD_DIM, ATT_TILE), BF16),
            jax.ShapeDtypeStruct((batch, N_ATT_HEADS, seq // ATT_TILE, HEAD_DIM), F32),
        ] + [jax.ShapeDtypeStruct((batch, seq, DA_HEADS, HEAD_DIM), F32)] * 4,
        compiler_params=_cparams(3),
        name="kv_prep",
    )(z, z)


def _query_feature_rows(slope, q0, n_cols):
    row = lax.broadcasted_iota(jnp.int32, (8, n_cols), 0)
    c2 = -(slope * q0.astype(F32))
    return jnp.where(row == 0, slope * POS_SPLIT, jnp.where(row == 1, slope, jnp.where(row == 2, c2, 0.0)))


def _softmax_t_step(n_heads, kj, ka_ref, vt_ref, qa_sc, m_sc, l_sc, acc_sc, bias=None):
    t = ATT_TILE
    rows = pl.ds(pl.multiple_of(kj * t, t), t)
    for h0 in range(0, n_heads, HEADS_PER_PHASE):
        heads = range(h0, min(h0 + HEADS_PER_PHASE, n_heads))
        scores = {h: jnp.dot(ka_ref[h, rows, :], qa_sc[h], preferred_element_type=F32) for h in heads}
        probs = {}
        for h in heads:
            s = scores[h] if bias is None else scores[h] + bias
            m = m_sc[h]
            m_new = jnp.maximum(m, jnp.max(s, axis=0, keepdims=True))
            alpha = jnp.exp(m - m_new)
            p = jnp.exp(s - m_new)
            l_sc[h] = alpha * l_sc[h] + jnp.sum(p, axis=0, keepdims=True)
            m_sc[h] = m_new
            probs[h] = (alpha, p.astype(BF16))
        for h in heads:
            alpha, p = probs[h]
            acc_sc[h] = alpha * acc_sc[h] + jnp.dot(vt_ref[h, kj], p, preferred_element_type=F32)


def _softmax_t_run(n_heads, own, ka_ref, vt_ref, mask_ref, qa_sc, m_sc, l_sc, acc_sc):
    m_sc[...] = jnp.full(m_sc.shape, -jnp.inf, F32)
    l_sc[...] = jnp.zeros(l_sc.shape, F32)
    acc_sc[...] = jnp.zeros(acc_sc.shape, F32)
    _softmax_t_step(n_heads, own, ka_ref, vt_ref, qa_sc, m_sc, l_sc, acc_sc, bias=mask_ref[...])

    def body(kj, carry):
        _softmax_t_step(n_heads, kj, ka_ref, vt_ref, qa_sc, m_sc, l_sc, acc_sc)
        return carry

    lax.fori_loop(0, own, body, 0)


def _causal_mask_bias(n_maps):
    t = ATT_TILE
    k_i = lax.broadcasted_iota(jnp.int32, (t, n_maps * t), 0)
    q_i = lax.broadcasted_iota(jnp.int32, (t, n_maps * t), 1) % t
    return jnp.where(k_i <= q_i, 0.0, -jnp.inf).astype(F32)


def _diff_prompt_kernel(slopes_ref, dl_ref, q_ref, ka_ref, vt_ref, mask_ref, nw_ref, o_ref,
                        qa_sc, m_sc, l_sc, acc_sc, *, lam_init):
    t = ATT_TILE
    qi = pl.program_id(1)
    q0 = qi * t
    n_cols = 2 * t
    for h in range(DA_HEADS):
        qt = q_ref[:, h * HEAD_DIM:(h + 1) * HEAD_DIM].T * (DA_QK ** -0.5)
        row = lax.broadcasted_iota(jnp.int32, qt.shape, 0)
        q_top = jnp.concatenate([jnp.where(row < DA_QK, qt, 0.0), jnp.where(row >= DA_QK, qt, 0.0)], axis=1)
        q_feat = jnp.concatenate([_query_feature_rows(slopes_ref[h], q0, n_cols),
                                  jnp.zeros((LANES - 8, n_cols), F32)], axis=0)
        qa_sc[h] = jnp.concatenate([q_top, q_feat], axis=0).astype(BF16)

    _softmax_t_run(DA_HEADS, qi, ka_ref, vt_ref, mask_ref, qa_sc, m_sc, l_sc, acc_sc)

    dl = dl_ref[...]
    lam = (jnp.exp(jnp.sum(dl[0:1] * dl[1:2], axis=-1, keepdims=True))
           - jnp.exp(jnp.sum(dl[2:3] * dl[3:4], axis=-1, keepdims=True)) + lam_init)
    for h in range(DA_HEADS):
        o2 = acc_sc[h] / l_sc[h]
        ot = o2[:, :t] - lam * o2[:, t:]
        var = jnp.mean(ot * ot, axis=0, keepdims=True)
        o = (ot * lax.rsqrt(var + RMS_EPS)).T * nw_ref[...] * (1.0 - lam_init)
        o_ref[:, h * HEAD_DIM:(h + 1) * HEAD_DIM] = o.astype(o_ref.dtype)


def _diff_prompt(z, ka, vt, slopes, dl, nw, *, batch, seq, lam_init):
    t = ATT_TILE
    nq = seq // t
    width = DA_HEADS * HEAD_DIM
    n_cols = 2 * t
    kern = functools.partial(_diff_prompt_kernel, lam_init=lam_init)
    return pl.pallas_call(
        kern,
        grid=(batch, nq),
        in_specs=[
            pl.BlockSpec(memory_space=pltpu.SMEM),
            pl.BlockSpec((4, DA_QK), lambda b, i: (0, 0)),
            pl.BlockSpec((t, width), lambda b, i: (b * nq + i, QA_BLK // DA_HEADS)),
            pl.BlockSpec((None, DA_HEADS, seq, 2 * HEAD_DIM), lambda b, i: (b, 0, 0, 0)),
            pl.BlockSpec((None, DA_HEADS, nq, HEAD_DIM, t), lambda b, i: (b, 0, 0, 0, 0)),
            pl.BlockSpec((t, n_cols), lambda b, i: (0, 0)),
            pl.BlockSpec((1, HEAD_DIM), lambda b, i: (0, 0)),
        ],
        out_specs=pl.BlockSpec((t, width), lambda b, i: (b * nq + i, 0)),
        out_shape=jax.ShapeDtypeStruct((batch * seq, width), BF16),
        scratch_shapes=[
            pltpu.VMEM((DA_HEADS, 2 * HEAD_DIM, n_cols), BF16),
            pltpu.VMEM((DA_HEADS, 1, n_cols), F32), pltpu.VMEM((DA_HEADS, 1, n_cols), F32),
            pltpu.VMEM((DA_HEADS, HEAD_DIM, n_cols), F32),
        ],
        compiler_params=_cparams(2),
        name="diff_prompt",
    )(slopes, dl, z, ka, vt, _causal_mask_bias(2), nw)


def _top3_mask_rows(gate_t, n_valid, own):
    nb = gate_t.shape[0]
    row = lax.broadcasted_iota(jnp.int32, gate_t.shape, 0)
    g = jnp.where(row < n_valid, gate_t, -jnp.inf)
    sel = row == own
    for _ in range(MOBA_TOPK):
        mx = jnp.max(g, axis=0, keepdims=True)
        idx = jnp.min(jnp.where(g == mx, row, nb), axis=0, keepdims=True)
        pick = (row == idx) & (mx > -jnp.inf)
        sel = sel | pick
        g = jnp.where(pick, -jnp.inf, g)
    return jnp.where(sel, 0.0, MASK_BIAS)


def _moba_prompt_kernel(slopes_ref, q_ref, ka_ref, vt_ref, km_ref, mask_ref, o_ref,
                        qa_sc, m_sc, l_sc, acc_sc, *, nb):
    blk = MOBA_BLOCK
    own = pl.program_id(1)
    q0 = own * blk
    for h in range(MB_HEADS):
        qt = q_ref[:, h * HEAD_DIM:(h + 1) * HEAD_DIM].T
        gate_t = jnp.dot(km_ref[h], qt, preferred_element_type=F32, precision=lax.Precision.HIGHEST)
        q_feat = jnp.concatenate([_query_feature_rows(slopes_ref[DA_HEADS + h], q0, blk),
                                  _top3_mask_rows(gate_t, own, own),
                                  jnp.zeros((LANES - ONEHOT_LANE0 - nb, blk), F32)], axis=0)
        qa_sc[h] = jnp.concatenate([qt * (HEAD_DIM ** -0.5), q_feat], axis=0).astype(BF16)

    _softmax_t_run(MB_HEADS, own, ka_ref, vt_ref, mask_ref, qa_sc, m_sc, l_sc, acc_sc)

    for h in range(MB_HEADS):
        o_ref[:, h * HEAD_DIM:(h + 1) * HEAD_DIM] = (acc_sc[h] / l_sc[h]).T.astype(o_ref.dtype)


def _moba_prompt(z, ka, vt, kmean, slopes, *, batch, seq):
    blk = MOBA_BLOCK
    assert blk == ATT_TILE
    nb = seq // blk
    assert nb <= LANES - ONEHOT_LANE0 and nb % 8 == 0
    width = MB_HEADS * HEAD_DIM
    kern = functools.partial(_moba_prompt_kernel, nb=nb)
    return pl.pallas_call(
        kern,
        grid=(batch, nb),
        in_specs=[
            pl.BlockSpec(memory_space=pltpu.SMEM),
            pl.BlockSpec((blk, width), lambda b, i: (b * nb + i, QB_BLK // MB_HEADS)),
            pl.BlockSpec((None, MB_HEADS, seq, 2 * HEAD_DIM), lambda b, i: (b, 1, 0, 0)),
            pl.BlockSpec((None, MB_HEADS, nb, HEAD_DIM, blk), lambda b, i: (b, 1, 0, 0, 0)),
            pl.BlockSpec((None, MB_HEADS, nb, HEAD_DIM), lambda b, i: (b, 1, 0, 0)),
            pl.BlockSpec((blk, blk), lambda b, i: (0, 0)),
        ],
        out_specs=pl.BlockSpec((blk, width), lambda b, i: (b * nb + i, 0)),
        out_shape=jax.ShapeDtypeStruct((batch * seq, width), BF16),
        scratch_shapes=[
            pltpu.VMEM((MB_HEADS, 2 * HEAD_DIM, blk), BF16),
            pltpu.VMEM((MB_HEADS, 1, blk), F32), pltpu.VMEM((MB_HEADS, 1, blk), F32),
            pltpu.VMEM((MB_HEADS, HEAD_DIM, blk), F32),
        ],
        compiler_params=_cparams(2),
        name="moba_prompt",
    )(slopes, z, ka, vt, kmean, _causal_mask_bias(1))


def _softplus(x):
    return jnp.maximum(x, 0.0) + jnp.log1p(jnp.exp(-jnp.abs(x)))


def _silu(x):
    return x * jax.nn.sigmoid(x)


def _conv_silu(xp_sc, w, b, t):
    y = b
    for j in range(CONV_W):
        y = y + xp_sc[5 + j:5 + j + t, :] * w[j:j + 1, :]
    return _silu(y)


def _ssd_prompt_kernel(zc_ref, x_ref, bc_ref, dt_ref, cw_ref, cb_ref, dtb_ref, alog_ref, dsk_ref, nw_ref,
                       y_ref, hfin_ref, xpx_sc, xpbc_sc, h_sc, y_sc, *, t):
    c = pl.program_id(1)
    nc = pl.num_programs(1)
    hp = LANES // SSM_HEADDIM
    hpg = SSM_HEADS // SSM_GROUPS

    @pl.when(c == 0)
    def _():
        xpx_sc[0:8, :] = jnp.zeros((8, D_SSM), F32)
        xpbc_sc[0:8, :] = jnp.zeros((8, D_XBC - D_SSM), F32)
        h_sc[...] = jnp.zeros(h_sc.shape, F32)

    xpx_sc[8:8 + t, :] = x_ref[...]
    xpbc_sc[8:8 + t, :] = bc_ref[...]
    cw = cw_ref[...]
    cb = cb_ref[...]
    xs = _conv_silu(xpx_sc, cw[:, :D_SSM], cb[:, :D_SSM], t)
    bc = _conv_silu(xpbc_sc, cw[:, D_SSM:], cb[:, D_SSM:], t)
    xpx_sc[5:8, :] = xpx_sc[t + 5:t + 8, :]
    xpbc_sc[5:8, :] = xpbc_sc[t + 5:t + 8, :]

    dt = _softplus(dt_ref[...] + dtb_ref[...])
    da = dt * (-jnp.exp(alog_ref[...]))
    r_i = lax.broadcasted_iota(jnp.int32, (t, t), 0)
    c_i = lax.broadcasted_iota(jnp.int32, (t, t), 1)
    tril = c_i <= r_i
    acs = jnp.dot(tril.astype(F32), da, preferred_element_type=F32, precision=lax.Precision.HIGHEST)
    acs_t = acs.T
    lane = lax.broadcasted_iota(jnp.int32, (t, LANES), 1)
    sub = lax.broadcasted_iota(jnp.int32, (LANES, LANES), 0)
    lo_lane = lane < SSM_HEADDIM
    lo_sub = sub < SSM_HEADDIM

    for g in range(SSM_GROUPS):
        bg = bc[:, g * D_STATE:(g + 1) * D_STATE]
        cg = bc[:, (SSM_GROUPS + g) * D_STATE:(SSM_GROUPS + g + 1) * D_STATE]
        bgb = bg.astype(BF16)
        cgb = cg.astype(BF16)
        gmat = lax.dot_general(cgb, bgb, NT_DIMS, preferred_element_type=F32)
        for pr in range(hpg // hp):
            ha = g * hpg + pr * hp
            cb0 = ha * SSM_HEADDIM
            x2 = xs[:, cb0:cb0 + LANES]
            dt2 = jnp.where(lo_lane, dt[:, ha:ha + 1], dt[:, ha + 1:ha + 2])
            xdt = x2 * dt2
            ydiag = jnp.zeros((t, LANES), F32)
            for u in range(hp):
                hh = ha + u
                seg = acs[:, hh:hh + 1] - acs_t[hh:hh + 1, :]
                lmat = jnp.exp(jnp.where(tril, seg, -jnp.inf))
                xu = jnp.where(lo_lane if u == 0 else ~lo_lane, xdt, 0.0)
                ydiag = ydiag + jnp.dot((gmat * lmat).astype(BF16), xu.astype(BF16), preferred_element_type=F32)
            h2 = h_sc[ha:ha + hp].reshape(LANES, D_STATE)
            yoff = lax.dot_general(cgb, h2.astype(BF16), NT_DIMS, preferred_element_type=F32)
            e2 = jnp.where(lo_lane, jnp.exp(acs[:, ha:ha + 1]), jnp.exp(acs[:, ha + 1:ha + 2]))
            y = ydiag + e2 * yoff + dsk_ref[:, cb0:cb0 + LANES] * x2
            y_sc[:, cb0:cb0 + LANES] = y * _silu(zc_ref[:, cb0:cb0 + LANES])
            last = acs[t - 1:t, :]
            te2 = jnp.where(lo_lane, jnp.exp(last[:, ha:ha + 1] - acs[:, ha:ha + 1]),
                            jnp.exp(last[:, ha + 1:ha + 2] - acs[:, ha + 1:ha + 2]))
            upd = lax.dot_general((xdt * te2).astype(BF16), bgb, TN_DIMS, preferred_element_type=F32)
            dec = jnp.where(lo_sub, jnp.exp(last[:, ha:ha + 1]), jnp.exp(last[:, ha + 1:ha + 2]))
            h_sc[ha:ha + hp] = (dec * h2 + upd).reshape(hp, SSM_HEADDIM, D_STATE)

    gw = D_SSM // SSM_GROUPS
    for g in range(SSM_GROUPS):
        yg = y_sc[:, g * gw:(g + 1) * gw]
        var = jnp.mean(yg * yg, axis=-1, keepdims=True)
        y_ref[:, g * gw:(g + 1) * gw] = (yg * lax.rsqrt(var + RMS_EPS) * nw_ref[:, g * gw:(g + 1) * gw]).astype(y_ref.dtype)

    @pl.when(c == nc - 1)
    def _():
        hfin_ref[0] = h_sc[...]


def _ssd_prompt(z, cw, cb, dtb, alog, dsk, nw, *, batch, seq, t=128):
    nc = seq // t
    kern = functools.partial(_ssd_prompt_kernel, t=t)
    full = lambda shape: pl.BlockSpec(shape, lambda b, c: (0,) * len(shape))
    return pl.pallas_call(
        kern,
        grid=(batch, nc),
        in_specs=[
            pl.BlockSpec((t, D_SSM), lambda b, c: (b * nc + c, ZC_COL // D_SSM)),
            pl.BlockSpec((t, D_SSM), lambda b, c: (b * nc + c, XS_COL // D_SSM)),
            pl.BlockSpec((t, 512), lambda b, c: (b * nc + c, BC_COL // 512)),
            pl.BlockSpec((t, LANES), lambda b, c: (b * nc + c, DT_COL // LANES)),
            full((CONV_W, D_XBC)), full((1, D_XBC)), full((1, LANES)), full((1, LANES)),
            full((1, D_SSM)), full((1, D_SSM)),
        ],
        out_specs=[
            pl.BlockSpec((t, D_SSM), lambda b, c: (b * nc + c, 0)),
            pl.BlockSpec((1, SSM_HEADS, SSM_HEADDIM, D_STATE), lambda b, c: (b, 0, 0, 0)),
        ],
        out_shape=[
            jax.ShapeDtypeStruct((batch * seq, D_SSM), BF16),
            jax.ShapeDtypeStruct((batch, SSM_HEADS, SSM_HEADDIM, D_STATE), F32),
        ],
        scratch_shapes=[
            pltpu.VMEM((t + 8, D_SSM), F32), pltpu.VMEM((t + 8, D_XBC - D_SSM), F32),
            pltpu.VMEM((SSM_HEADS, SSM_HEADDIM, D_STATE), F32), pltpu.VMEM((t, D_SSM), F32),
        ],
        compiler_params=_cparams(2),
        name="ssd_prompt",
    )(z, z, z, z, cw, cb, dtb, alog, dsk, nw)


NEW_ROWS = 16


SEQS_PER_STEP = 2


def _attn_decode_kernel(*refs, moba, n_pages, page, t_new, lam_init, spb):
    pt_ref = refs[0]
    del pt_ref
    slopes_ref, dl_ref, nw_ref, q_ref, kn_ref, vn_ref = refs[1:7]
    n_pg = spb * n_pages
    kp_refs = refs[7:7 + n_pg]
    vp_refs = refs[7 + n_pg:7 + 2 * n_pg]
    o_ref = refs[7 + 2 * n_pg]
    kb_sc, vb_sc, new_sc = refs[8 + 2 * n_pg:]
    past = n_pages * page
    width = DA_HEADS * HEAD_DIM
    cols_per_head = t_new if moba else 2 * t_new
    lanes_per_col_grp = HEAD_DIM if moba else DA_QK
    scale = HEAD_DIM ** -0.5 if moba else DA_QK ** -0.5
    seqs = range(spb)
    coli = lax.broadcasted_iota(jnp.int32, (1, LANES), 1)
    slope = slopes_ref[...]

    ksums = []
    for s in seqs:
        ksum = []
        for p in range(n_pages):
            ksum_p = []
            kp, vp = kp_refs[s * n_pages + p], vp_refs[s * n_pages + p]
            for hd in range(DA_HEADS):
                lo = hd * HEAD_DIM
                kh = kp[pl.ds(hd, page, stride=DA_HEADS), :]
                kb_sc[s, p * page:(p + 1) * page, lo:lo + HEAD_DIM] = kh.astype(BF16)
                vb_sc[s, p * page:(p + 1) * page, lo:lo + HEAD_DIM] = vp[pl.ds(hd, page, stride=DA_HEADS), :].astype(BF16)
                if moba:
                    ksum_p.append(jnp.sum(kh, axis=0, keepdims=True))
            if moba:
                ksum.append(jnp.concatenate(ksum_p, axis=1))
        ksums.append(ksum)
        for j, (src, dst) in enumerate(((kn_ref, kb_sc), (vn_ref, vb_sc))):
            new_sc[s, j] = jnp.zeros(new_sc.shape[2:], F32)
            new_sc[s, j, 0:t_new, :] = src[s]
            dst[s, past:past + NEW_ROWS, :] = new_sc[s, j].astype(BF16)

    qs, scores = [], []
    for s in seqs:
        q = q_ref[s]
        qt = jnp.concatenate([q] * (LANES // t_new), axis=0)
        r_i = lax.broadcasted_iota(jnp.int32, qt.shape, 0)
        l_i = lax.broadcasted_iota(jnp.int32, qt.shape, 1)
        qmat = jnp.where(l_i // lanes_per_col_grp == r_i // t_new, qt, 0.0)
        qs.append(q)
        scores.append(lax.dot_general(kb_sc[s], (qmat * scale).astype(BF16), NT_DIMS,
                                      preferred_element_type=F32))

    probs = []
    for s in seqs:
        rowp = lax.broadcasted_iota(jnp.int32, (past, LANES), 0)
        s_past = scores[s][:past] + slope * (rowp - past).astype(F32)
        rown = lax.broadcasted_iota(jnp.int32, (NEW_ROWS, LANES), 0)
        s_new = scores[s][past:] + slope * rown.astype(F32)
        s_new = jnp.where(rown <= coli % t_new, s_new, -jnp.inf)

        if moba:
            nbp = past // MOBA_BLOCK
            ppb = MOBA_BLOCK // page
            gate = jnp.zeros((nbp, LANES), F32)
            kmean = jnp.concatenate(
                [sum(ksums[s][n * ppb:(n + 1) * ppb]) for n in range(nbp)], axis=0) * (1.0 / MOBA_BLOCK)
            for i in range(MB_HEADS * t_new):
                hd, tok = i // t_new, i % t_new
                lo = hd * HEAD_DIM
                g_i = jnp.sum(kmean[:, lo:lo + HEAD_DIM] * qs[s][tok:tok + 1, lo:lo + HEAD_DIM], axis=-1,
                              keepdims=True)
                gate = jnp.where(coli == i, g_i, gate)
            rowb = lax.broadcasted_iota(jnp.int32, gate.shape, 0)
            sel = jnp.zeros(gate.shape, jnp.bool_)
            g = gate
            for _ in range(MOBA_TOPK):
                mx = jnp.max(g, axis=0, keepdims=True)
                idx = jnp.min(jnp.where(g == mx, rowb, nbp), axis=0, keepdims=True)
                pick = (rowb == idx) & (mx > -jnp.inf)
                sel = sel | pick
                g = jnp.where(pick, -jnp.inf, g)
            sel_bias = jnp.where(sel, 0.0, -jnp.inf)
            s_past = (s_past.reshape(nbp, MOBA_BLOCK, LANES) + sel_bias[:, None, :]).reshape(past, LANES)

        m = jnp.maximum(jnp.max(s_past, axis=0, keepdims=True), jnp.max(s_new, axis=0, keepdims=True))
        p_past = jnp.exp(s_past - m)
        p_new = jnp.exp(s_new - m)
        l = jnp.sum(p_past, axis=0, keepdims=True) + jnp.sum(p_new, axis=0, keepdims=True)
        probs.append((jnp.concatenate([p_past, p_new], axis=0).astype(BF16), l))

    outs_t = [lax.dot_general(vb_sc[s], probs[s][0], TN_DIMS, preferred_element_type=F32) for s in seqs]

    for s in seqs:
        o = outs_t[s].T
        l_col = jnp.broadcast_to(probs[s][1], (LANES, LANES)).T
        o = o / jnp.concatenate([l_col] * (width // LANES), axis=1)
        outs = []
        for hd in range(DA_HEADS):
            lo = hd * HEAD_DIM
            r0 = hd * cols_per_head
            if moba:
                outs.append(o[r0:r0 + t_new, lo:lo + HEAD_DIM])
            else:
                dl = dl_ref[...]
                lam = (jnp.exp(jnp.sum(dl[0:1] * dl[1:2], axis=-1, keepdims=True))
                       - jnp.exp(jnp.sum(dl[2:3] * dl[3:4], axis=-1, keepdims=True)) + lam_init)
                oh = o[r0:r0 + t_new, lo:lo + HEAD_DIM] - lam * o[r0 + t_new:r0 + 2 * t_new, lo:lo + HEAD_DIM]
                var = jnp.mean(oh * oh, axis=-1, keepdims=True)
                outs.append(oh * lax.rsqrt(var + RMS_EPS) * nw_ref[...] * (1.0 - lam_init))
        o_ref[s] = jnp.concatenate(outs, axis=1).astype(o_ref.dtype)


def _attn_decode(zs, cache_k, cache_v, page_table, slopes_cols, dl, nw, *, layer, moba, lam_init, spb=SEQS_PER_STEP):
    n_seq, t_new, _ = zs.shape
    page = cache_k.shape[2] // DA_HEADS
    n_pages = page_table.shape[1]
    width = DA_HEADS * HEAD_DIM
    qblk, kblk, vblk = (QB_BLK, KB_BLK, VB_BLK) if moba else (QA_BLK, KA_BLK, VA_BLK)
    past = n_pages * page
    spb = spb if n_seq % spb == 0 else 1

    def page_spec(s, p):
        return pl.BlockSpec((None, None, page * DA_HEADS, HEAD_DIM), lambda b, pt: (layer, pt[b * spb + s, p], 0, 0))

    def z_spec(blk):
        return pl.BlockSpec((spb, t_new, width), lambda b, pt: (b, 0, blk // DA_HEADS))

    kern = functools.partial(_attn_decode_kernel, moba=moba, n_pages=n_pages, page=page, t_new=t_new,
                             lam_init=lam_init, spb=spb)
    page_specs = [page_spec(s, p) for s in range(spb) for p in range(n_pages)]
    grid_spec = pltpu.PrefetchScalarGridSpec(
        num_scalar_prefetch=1,
        grid=(n_seq // spb,),
        in_specs=[
            pl.BlockSpec((1, LANES), lambda b, pt: (0, 0)),
            pl.BlockSpec((4, DA_QK), lambda b, pt: (0, 0)),
            pl.BlockSpec((1, HEAD_DIM), lambda b, pt: (0, 0)),
            z_spec(qblk), z_spec(kblk), z_spec(vblk),
        ] + page_specs * 2,
        out_specs=pl.BlockSpec((spb, t_new, width), lambda b, pt: (b, 0, 0)),
        scratch_shapes=[
            pltpu.VMEM((spb, past + NEW_ROWS, width), BF16), pltpu.VMEM((spb, past + NEW_ROWS, width), BF16),
            pltpu.VMEM((spb, 2, NEW_ROWS, width), F32),
        ],
    )
    return pl.pallas_call(
        kern,
        grid_spec=grid_spec,
        out_shape=jax.ShapeDtypeStruct((n_seq, t_new, width), BF16),
        compiler_params=_cparams(1),
        name="moba_decode" if moba else "diff_decode",
    )(page_table, slopes_cols, dl, nw, zs, zs, zs, *([cache_k] * len(page_specs)), *([cache_v] * len(page_specs)))


def _ssd_decode_kernel(zc_ref, xs_ref, bc_ref, dtx_ref, cst_ref, h0_ref, cw_ref, cb_ref, dtb_ref, alog_ref,
                       dsk_ref, nw_ref, y_ref, hout_ref, cout_ref, xp_sc, *, t_new):
    hpg = SSM_HEADS // SSM_GROUPS
    gw = D_SSM // SSM_GROUPS
    xp_sc[0:CONV_W - 1, :] = cst_ref[...]
    xp_sc[CONV_W - 1:CONV_W - 1 + t_new, :D_SSM] = xs_ref[...]
    xp_sc[CONV_W - 1:CONV_W - 1 + t_new, D_SSM:] = bc_ref[...]
    cout_ref[...] = xp_sc[t_new:t_new + CONV_W - 1, :]
    y = cb_ref[...]
    for j in range(CONV_W):
        y = y + xp_sc[j:j + t_new, :] * cw_ref[j:j + 1, :]
    xbc = _silu(y)
    xs = xbc[:, :D_SSM]
    dt = _softplus(dtx_ref[...] + dtb_ref[...])
    da = dt * (-jnp.exp(alog_ref[...]))
    acs = [da[0:1]]
    for s in range(1, t_new):
        acs.append(acs[-1] + da[s:s + 1])
    xdt = xs * dt
    last = acs[-1]
    wx = jnp.concatenate([xdt[s:s + 1] * jnp.exp(last - acs[s]) for s in range(t_new)], axis=0)
    dec = jnp.exp(last)
    d_hi = dec.astype(BF16).astype(F32)
    d_md = (dec - d_hi).astype(BF16).astype(F32)
    d_lo = (dec - d_hi - d_md).astype(BF16).astype(F32)
    dec3 = jnp.concatenate([d_hi, d_md, d_lo, jnp.zeros((5, D_SSM), F32)], axis=0).astype(BF16)
    ones = jnp.ones((8, D_STATE), BF16)
    zc = zc_ref[...]
    for g in range(SSM_GROUPS):
        sl = slice(g * gw, (g + 1) * gw)
        bg = xbc[:, D_SSM + g * D_STATE:D_SSM + (g + 1) * D_STATE]
        cg = xbc[:, D_SSM + (SSM_GROUPS + g) * D_STATE:D_SSM + (SSM_GROUPS + g + 1) * D_STATE]
        h0 = h0_ref[g * hpg:(g + 1) * hpg].reshape(gw, D_STATE)
        yoff = lax.dot_general(cg.astype(BF16), h0.astype(BF16), NT_DIMS, preferred_element_type=F32)
        rows = []
        for t in range(t_new):
            yt = jnp.exp(acs[t][:, sl]) * yoff[t:t + 1]
            for s in range(t + 1):
                gts = jnp.sum(cg[t:t + 1] * bg[s:s + 1], axis=-1, keepdims=True)
                yt = yt + gts * jnp.exp(acs[t][:, sl] - acs[s][:, sl]) * xdt[s:s + 1, sl]
            rows.append(yt)
        yg = jnp.concatenate(rows, axis=0) + dsk_ref[:, sl] * xs[:, sl]
        yg = yg * _silu(zc[:, sl])
        var = jnp.mean(yg * yg, axis=-1, keepdims=True)
        y_ref[:, sl] = (yg * lax.rsqrt(var + RMS_EPS) * nw_ref[:, sl]).astype(y_ref.dtype)
        wxg = wx[:, sl]
        wx_hi = wxg.astype(BF16).astype(F32)
        bg_hi = bg.astype(BF16).astype(F32)
        pad_rows = 16 - 3 * t_new
        lhs = jnp.concatenate([wx_hi, wxg - wx_hi, wx_hi, jnp.zeros((pad_rows, gw), F32)], axis=0).astype(BF16)
        rhs = jnp.concatenate([bg_hi, bg_hi, bg - bg_hi, jnp.zeros((pad_rows, D_STATE), F32)], axis=0).astype(BF16)
        upd = lax.dot_general(lhs, rhs, TN_DIMS, preferred_element_type=F32)
        dec_full = lax.dot_general(dec3[:, sl], ones, TN_DIMS, preferred_element_type=F32)
        hout_ref[g * hpg:(g + 1) * hpg] = (dec_full * h0 + upd).reshape(hpg, SSM_HEADDIM, D_STATE)


def _ssd_decode(zs, dtx, state_conv, state_ssm, cw, cb, dtbx, alogx, dskx, nw, *, layer):
    n_seq, t_new, _ = zs.shape
    kern = functools.partial(_ssd_decode_kernel, t_new=t_new)
    full = lambda shape: pl.BlockSpec(shape, lambda b: (0,) * len(shape))
    return pl.pallas_call(
        kern,
        grid=(n_seq,),
        in_specs=[
            pl.BlockSpec((None, t_new, D_SSM), lambda b: (b, 0, ZC_COL // D_SSM)),
            pl.BlockSpec((None, t_new, D_SSM), lambda b: (b, 0, XS_COL // D_SSM)),
            pl.BlockSpec((None, t_new, 512), lambda b: (b, 0, BC_COL // 512)),
            pl.BlockSpec((None, t_new, D_SSM), lambda b: (b, 0, 0)),
            pl.BlockSpec((None, None, CONV_W - 1, D_XBC), lambda b: (layer, b, 0, 0)),
            pl.BlockSpec((None, None, SSM_HEADS, SSM_HEADDIM, D_STATE), lambda b: (layer, b, 0, 0, 0)),
            full((CONV_W, D_XBC)), full((1, D_XBC)), full((1, D_SSM)), full((1, D_SSM)),
            full((1, D_SSM)), full((1, D_SSM)),
        ],
        out_specs=[
            pl.BlockSpec((None, t_new, D_SSM), lambda b: (b, 0, 0)),
            pl.BlockSpec((None, SSM_HEADS, SSM_HEADDIM, D_STATE), lambda b: (b, 0, 0, 0)),
            pl.BlockSpec((None, CONV_W - 1, D_XBC), lambda b: (b, 0, 0)),
        ],
        out_shape=[
            jax.ShapeDtypeStruct((n_seq, t_new, D_SSM), BF16),
            jax.ShapeDtypeStruct((n_seq, SSM_HEADS, SSM_HEADDIM, D_STATE), F32),
            jax.ShapeDtypeStruct((n_seq, CONV_W - 1, D_XBC), F32),
        ],
        scratch_shapes=[pltpu.VMEM((8, D_XBC), F32)],
        compiler_params=_cparams(1),
        name="ssd_decode",
    )(zs, zs, zs, dtx, state_conv, state_ssm, cw, cb, dtbx, alogx, dskx, nw)


def _layer_norm(x, w, b):
    mu = jnp.mean(x, axis=-1, keepdims=True)
    xc = x - mu
    var = jnp.mean(xc * xc, axis=-1, keepdims=True)
    return xc * lax.rsqrt(var + LN_EPS) * w + b


def _top2_of(vals, col, n):
    m1 = jnp.max(vals, axis=-1, keepdims=True)
    i1 = jnp.min(jnp.where(vals == m1, col, n), axis=-1, keepdims=True)
    rest = jnp.where(col == i1, -jnp.inf, vals)
    m2 = jnp.max(rest, axis=-1, keepdims=True)
    i2 = jnp.min(jnp.where(rest == m2, col, n), axis=-1, keepdims=True)
    return m1, i1, m2, i2


def _router_gates(logits, b_router):
    scores = jax.nn.sigmoid(logits)
    biased = scores + b_router
    col = lax.broadcasted_iota(jnp.int32, logits.shape, 1)
    grp = col // EXPERTS_PER_GROUP
    best = jnp.zeros((logits.shape[0], 1), jnp.int32)
    best_v = None
    for g in range(N_EXPERT_GROUPS):
        m1, _, m2, _ = _top2_of(jnp.where(grp == g, biased, -jnp.inf), col, N_EXPERTS)
        gs = m1 + m2
        if g == 0:
            best_v = gs
        else:
            upd = gs > best_v
            best = jnp.where(upd, g, best)
            best_v = jnp.where(upd, gs, best_v)
    _, i1, _, i2 = _top2_of(jnp.where(grp == best, biased, -jnp.inf), col, N_EXPERTS)
    w = jnp.where((col == i1) | (col == i2), scores, 0.0)
    return w / jnp.sum(w, axis=-1, keepdims=True)


ROW_CHUNKS = 16


def _store_token_rows(ref, val):
    rows = val.shape[0]
    for c in range(ROW_CHUNKS):
        ref[pl.ds(c, rows, stride=ROW_CHUNKS), :] = val[:, c * LANES:(c + 1) * LANES]


def _load_token_rows(ref, rows):
    return jnp.concatenate([ref[pl.ds(c, rows, stride=ROW_CHUNKS), :] for c in range(ROW_CHUNKS)], axis=1)


def _post1_kernel(x_ref, oa_ref, ob_ref, oc_ref, wa_ref, wb_ref, wc_ref, lw_ref, lb_ref, wr_ref, br_ref,
                  x1r_ref, x1b_ref, g_ref, *, alpha):
    tm = x_ref.shape[0]
    sub = 128
    mixes = []
    for r in range(tm // sub):
        rows = slice(r * sub, (r + 1) * sub)
        mixes.append(jnp.dot(oa_ref[rows, :], wa_ref[...], preferred_element_type=F32)
                     + jnp.dot(ob_ref[rows, :], wb_ref[...], preferred_element_type=F32)
                     + jnp.dot(oc_ref[rows, :], wc_ref[...], preferred_element_type=F32))
    for r in range(tm // sub):
        rows = slice(r * sub, (r + 1) * sub)
        x1 = _layer_norm(alpha * x_ref[rows, :] + mixes[r], lw_ref[...], lb_ref[...])
        _store_token_rows(x1r_ref.at[pl.ds(r * sub * ROW_CHUNKS, sub * ROW_CHUNKS), :], x1)
        x1b_ref[rows, :] = x1.astype(BF16)
        x1_hi = x1.astype(BF16)
        x1_lo = (x1 - x1_hi.astype(F32)).astype(BF16)
        logits = jnp.dot(jnp.concatenate([x1_hi, x1_lo, x1_hi], axis=1), wr_ref[...], preferred_element_type=F32)
        g_ref[rows, :] = _router_gates(logits, br_ref[...])


def _router_weight_passes(w_router):
    wr_hi = w_router.astype(BF16)
    wr_lo = (w_router - wr_hi.astype(F32)).astype(BF16)
    return jnp.concatenate([wr_hi, wr_hi, wr_lo], axis=0)


def _post1(x, oa, ob, oc, w_out_b, lw, lb, w_router3, b_router, *, layer, alpha, tm=512):
    n, d = x.shape
    assert d == ROW_CHUNKS * LANES
    full = lambda shape: pl.BlockSpec(shape, lambda i: (0,) * len(shape))
    row = lambda width: pl.BlockSpec((tm, width), lambda i: (i, 0))
    per_layer = lambda shape: pl.BlockSpec((None,) + shape, lambda i: (layer,) + (0,) * len(shape))
    wa, wb = DA_HEADS * HEAD_DIM, MB_HEADS * HEAD_DIM
    return pl.pallas_call(
        functools.partial(_post1_kernel, alpha=alpha),
        grid=(n // tm,),
        in_specs=[
            row(d), row(wa), row(wb), row(D_SSM),
            pl.BlockSpec((None, wa, d), lambda i: (layer, 0, 0)), pl.BlockSpec((None, wb, d), lambda i: (layer, 1, 0)),
            pl.BlockSpec((None, D_SSM, d), lambda i: (layer, 1, 0)),
            per_layer((1, d)), per_layer((1, d)), full((3 * d, N_EXPERTS)), full((1, N_EXPERTS)),
        ],
        out_specs=[pl.BlockSpec((tm * ROW_CHUNKS, LANES), lambda i: (i, 0)), row(d), row(N_EXPERTS)],
        out_shape=[jax.ShapeDtypeStruct((n * ROW_CHUNKS, LANES), F32), jax.ShapeDtypeStruct((n, d), BF16),
                   jax.ShapeDtypeStruct((n, N_EXPERTS), F32)],
        compiler_params=_cparams(1),
        name="out_proj_ln1_router",
    )(x, oa, ob, oc, w_out_b, w_out_b, w_out_b, lw, lb, w_router3, b_router)


MOE_TILE = 256


def _moe_routing(gates, tile):
    n = gates.shape[0]
    n_tiles = n // tile + N_EXPERT_GROUPS
    n_slots = n_tiles * tile
    expert_ids = jnp.arange(N_EXPERTS, dtype=jnp.int32)[None, :]
    gid = jnp.min(jnp.where(gates > 0, expert_ids, N_EXPERTS), axis=1) // EXPERTS_PER_GROUP
    onehot = (gid[:, None] == jnp.arange(N_EXPERT_GROUPS, dtype=jnp.int32)[None, :]).astype(jnp.int32)
    rank = jnp.sum((jnp.cumsum(onehot, axis=0) - onehot) * onehot, axis=1)
    counts = jnp.sum(onehot, axis=0)
    padded = (counts + tile - 1) // tile * tile
    seg_end = jnp.cumsum(padded)
    slot = (seg_end - padded)[gid] + rank
    tok = jnp.arange(n, dtype=jnp.int32)
    slot_ids = jnp.arange(n_slots + tile, dtype=jnp.int32)
    slot_src = jnp.zeros((n_slots + tile,), jnp.int32).at[slot].set(tok)
    slot_dst = (n + (slot_ids // tile % 2) * tile + slot_ids % tile).at[slot].set(tok)
    grp_gates = jnp.take_along_axis(gates.reshape(n, N_EXPERT_GROUPS, EXPERTS_PER_GROUP),
                                    gid[:, None, None], axis=1)[:, 0]
    is_real = slot_dst[:n_slots] < n
    slot_gates = jnp.where(is_real[:, None], grp_gates[slot_src[:n_slots]], 0.0)
    tile_start = jnp.arange(n_tiles, dtype=jnp.int32) * tile
    tile_grp = jnp.minimum(jnp.sum((tile_start[:, None] >= seg_end[None, :]).astype(jnp.int32), axis=1),
                           N_EXPERT_GROUPS - 1)
    return tile_grp, slot_src, slot_dst, slot_gates


def _moe_kernel(tile_grp_ref, src_ref, dst_ref, x_hbm, g_ref, wg_ref, wu_ref, wd_ref, ff_hbm,
                xg_sc, x_sc, acc_sc, ys_sc, gsem, ssem, *, tile, n_tok):
    del tile_grp_ref
    i = pl.program_id(0)
    e = pl.program_id(1)
    n_tiles = pl.num_programs(0)
    n_e = EXPERTS_PER_GROUP
    slab = ROW_CHUNKS

    def gather_copy(slot, row, buf):
        tok = src_ref[slot]
        return pltpu.make_async_copy(x_hbm.at[pl.ds(tok * slab, slab), :],
                                     xg_sc.at[buf, pl.ds(row * slab, slab), :], gsem.at[buf])

    def scatter_copy(slot, row, buf):
        tok = dst_ref[slot]
        return pltpu.make_async_copy(ys_sc.at[buf, pl.ds(row * slab, slab), :],
                                     ff_hbm.at[pl.ds(tok * slab, slab), :], ssem.at[buf])

    def wait_all(sc, sem, buf):
        pltpu.make_async_copy(sc.at[buf], sc.at[buf], sem.at[buf]).wait()

    cur = i % 2
    nxt = (i + 1) % 2

    @pl.when((i == 0) & (e == 0))
    def _():
        for j in range(tile):
            gather_copy(j, j, 0).start()
        ys_sc[...] = jnp.zeros(ys_sc.shape, F32)
        for b in range(2):
            fill = pltpu.make_async_copy(
                ys_sc.at[b], ff_hbm.at[pl.ds((n_tok + b * tile) * slab, tile * slab), :], ssem.at[b])
            fill.start()
            fill.wait()

    part_rows = tile // n_e
    for j in range(part_rows):
        row = e * part_rows + j
        gather_copy((i + 1) * tile + row, row, nxt).start()

    @pl.when(e == 0)
    def _():
        wait_all(xg_sc, gsem, cur)
        for c in range(ROW_CHUNKS):
            x_sc[:, c * LANES:(c + 1) * LANES] = xg_sc[cur, pl.ds(c, tile, stride=ROW_CHUNKS), :].astype(BF16)

    x = x_sc[...]
    hid = _silu(jnp.dot(x, wg_ref[...], preferred_element_type=F32)) * jnp.dot(x, wu_ref[...],
                                                                                preferred_element_type=F32)
    col = lax.broadcasted_iota(jnp.int32, g_ref.shape, 1)
    gate = jnp.sum(jnp.where(col == e, g_ref[...], 0.0), axis=-1, keepdims=True)
    part = jnp.dot((hid * gate).astype(BF16), wd_ref[...], preferred_element_type=F32)

    @pl.when(e == 0)
    def _():
        acc_sc[...] = part

    @pl.when(e > 0)
    def _():
        acc_sc[...] += part

    @pl.when(e == n_e - 1)
    def _():
        @pl.when(i >= 2)
        def _():
            wait_all(ys_sc, ssem, cur)

        y = acc_sc[...]
        for c in range(ROW_CHUNKS):
            ys_sc[cur, pl.ds(c, tile, stride=ROW_CHUNKS), :] = y[:, c * LANES:(c + 1) * LANES]
        for j in range(tile):
            scatter_copy(i * tile + j, j, cur).start()

        @pl.when(i == n_tiles - 1)
        def _():
            wait_all(ys_sc, ssem, cur)

            @pl.when(n_tiles >= 2)
            def _():
                wait_all(ys_sc, ssem, nxt)

            wait_all(xg_sc, gsem, nxt)


def _moe(x1r, gates, wg, wu, wd, *, layer=0, tile=MOE_TILE):
    n = gates.shape[0]
    d, f = wg.shape[1], wg.shape[2]
    e0 = layer * N_EXPERTS
    tile_grp, slot_src, slot_dst, slot_gates = _moe_routing(gates, tile)
    n_tiles = tile_grp.shape[0]
    n_rows = n + 2 * tile
    n_e = EXPERTS_PER_GROUP
    grid_spec = pltpu.PrefetchScalarGridSpec(
        num_scalar_prefetch=3,
        grid=(n_tiles, n_e),
        in_specs=[
            pl.BlockSpec(memory_space=pl.ANY),
            pl.BlockSpec((tile, n_e), lambda i, e, tg, s, t: (i, 0)),
            pl.BlockSpec((None, d, f), lambda i, e, tg, s, t: (e0 + tg[i] * n_e + e, 0, 0)),
            pl.BlockSpec((None, d, f), lambda i, e, tg, s, t: (e0 + tg[i] * n_e + e, 0, 0)),
            pl.BlockSpec((None, f, d), lambda i, e, tg, s, t: (e0 + tg[i] * n_e + e, 0, 0)),
        ],
        out_specs=pl.BlockSpec(memory_space=pl.ANY),
        scratch_shapes=[
            pltpu.VMEM((2, tile * ROW_CHUNKS, LANES), F32), pltpu.VMEM((tile, d), BF16),
            pltpu.VMEM((tile, d), F32), pltpu.VMEM((2, tile * ROW_CHUNKS, LANES), F32),
            pltpu.SemaphoreType.DMA((2,)), pltpu.SemaphoreType.DMA((2,)),
        ],
    )
    return pl.pallas_call(
        functools.partial(_moe_kernel, tile=tile, n_tok=n),
        grid_spec=grid_spec,
        out_shape=jax.ShapeDtypeStruct((n_rows * ROW_CHUNKS, LANES), F32),
        compiler_params=_cparams(2),
        name="moe_grouped",
    )(tile_grp, slot_src, slot_dst, x1r, slot_gates, wg, wu, wd)


def _post2_kernel(x1r_ref, x1b_ref, ffr_ref, p_ref, wgate_ref, wproj_ref, lw_ref, lb_ref, out_a_ref, out_b_ref, *,
                  alpha, head_tiles):
    i = pl.program_id(0)
    tm = x1b_ref.shape[0]
    sub = 128
    gates, projs = [], []
    for r in range(tm // sub):
        rows = slice(r * sub, (r + 1) * sub)
        gates.append(jnp.dot(x1b_ref[rows, :], wgate_ref[...], preferred_element_type=F32))
        projs.append(jnp.dot(p_ref[rows, :], wproj_ref[...], preferred_element_type=F32))
    for r in range(tm // sub):
        rows = slice(r * sub, (r + 1) * sub)
        slab = pl.ds(r * sub * ROW_CHUNKS, sub * ROW_CHUNKS)
        ple = jax.nn.sigmoid(gates[r]) * projs[r]
        x2 = _layer_norm(alpha * _load_token_rows(x1r_ref.at[slab, :], sub)
                         + _load_token_rows(ffr_ref.at[slab, :], sub) + ple, lw_ref[...], lb_ref[...])
        if head_tiles is None:
            out_a_ref[rows, :] = x2
            out_b_ref[rows, :] = x2.astype(BF16)
        else:
            @pl.when(i < head_tiles)
            def _():
                out_a_ref[rows, :] = x2

            @pl.when(i >= head_tiles)
            def _():
                out_b_ref[rows, :] = x2


def _post2(x1r, x1b, ffr, p, w_gate_b, w_proj_b, lw, lb, *, layer, alpha, split_at=None, tm=512):
    n, d = x1b.shape
    dp = p.shape[1]
    per_layer = lambda shape: pl.BlockSpec((None,) + shape, lambda i: (layer,) + (0,) * len(shape))
    row = lambda width: pl.BlockSpec((tm, width), lambda i: (i, 0))
    slab = pl.BlockSpec((tm * ROW_CHUNKS, LANES), lambda i: (i, 0))
    if split_at is None:
        head_tiles = None
        out_specs = [row(d), row(d)]
        out_shape = [jax.ShapeDtypeStruct((n, d), F32), jax.ShapeDtypeStruct((n, d), BF16)]
    else:
        assert split_at % tm == 0 and (n - split_at) % tm == 0
        head_tiles = split_at // tm
        out_specs = [pl.BlockSpec((tm, d), lambda i: (jnp.minimum(i, head_tiles - 1), 0)),
                     pl.BlockSpec((tm, d), lambda i: (jnp.maximum(i - head_tiles, 0), 0))]
        out_shape = [jax.ShapeDtypeStruct((split_at, d), F32), jax.ShapeDtypeStruct((n - split_at, d), F32)]
    return pl.pallas_call(
        functools.partial(_post2_kernel, alpha=alpha, head_tiles=head_tiles),
        grid=(n // tm,),
        in_specs=[slab, row(d), slab, row(dp), per_layer((d, d)), per_layer((dp, d)), per_layer((1, d)),
                  per_layer((1, d))],
        out_specs=out_specs,
        out_shape=out_shape,
        compiler_params=_cparams(1),
        name="ple_ln2",
    )(x1r, x1b, ffr, p, w_gate_b, w_proj_b, lw, lb)


def kernel(x_prompt, x_sample, p_prompt, p_sample, cache_diff_k, cache_diff_v, cache_moba_k, cache_moba_v, state_ssm, state_conv, page_table, w_in, w_out, diff_lambda, diff_norm_w, ssm_conv_w, ssm_conv_b, ssm_dt_bias, ssm_a_log, ssm_d, ssm_norm_w, ln1_w, ln1_b, ln2_w, ln2_b, w_router, b_router, w_exp_gate, w_exp_up, w_exp_down, w_ple_proj, w_ple_gate):
    depth = w_in.shape[0]
    batch, seq, d_model = x_prompt.shape
    n_seq, t_new, _ = x_sample.shape
    n_p, n_s = batch * seq, n_seq * t_new
    alpha = (2 * depth) ** 0.25
    n_heads = DA_HEADS + MB_HEADS
    slopes = jnp.asarray(2.0 ** (-8.0 * np.arange(1, n_heads + 1) / n_heads), dtype=F32)

    def score_col_slopes(sl, cols_per_head):
        v = jnp.repeat(sl, cols_per_head)
        return jnp.pad(v, (0, LANES - v.shape[0])).reshape(1, LANES)

    slopes_diff_cols = score_col_slopes(slopes[:DA_HEADS], 2 * t_new)
    slopes_moba_cols = score_col_slopes(slopes[DA_HEADS:], t_new)
    pad_lanes = lambda v: jnp.pad(v, (0, LANES - v.shape[0])).reshape(1, LANES)
    per_channel = lambda v: jnp.repeat(v, SSM_HEADDIM).reshape(1, D_SSM)

    n_pool, page = cache_diff_k.shape[1], cache_diff_k.shape[2]
    flat_cache = lambda c: c.reshape(depth, n_pool, page * DA_HEADS, HEAD_DIM)
    ck_a, cv_a, ck_b, cv_b = map(flat_cache, (cache_diff_k, cache_diff_v, cache_moba_k, cache_moba_v))

    w_in_b = jnp.pad(w_in.astype(BF16), ((0, 0), (0, 0), (0, D_IN_PAD - D_IN_PROJ)))
    w_dtx = jnp.repeat(w_in[:, :, DT_COL:DT_COL + SSM_HEADS].astype(BF16), SSM_HEADDIM, axis=2)
    w_out_b = w_out.astype(BF16)
    w_gate_b, w_proj_b = w_ple_gate.astype(BF16), w_ple_proj.astype(BF16)
    d_exp = w_exp_gate.shape[-1]
    wg_b = w_exp_gate.astype(BF16).reshape(depth * N_EXPERTS, d_model, d_exp)
    wu_b = w_exp_up.astype(BF16).reshape(depth * N_EXPERTS, d_model, d_exp)
    wd_b = w_exp_down.astype(BF16).reshape(depth * N_EXPERTS, d_exp, d_model)
    w_router3 = _router_weight_passes(w_router)
    b_router2 = b_router.reshape(1, N_EXPERTS)
    ln_rows = lambda v: v.reshape(depth, 1, d_model)
    ln1_w3, ln1_b3, ln2_w3, ln2_b3 = map(ln_rows, (ln1_w, ln1_b, ln2_w, ln2_b))

    x = jnp.concatenate([x_prompt.reshape(n_p, d_model), x_sample.reshape(n_s, d_model)], axis=0)
    xb = x.astype(BF16)
    outs_p = [[] for _ in range(6)]
    outs_s = [[] for _ in range(6)]
    for i in range(depth):
        lam_init = 0.8 - 0.6 * math.exp(-0.3 * i)
        z = _matmul(xb, w_in_b, i, tm=512, tn=D_IN_PAD // 3)
        zs = z[n_p:].reshape(n_seq, t_new, D_IN_PAD)
        dtx = _matmul(xb[n_p:], w_dtx, i, tm=n_s, tn=D_SSM).reshape(n_seq, t_new, D_SSM)

        dl = diff_lambda[i]
        nw_a = diff_norm_w[i].reshape(1, HEAD_DIM)
        cw, cb = ssm_conv_w[i], ssm_conv_b[i].reshape(1, D_XBC)
        nw_c = ssm_norm_w[i].reshape(1, D_SSM)

        ka, vt, kmean, *kv_new_p = _kv_prep(z, batch=batch, seq=seq)
        oa_p = _diff_prompt(z, ka, vt, slopes, dl, nw_a, batch=batch, seq=seq, lam_init=lam_init)
        ob_p = _moba_prompt(z, ka, vt, kmean, slopes, batch=batch, seq=seq)
        oc_p, h_p = _ssd_prompt(z, cw, cb, pad_lanes(ssm_dt_bias[i]), pad_lanes(ssm_a_log[i]),
                                per_channel(ssm_d[i]), nw_c, batch=batch, seq=seq)
        oa_s = _attn_decode(zs, ck_a, cv_a, page_table, slopes_diff_cols, dl, nw_a, layer=i, moba=False,
                            lam_init=lam_init)
        ob_s = _attn_decode(zs, ck_b, cv_b, page_table, slopes_moba_cols, dl, nw_a, layer=i, moba=True,
                            lam_init=lam_init)
        oc_s, h_s, conv_s = _ssd_decode(zs, dtx, state_conv, state_ssm, cw, cb, per_channel(ssm_dt_bias[i]),
                                        per_channel(ssm_a_log[i]), per_channel(ssm_d[i]), nw_c, layer=i)

        head_shape = lambda a, lead: a.reshape(*lead, DA_HEADS, HEAD_DIM)
        for j, blk in enumerate((KA_BLK, VA_BLK, KB_BLK, VB_BLK)):
            c0 = blk * LANES
            outs_p[j].append(kv_new_p[j])
            outs_s[j].append(head_shape(zs[:, :, c0:c0 + DA_HEADS * HEAD_DIM], (n_seq, t_new)))
        outs_p[4].append(h_p)
        outs_p[5].append(jnp.stack([z[(b + 1) * seq - (CONV_W - 1):(b + 1) * seq, XS_COL:XS_COL + D_XBC]
                                    for b in range(batch)]))
        outs_s[4].append(h_s)
        outs_s[5].append(conv_s)

        oa = jnp.concatenate([oa_p, oa_s.reshape(n_s, -1)], axis=0)
        ob = jnp.concatenate([ob_p, ob_s.reshape(n_s, -1)], axis=0)
        oc = jnp.concatenate([oc_p, oc_s.reshape(n_s, -1)], axis=0)
        x1r, x1b, gates = _post1(x, oa, ob, oc, w_out_b, ln1_w3, ln1_b3, w_router3, b_router2, layer=i, alpha=alpha)
        ffr = _moe(x1r, gates, wg_b, wu_b, wd_b, layer=i)
        p = jnp.concatenate([p_prompt[i].reshape(n_p, -1), p_sample[i].reshape(n_s, -1)], axis=0).astype(BF16)
        last = i == depth - 1
        x, xb = _post2(x1r, x1b, ffr, p, w_gate_b, w_proj_b, ln2_w3, ln2_b3, layer=i, alpha=alpha,
                       split_at=n_p if last else None, tm=256 if last else 512)

    y_p = x.reshape(batch, seq, d_model)
    y_s = xb.reshape(n_seq, t_new, d_model)
    return (y_p, y_s, *[jnp.stack(o) for o in outs_p], *[jnp.stack(o) for o in outs_s])
```

```python
import functools
import math

import jax
import jax.numpy as jnp
import numpy as np
from jax import lax
from jax.experimental import pallas as pl
from jax.experimental.pallas import tpu as pltpu

F32 = jnp.float32
BF16 = jnp.bfloat16

LANES = 128
HEAD_DIM = 128
DA_HEADS = 4
DA_QK = 64
MB_HEADS = 4
MOBA_BLOCK = 256
MOBA_TOPK = 3
D_SSM = 1024
SSM_HEADS = 16
SSM_HEADDIM = 64
SSM_GROUPS = 2
D_STATE = 128
CONV_W = 4
D_XBC = D_SSM + 2 * SSM_GROUPS * D_STATE
N_EXPERTS = 16
N_EXPERT_GROUPS = 4
EXPERTS_PER_GROUP = 4
LN_EPS = 1e-5
RMS_EPS = 1e-6
VMEM_LIMIT = 56 * 1024 * 1024

QA_BLK, KA_BLK, VA_BLK, QB_BLK, KB_BLK, VB_BLK = 0, 4, 8, 12, 16, 20
ZC_COL, XS_COL, BC_COL, DT_COL = 3072, 4096, 5120, 5632
D_IN_PROJ = 5648
D_IN_PAD = 5760

NT_DIMS = (((1,), (1,)), ((), ()))
TN_DIMS = (((0,), (0,)), ((), ()))


def _cparams(n_axes):
    return pltpu.CompilerParams(dimension_semantics=("arbitrary",) * n_axes, vmem_limit_bytes=VMEM_LIMIT)


def _mm_kernel(x_ref, w_ref, o_ref):
    o_ref[...] = jnp.dot(x_ref[...], w_ref[...], preferred_element_type=F32).astype(o_ref.dtype)


def _matmul(x, w, layer, tm, tn, out_dtype=F32):
    m, k = x.shape
    n = w.shape[2]
    return pl.pallas_call(
        _mm_kernel,
        grid=(n // tn, m // tm),
        in_specs=[pl.BlockSpec((tm, k), lambda j, i: (i, 0)), pl.BlockSpec((None, k, tn), lambda j, i: (layer, 0, j))],
        out_specs=pl.BlockSpec((tm, tn), lambda j, i: (i, j)),
        out_shape=jax.ShapeDtypeStruct((m, n), out_dtype),
        compiler_params=_cparams(2),
        name="in_proj",
    )(x, w)


POS_SPLIT = 64
ONEHOT_LANE0 = 8
LOG2E = math.log2(math.e)
MASK_BIAS = -1e30


ATT_TILE = 256
N_ATT_HEADS = DA_HEADS + MB_HEADS
VT_ROWS = HEAD_DIM + 16
HEADS_PER_PHASE = 4


def _kv_prep_kernel(k_ref, v_ref, ka_ref, vt_ref, km_ref, ko_a_ref, vo_a_ref, ko_b_ref, vo_b_ref, *, ck):
    typ = pl.program_id(1)
    c = pl.program_id(2)
    tiles = ck // ATT_TILE
    pos = c * ck + lax.broadcasted_iota(jnp.int32, (ck, LANES), 0)
    lane = lax.broadcasted_iota(jnp.int32, (ck, LANES), 1)
    kind = lane % 3
    feat = jnp.where(lane < 6, jnp.where(kind == 0, pos // POS_SPLIT, jnp.where(kind == 1, pos % POS_SPLIT, 1)), 0)
    feat = jnp.where(lane - ONEHOT_LANE0 == pos // MOBA_BLOCK, 1, feat).astype(F32).astype(BF16)
    for hd in range(DA_HEADS):
        lanes = slice(hd * HEAD_DIM, (hd + 1) * HEAD_DIM)
        ka_ref[hd, :, :HEAD_DIM] = k_ref[:, lanes].astype(BF16)
        ka_ref[hd, :, HEAD_DIM:] = feat
        for j in range(tiles):
            rows = slice(j * ATT_TILE, (j + 1) * ATT_TILE)
            sum_rows = lax.broadcasted_iota(jnp.int32, (VT_ROWS - HEAD_DIM, ATT_TILE), 0) == 0
            vt_ref[hd, j] = jnp.concatenate([v_ref[rows, lanes].T, sum_rows.astype(F32)], axis=0).astype(BF16)
            km_ref[hd, pl.ds(c * tiles + j, 1), :] = jnp.mean(k_ref[rows, lanes], axis=0, keepdims=True)
    for t, (ko_ref, vo_ref) in enumerate(((ko_a_ref, vo_a_ref), (ko_b_ref, vo_b_ref))):
        @pl.when(typ == t)
        def _():
            for hd in range(DA_HEADS):
                lanes = slice(hd * HEAD_DIM, (hd + 1) * HEAD_DIM)
                ko_ref[:, hd, :] = k_ref[:, lanes]
                vo_ref[:, hd, :] = v_ref[:, lanes]


def _kv_prep(z, *, batch, seq, ck=1024):
    nc = seq // ck
    nq = seq // ATT_TILE
    tiles = ck // ATT_TILE
    width = DA_HEADS * HEAD_DIM
    k_blk = lambda t: (KA_BLK + (KB_BLK - KA_BLK) * t) // DA_HEADS
    v_blk = lambda t: (VA_BLK + (VB_BLK - VA_BLK) * t) // DA_HEADS
    new_a = pl.BlockSpec((None, ck, DA_HEADS, HEAD_DIM), lambda b, t, c: (b, jnp.where(t == 0, c, nc - 1), 0, 0))
    new_b = pl.BlockSpec((None, ck, DA_HEADS, HEAD_DIM), lambda b, t, c: (b, jnp.where(t == 1, c, 0), 0, 0))
    return pl.pallas_call(
        functools.partial(_kv_prep_kernel, ck=ck),
        grid=(batch, 2, nc),
        in_specs=[
            pl.BlockSpec((ck, width), lambda b, t, c: (b * nc + c, k_blk(t))),
            pl.BlockSpec((ck, width), lambda b, t, c: (b * nc + c, v_blk(t))),
        ],
        out_specs=[
            pl.BlockSpec((None, DA_HEADS, ck, 2 * HEAD_DIM), lambda b, t, c: (b, t, c, 0)),
            pl.BlockSpec((None, DA_HEADS, tiles, VT_ROWS, ATT_TILE), lambda b, t, c: (b, t, c, 0, 0)),
            pl.BlockSpec((None, DA_HEADS, nq, HEAD_DIM), lambda b, t, c: (b, t, 0, 0)),
            new_a, new_a, new_b, new_b,
        ],
        out_shape=[
            jax.ShapeDtypeStruct((batch, N_ATT_HEADS, seq, 2 * HEAD_DIM), BF16),
            jax.ShapeDtypeStruct((batch, N_ATT_HEADS, seq // ATT_TILE, VT_ROWS, ATT_TILE), BF16),
            jax.ShapeDtypeStruct((batch, N_ATT_HEADS, seq // ATT_TILE, HEAD_DIM), F32),
        ] + [jax.ShapeDtypeStruct((batch, seq, DA_HEADS, HEAD_DIM), F32)] * 4,
        compiler_params=_cparams(3),
        name="kv_prep",
    )(z, z)


def _query_feature_rows(slope, q0, n_cols):
    row = lax.broadcasted_iota(jnp.int32, (8, n_cols), 0)
    c = slope * LOG2E
    kind = row % 3
    val = jnp.where(kind == 0, c * POS_SPLIT, jnp.where(kind == 1, c, -(c * q0.astype(F32))))
    head = val.astype(BF16).astype(F32)
    return jnp.where(row < 3, head, jnp.where(row < 6, val - head, 0.0))


def _softmax_t_step(n_heads, kj, ka_ref, vt_ref, qa_sc, m_sc, acc_sc, bias=None):
    t = ATT_TILE
    rows = pl.ds(pl.multiple_of(kj * t, t), t)
    for h0 in range(0, n_heads, HEADS_PER_PHASE):
        heads = range(h0, min(h0 + HEADS_PER_PHASE, n_heads))
        scores = {h: jnp.dot(ka_ref[h, rows, :], qa_sc[h], preferred_element_type=F32) for h in heads}
        probs = {}
        for h in heads:
            s = scores[h] if bias is None else scores[h] + bias
            m = m_sc[h]
            m_new = jnp.maximum(m, jnp.max(s, axis=0, keepdims=True))
            alpha = jnp.exp2(m - m_new)
            p = jnp.exp2(s - m_new)
            m_sc[h] = m_new
            probs[h] = (alpha, p.astype(BF16))
        for h in heads:
            alpha, p = probs[h]
            acc_sc[h] = alpha * acc_sc[h] + jnp.dot(vt_ref[h, kj], p, preferred_element_type=F32)


def _softmax_t_run(n_heads, own, ka_ref, vt_ref, mask_ref, qa_sc, m_sc, acc_sc):
    m_sc[...] = jnp.full(m_sc.shape, -jnp.inf, F32)
    acc_sc[...] = jnp.zeros(acc_sc.shape, F32)
    _softmax_t_step(n_heads, own, ka_ref, vt_ref, qa_sc, m_sc, acc_sc, bias=mask_ref[...])

    def body(kj, carry):
        _softmax_t_step(n_heads, kj, ka_ref, vt_ref, qa_sc, m_sc, acc_sc)
        return carry

    lax.fori_loop(0, own, body, 0)


def _causal_mask_bias(n_maps):
    t = ATT_TILE
    k_i = lax.broadcasted_iota(jnp.int32, (t, n_maps * t), 0)
    q_i = lax.broadcasted_iota(jnp.int32, (t, n_maps * t), 1) % t
    return jnp.where(k_i <= q_i, 0.0, -jnp.inf).astype(F32)


def _diff_prompt_kernel(slopes_ref, dl_ref, q_ref, ka_ref, vt_ref, mask_ref, nw_ref, o_ref,
                        qa_sc, m_sc, acc_sc, *, lam_init):
    t = ATT_TILE
    qi = pl.program_id(1)
    q0 = qi * t
    n_cols = 2 * t
    for h in range(DA_HEADS):
        qt = q_ref[:, h * HEAD_DIM:(h + 1) * HEAD_DIM].T * (DA_QK ** -0.5 * LOG2E)
        row = lax.broadcasted_iota(jnp.int32, qt.shape, 0)
        q_top = jnp.concatenate([jnp.where(row < DA_QK, qt, 0.0), jnp.where(row >= DA_QK, qt, 0.0)], axis=1)
        q_feat = jnp.concatenate([_query_feature_rows(slopes_ref[h], q0, n_cols),
                                  jnp.zeros((LANES - 8, n_cols), F32)], axis=0)
        qa_sc[h] = jnp.concatenate([q_top, q_feat], axis=0).astype(BF16)

    _softmax_t_run(DA_HEADS, qi, ka_ref, vt_ref, mask_ref, qa_sc, m_sc, acc_sc)

    dl = dl_ref[...]
    lam = (jnp.exp(jnp.sum(dl[0:1] * dl[1:2], axis=-1, keepdims=True))
           - jnp.exp(jnp.sum(dl[2:3] * dl[3:4], axis=-1, keepdims=True)) + lam_init)
    for h in range(DA_HEADS):
        o2 = acc_sc[h, :HEAD_DIM, :] / acc_sc[h, HEAD_DIM:HEAD_DIM + 1, :]
        ot = o2[:, :t] - lam * o2[:, t:]
        var = jnp.mean(ot * ot, axis=0, keepdims=True)
        o = (ot * lax.rsqrt(var + RMS_EPS)).T * nw_ref[...] * (1.0 - lam_init)
        o_ref[:, h * HEAD_DIM:(h + 1) * HEAD_DIM] = o.astype(o_ref.dtype)


def _diff_prompt(z, ka, vt, slopes, dl, nw, *, batch, seq, lam_init):
    t = ATT_TILE
    nq = seq // t
    width = DA_HEADS * HEAD_DIM
    n_cols = 2 * t
    kern = functools.partial(_diff_prompt_kernel, lam_init=lam_init)
    return pl.pallas_call(
        kern,
        grid=(batch, nq),
        in_specs=[
            pl.BlockSpec(memory_space=pltpu.SMEM),
            pl.BlockSpec((4, DA_QK), lambda b, i: (0, 0)),
            pl.BlockSpec((t, width), lambda b, i: (b * nq + i, QA_BLK // DA_HEADS)),
            pl.BlockSpec((None, DA_HEADS, seq, 2 * HEAD_DIM), lambda b, i: (b, 0, 0, 0)),
            pl.BlockSpec((None, DA_HEADS, nq, VT_ROWS, t), lambda b, i: (b, 0, 0, 0, 0)),
            pl.BlockSpec((t, n_cols), lambda b, i: (0, 0)),
            pl.BlockSpec((1, HEAD_DIM), lambda b, i: (0, 0)),
        ],
        out_specs=pl.BlockSpec((t, width), lambda b, i: (b * nq + i, 0)),
        out_shape=jax.ShapeDtypeStruct((batch * seq, width), BF16),
        scratch_shapes=[
            pltpu.VMEM((DA_HEADS, 2 * HEAD_DIM, n_cols), BF16),
            pltpu.VMEM((DA_HEADS, 1, n_cols), F32), pltpu.VMEM((DA_HEADS, VT_ROWS, n_cols), F32),
        ],
        compiler_params=_cparams(2),
        name="diff_prompt",
    )(slopes, dl, z, ka, vt, _causal_mask_bias(2), nw)


def _top3_mask_rows(gate_t, n_valid, own):
    nb = gate_t.shape[0]
    row = lax.broadcasted_iota(jnp.int32, gate_t.shape, 0)
    g = jnp.where(row < n_valid, gate_t, -jnp.inf)
    sel = row == own
    for _ in range(MOBA_TOPK):
        mx = jnp.max(g, axis=0, keepdims=True)
        idx = jnp.min(jnp.where(g == mx, row, nb), axis=0, keepdims=True)
        pick = (row == idx) & (mx > -jnp.inf)
        sel = sel | pick
        g = jnp.where(pick, -jnp.inf, g)
    return jnp.where(sel, 0.0, MASK_BIAS)


def _moba_prompt_kernel(slopes_ref, q_ref, ka_ref, vt_ref, km_ref, mask_ref, o_ref,
                        qa_sc, m_sc, acc_sc, *, nb):
    blk = MOBA_BLOCK
    own = pl.program_id(1)
    q0 = own * blk
    for h in range(MB_HEADS):
        qt = q_ref[:, h * HEAD_DIM:(h + 1) * HEAD_DIM].T
        gate_t = jnp.dot(km_ref[h], qt, preferred_element_type=F32, precision=lax.Precision.HIGHEST)
        q_feat = jnp.concatenate([_query_feature_rows(slopes_ref[DA_HEADS + h], q0, blk),
                                  _top3_mask_rows(gate_t, own, own),
                                  jnp.zeros((LANES - ONEHOT_LANE0 - nb, blk), F32)], axis=0)
        qa_sc[h] = jnp.concatenate([qt * (HEAD_DIM ** -0.5 * LOG2E), q_feat], axis=0).astype(BF16)

    _softmax_t_run(MB_HEADS, own, ka_ref, vt_ref, mask_ref, qa_sc, m_sc, acc_sc)

    for h in range(MB_HEADS):
        o = acc_sc[h, :HEAD_DIM, :] / acc_sc[h, HEAD_DIM:HEAD_DIM + 1, :]
        o_ref[:, h * HEAD_DIM:(h + 1) * HEAD_DIM] = o.T.astype(o_ref.dtype)


def _moba_prompt(z, ka, vt, kmean, slopes, *, batch, seq):
    blk = MOBA_BLOCK
    assert blk == ATT_TILE
    nb = seq // blk
    assert nb <= LANES - ONEHOT_LANE0 and nb % 8 == 0
    width = MB_HEADS * HEAD_DIM
    kern = functools.partial(_moba_prompt_kernel, nb=nb)
    return pl.pallas_call(
        kern,
        grid=(batch, nb),
        in_specs=[
            pl.BlockSpec(memory_space=pltpu.SMEM),
            pl.BlockSpec((blk, width), lambda b, i: (b * nb + i, QB_BLK // MB_HEADS)),
            pl.BlockSpec((None, MB_HEADS, seq, 2 * HEAD_DIM), lambda b, i: (b, 1, 0, 0)),
            pl.BlockSpec((None, MB_HEADS, nb, VT_ROWS, blk), lambda b, i: (b, 1, 0, 0, 0)),
            pl.BlockSpec((None, MB_HEADS, nb, HEAD_DIM), lambda b, i: (b, 1, 0, 0)),
            pl.BlockSpec((blk, blk), lambda b, i: (0, 0)),
        ],
        out_specs=pl.BlockSpec((blk, width), lambda b, i: (b * nb + i, 0)),
        out_shape=jax.ShapeDtypeStruct((batch * seq, width), BF16),
        scratch_shapes=[
            pltpu.VMEM((MB_HEADS, 2 * HEAD_DIM, blk), BF16),
            pltpu.VMEM((MB_HEADS, 1, blk), F32), pltpu.VMEM((MB_HEADS, VT_ROWS, blk), F32),
        ],
        compiler_params=_cparams(2),
        name="moba_prompt",
    )(slopes, z, ka, vt, kmean, _causal_mask_bias(1))


def _softplus(x):
    return jnp.maximum(x, 0.0) + jnp.log1p(jnp.exp(-jnp.abs(x)))


def _silu(x):
    return x * jax.nn.sigmoid(x)


def _conv_silu(xp_sc, w, b, t):
    y = b
    for j in range(CONV_W):
        y = y + xp_sc[5 + j:5 + j + t, :] * w[j:j + 1, :]
    return _silu(y)


def _ssd_prompt_kernel(zc_ref, x_ref, bc_ref, dt_ref, cw_ref, cb_ref, dtb_ref, alog_ref, dsk_ref, nw_ref,
                       y_ref, hfin_ref, xpx_sc, xpbc_sc, h_sc, y_sc, *, t):
    c = pl.program_id(1)
    nc = pl.num_programs(1)
    hp = LANES // SSM_HEADDIM
    hpg = SSM_HEADS // SSM_GROUPS

    @pl.when(c == 0)
    def _():
        xpx_sc[0:8, :] = jnp.zeros((8, D_SSM), F32)
        xpbc_sc[0:8, :] = jnp.zeros((8, D_XBC - D_SSM), F32)
        h_sc[...] = jnp.zeros(h_sc.shape, F32)

    xpx_sc[8:8 + t, :] = x_ref[...]
    xpbc_sc[8:8 + t, :] = bc_ref[...]
    cw = cw_ref[...]
    cb = cb_ref[...]
    xs = _conv_silu(xpx_sc, cw[:, :D_SSM], cb[:, :D_SSM], t)
    bc = _conv_silu(xpbc_sc, cw[:, D_SSM:], cb[:, D_SSM:], t)
    xpx_sc[5:8, :] = xpx_sc[t + 5:t + 8, :]
    xpbc_sc[5:8, :] = xpbc_sc[t + 5:t + 8, :]

    dt = _softplus(dt_ref[...] + dtb_ref[...])
    da = dt * (-jnp.exp(alog_ref[...]))
    r_i = lax.broadcasted_iota(jnp.int32, (t, t), 0)
    c_i = lax.broadcasted_iota(jnp.int32, (t, t), 1)
    tril = c_i <= r_i
    acs = jnp.dot(tril.astype(F32), da, preferred_element_type=F32, precision=lax.Precision.HIGHEST)
    acs_t = acs.T
    lane = lax.broadcasted_iota(jnp.int32, (t, LANES), 1)
    sub = lax.broadcasted_iota(jnp.int32, (LANES, LANES), 0)
    lo_lane = lane < SSM_HEADDIM
    lo_sub = sub < SSM_HEADDIM

    for g in range(SSM_GROUPS):
        bg = bc[:, g * D_STATE:(g + 1) * D_STATE]
        cg = bc[:, (SSM_GROUPS + g) * D_STATE:(SSM_GROUPS + g + 1) * D_STATE]
        bgb = bg.astype(BF16)
        cgb = cg.astype(BF16)
        gmat = lax.dot_general(cgb, bgb, NT_DIMS, preferred_element_type=F32)
        for pr in range(hpg // hp):
            ha = g * hpg + pr * hp
            cb0 = ha * SSM_HEADDIM
            x2 = xs[:, cb0:cb0 + LANES]
            dt2 = jnp.where(lo_lane, dt[:, ha:ha + 1], dt[:, ha + 1:ha + 2])
            xdt = x2 * dt2
            ydiag = jnp.zeros((t, LANES), F32)
            for u in range(hp):
                hh = ha + u
                seg = acs[:, hh:hh + 1] - acs_t[hh:hh + 1, :]
                lmat = jnp.exp(jnp.where(tril, seg, -jnp.inf))
                xu = jnp.where(lo_lane if u == 0 else ~lo_lane, xdt, 0.0)
                ydiag = ydiag + jnp.dot((gmat * lmat).astype(BF16), xu.astype(BF16), preferred_element_type=F32)
            h2 = h_sc[ha:ha + hp].reshape(LANES, D_STATE)
            yoff = lax.dot_general(cgb, h2.astype(BF16), NT_DIMS, preferred_element_type=F32)
            e2 = jnp.where(lo_lane, jnp.exp(acs[:, ha:ha + 1]), jnp.exp(acs[:, ha + 1:ha + 2]))
            y = ydiag + e2 * yoff + dsk_ref[:, cb0:cb0 + LANES] * x2
            y_sc[:, cb0:cb0 + LANES] = y * _silu(zc_ref[:, cb0:cb0 + LANES])
            last = acs[t - 1:t, :]
            te2 = jnp.where(lo_lane, jnp.exp(last[:, ha:ha + 1] - acs[:, ha:ha + 1]),
                            jnp.exp(last[:, ha + 1:ha + 2] - acs[:, ha + 1:ha + 2]))
            upd = lax.dot_general((xdt * te2).astype(BF16), bgb, TN_DIMS, preferred_element_type=F32)
            dec = jnp.where(lo_sub, jnp.exp(last[:, ha:ha + 1]), jnp.exp(last[:, ha + 1:ha + 2]))
            h_sc[ha:ha + hp] = (dec * h2 + upd).reshape(hp, SSM_HEADDIM, D_STATE)

    gw = D_SSM // SSM_GROUPS
    for g in range(SSM_GROUPS):
        yg = y_sc[:, g * gw:(g + 1) * gw]
        var = jnp.mean(yg * yg, axis=-1, keepdims=True)
        y_ref[:, g * gw:(g + 1) * gw] = (yg * lax.rsqrt(var + RMS_EPS) * nw_ref[:, g * gw:(g + 1) * gw]).astype(y_ref.dtype)

    @pl.when(c == nc - 1)
    def _():
        hfin_ref[0] = h_sc[...]


def _ssd_prompt(z, cw, cb, dtb, alog, dsk, nw, *, batch, seq, t=128):
    nc = seq // t
    kern = functools.partial(_ssd_prompt_kernel, t=t)
    full = lambda shape: pl.BlockSpec(shape, lambda b, c: (0,) * len(shape))
    return pl.pallas_call(
        kern,
        grid=(batch, nc),
        in_specs=[
            pl.BlockSpec((t, D_SSM), lambda b, c: (b * nc + c, ZC_COL // D_SSM)),
            pl.BlockSpec((t, D_SSM), lambda b, c: (b * nc + c, XS_COL // D_SSM)),
            pl.BlockSpec((t, 512), lambda b, c: (b * nc + c, BC_COL // 512)),
            pl.BlockSpec((t, LANES), lambda b, c: (b * nc + c, DT_COL // LANES)),
            full((CONV_W, D_XBC)), full((1, D_XBC)), full((1, LANES)), full((1, LANES)),
            full((1, D_SSM)), full((1, D_SSM)),
        ],
        out_specs=[
            pl.BlockSpec((t, D_SSM), lambda b, c: (b * nc + c, 0)),
            pl.BlockSpec((1, SSM_HEADS, SSM_HEADDIM, D_STATE), lambda b, c: (b, 0, 0, 0)),
        ],
        out_shape=[
            jax.ShapeDtypeStruct((batch * seq, D_SSM), BF16),
            jax.ShapeDtypeStruct((batch, SSM_HEADS, SSM_HEADDIM, D_STATE), F32),
        ],
        scratch_shapes=[
            pltpu.VMEM((t + 8, D_SSM), F32), pltpu.VMEM((t + 8, D_XBC - D_SSM), F32),
            pltpu.VMEM((SSM_HEADS, SSM_HEADDIM, D_STATE), F32), pltpu.VMEM((t, D_SSM), F32),
        ],
        compiler_params=_cparams(2),
        name="ssd_prompt",
    )(z, z, z, z, cw, cb, dtb, alog, dsk, nw)


NEW_ROWS = 16


SEQS_PER_STEP = 2


def _attn_decode_kernel(*refs, moba, n_pages, page, t_new, lam_init, spb):
    pt_ref = refs[0]
    del pt_ref
    slopes_ref, dl_ref, nw_ref, q_ref, kn_ref, vn_ref = refs[1:7]
    n_pg = spb * n_pages
    kp_refs = refs[7:7 + n_pg]
    vp_refs = refs[7 + n_pg:7 + 2 * n_pg]
    o_ref = refs[7 + 2 * n_pg]
    kb_sc, vb_sc, new_sc = refs[8 + 2 * n_pg:]
    past = n_pages * page
    width = DA_HEADS * HEAD_DIM
    cols_per_head = t_new if moba else 2 * t_new
    lanes_per_col_grp = HEAD_DIM if moba else DA_QK
    scale = HEAD_DIM ** -0.5 if moba else DA_QK ** -0.5
    seqs = range(spb)
    coli = lax.broadcasted_iota(jnp.int32, (1, LANES), 1)
    slope = slopes_ref[...]

    ksums = []
    for s in seqs:
        ksum = []
        for p in range(n_pages):
            ksum_p = []
            kp, vp = kp_refs[s * n_pages + p], vp_refs[s * n_pages + p]
            for hd in range(DA_HEADS):
                lo = hd * HEAD_DIM
                kh = kp[pl.ds(hd, page, stride=DA_HEADS), :]
                kb_sc[s, p * page:(p + 1) * page, lo:lo + HEAD_DIM] = kh.astype(BF16)
                vb_sc[s, p * page:(p + 1) * page, lo:lo + HEAD_DIM] = vp[pl.ds(hd, page, stride=DA_HEADS), :].astype(BF16)
                if moba:
                    ksum_p.append(jnp.sum(kh, axis=0, keepdims=True))
            if moba:
                ksum.append(jnp.concatenate(ksum_p, axis=1))
        ksums.append(ksum)
        for j, (src, dst) in enumerate(((kn_ref, kb_sc), (vn_ref, vb_sc))):
            new_sc[s, j] = jnp.zeros(new_sc.shape[2:], F32)
            new_sc[s, j, 0:t_new, :] = src[s]
            dst[s, past:past + NEW_ROWS, :] = new_sc[s, j].astype(BF16)

    qs, scores = [], []
    for s in seqs:
        q = q_ref[s]
        qt = jnp.concatenate([q] * (LANES // t_new), axis=0)
        r_i = lax.broadcasted_iota(jnp.int32, qt.shape, 0)
        l_i = lax.broadcasted_iota(jnp.int32, qt.shape, 1)
        qmat = jnp.where(l_i // lanes_per_col_grp == r_i // t_new, qt, 0.0)
        qs.append(q)
        scores.append(lax.dot_general(kb_sc[s], (qmat * scale).astype(BF16), NT_DIMS,
                                      preferred_element_type=F32))

    probs = []
    for s in seqs:
        rowp = lax.broadcasted_iota(jnp.int32, (past, LANES), 0)
        s_past = scores[s][:past] + slope * (rowp - past).astype(F32)
        rown = lax.broadcasted_iota(jnp.int32, (NEW_ROWS, LANES), 0)
        s_new = scores[s][past:] + slope * rown.astype(F32)
        s_new = jnp.where(rown <= coli % t_new, s_new, -jnp.inf)

        if moba:
            nbp = past // MOBA_BLOCK
            ppb = MOBA_BLOCK // page
            gate = jnp.zeros((nbp, LANES), F32)
            kmean = jnp.concatenate(
                [sum(ksums[s][n * ppb:(n + 1) * ppb]) for n in range(nbp)], axis=0) * (1.0 / MOBA_BLOCK)
            for i in range(MB_HEADS * t_new):
                hd, tok = i // t_new, i % t_new
                lo = hd * HEAD_DIM
                g_i = jnp.sum(kmean[:, lo:lo + HEAD_DIM] * qs[s][tok:tok + 1, lo:lo + HEAD_DIM], axis=-1,
                              keepdims=True)
                gate = jnp.where(coli == i, g_i, gate)
            rowb = lax.broadcasted_iota(jnp.int32, gate.shape, 0)
            sel = jnp.zeros(gate.shape, jnp.bool_)
            g = gate
            for _ in range(MOBA_TOPK):
                mx = jnp.max(g, axis=0, keepdims=True)
                idx = jnp.min(jnp.where(g == mx, rowb, nbp), axis=0, keepdims=True)
                pick = (rowb == idx) & (mx > -jnp.inf)
                sel = sel | pick
                g = jnp.where(pick, -jnp.inf, g)
            sel_bias = jnp.where(sel, 0.0, -jnp.inf)
            s_past = (s_past.reshape(nbp, MOBA_BLOCK, LANES) + sel_bias[:, None, :]).reshape(past, LANES)

        m = jnp.maximum(jnp.max(s_past, axis=0, keepdims=True), jnp.max(s_new, axis=0, keepdims=True))
        p_past = jnp.exp(s_past - m)
        p_new = jnp.exp(s_new - m)
        l = jnp.sum(p_past, axis=0, keepdims=True) + jnp.sum(p_new, axis=0, keepdims=True)
        probs.append((jnp.concatenate([p_past, p_new], axis=0).astype(BF16), l))

    outs_t = [lax.dot_general(vb_sc[s], probs[s][0], TN_DIMS, preferred_element_type=F32) for s in seqs]

    for s in seqs:
        o = outs_t[s].T
        l_col = jnp.broadcast_to(probs[s][1], (LANES, LANES)).T
        o = o / jnp.concatenate([l_col] * (width // LANES), axis=1)
        outs = []
        for hd in range(DA_HEADS):
            lo = hd * HEAD_DIM
            r0 = hd * cols_per_head
            if moba:
                outs.append(o[r0:r0 + t_new, lo:lo + HEAD_DIM])
            else:
                dl = dl_ref[...]
                lam = (jnp.exp(jnp.sum(dl[0:1] * dl[1:2], axis=-1, keepdims=True))
                       - jnp.exp(jnp.sum(dl[2:3] * dl[3:4], axis=-1, keepdims=True)) + lam_init)
                oh = o[r0:r0 + t_new, lo:lo + HEAD_DIM] - lam * o[r0 + t_new:r0 + 2 * t_new, lo:lo + HEAD_DIM]
                var = jnp.mean(oh * oh, axis=-1, keepdims=True)
                outs.append(oh * lax.rsqrt(var + RMS_EPS) * nw_ref[...] * (1.0 - lam_init))
        o_ref[s] = jnp.concatenate(outs, axis=1).astype(o_ref.dtype)


def _attn_decode(zs, cache_k, cache_v, page_table, slopes_cols, dl, nw, *, layer, moba, lam_init, spb=SEQS_PER_STEP):
    n_seq, t_new, _ = zs.shape
    page = cache_k.shape[2] // DA_HEADS
    n_pages = page_table.shape[1]
    width = DA_HEADS * HEAD_DIM
    qblk, kblk, vblk = (QB_BLK, KB_BLK, VB_BLK) if moba else (QA_BLK, KA_BLK, VA_BLK)
    past = n_pages * page
    spb = spb if n_seq % spb == 0 else 1

    def page_spec(s, p):
        return pl.BlockSpec((None, None, page * DA_HEADS, HEAD_DIM), lambda b, pt: (layer, pt[b * spb + s, p], 0, 0))

    def z_spec(blk):
        return pl.BlockSpec((spb, t_new, width), lambda b, pt: (b, 0, blk // DA_HEADS))

    kern = functools.partial(_attn_decode_kernel, moba=moba, n_pages=n_pages, page=page, t_new=t_new,
                             lam_init=lam_init, spb=spb)
    page_specs = [page_spec(s, p) for s in range(spb) for p in range(n_pages)]
    grid_spec = pltpu.PrefetchScalarGridSpec(
        num_scalar_prefetch=1,
        grid=(n_seq // spb,),
        in_specs=[
            pl.BlockSpec((1, LANES), lambda b, pt: (0, 0)),
            pl.BlockSpec((4, DA_QK), lambda b, pt: (0, 0)),
            pl.BlockSpec((1, HEAD_DIM), lambda b, pt: (0, 0)),
            z_spec(qblk), z_spec(kblk), z_spec(vblk),
        ] + page_specs * 2,
        out_specs=pl.BlockSpec((spb, t_new, width), lambda b, pt: (b, 0, 0)),
        scratch_shapes=[
            pltpu.VMEM((spb, past + NEW_ROWS, width), BF16), pltpu.VMEM((spb, past + NEW_ROWS, width), BF16),
            pltpu.VMEM((spb, 2, NEW_ROWS, width), F32),
        ],
    )
    return pl.pallas_call(
        kern,
        grid_spec=grid_spec,
        out_shape=jax.ShapeDtypeStruct((n_seq, t_new, width), BF16),
        compiler_params=_cparams(1),
        name="moba_decode" if moba else "diff_decode",
    )(page_table, slopes_cols, dl, nw, zs, zs, zs, *([cache_k] * len(page_specs)), *([cache_v] * len(page_specs)))


def _ssd_decode_kernel(zc_ref, xs_ref, bc_ref, dtx_ref, cst_ref, h0_ref, cw_ref, cb_ref, dtb_ref, alog_ref,
                       dsk_ref, nw_ref, y_ref, hout_ref, cout_ref, xp_sc, *, t_new):
    hpg = SSM_HEADS // SSM_GROUPS
    gw = D_SSM // SSM_GROUPS
    xp_sc[0:CONV_W - 1, :] = cst_ref[...]
    xp_sc[CONV_W - 1:CONV_W - 1 + t_new, :D_SSM] = xs_ref[...]
    xp_sc[CONV_W - 1:CONV_W - 1 + t_new, D_SSM:] = bc_ref[...]
    cout_ref[...] = xp_sc[t_new:t_new + CONV_W - 1, :]
    y = cb_ref[...]
    for j in range(CONV_W):
        y = y + xp_sc[j:j + t_new, :] * cw_ref[j:j + 1, :]
    xbc = _silu(y)
    xs = xbc[:, :D_SSM]
    dt = _softplus(dtx_ref[...] + dtb_ref[...])
    da = dt * (-jnp.exp(alog_ref[...]))
    acs = [da[0:1]]
    for s in range(1, t_new):
        acs.append(acs[-1] + da[s:s + 1])
    xdt = xs * dt
    last = acs[-1]
    wx = jnp.concatenate([xdt[s:s + 1] * jnp.exp(last - acs[s]) for s in range(t_new)], axis=0)
    dec = jnp.exp(last)
    d_hi = dec.astype(BF16).astype(F32)
    d_md = (dec - d_hi).astype(BF16).astype(F32)
    d_lo = (dec - d_hi - d_md).astype(BF16).astype(F32)
    dec3 = jnp.concatenate([d_hi, d_md, d_lo, jnp.zeros((5, D_SSM), F32)], axis=0).astype(BF16)
    ones = jnp.ones((8, D_STATE), BF16)
    zc = zc_ref[...]
    for g in range(SSM_GROUPS):
        sl = slice(g * gw, (g + 1) * gw)
        bg = xbc[:, D_SSM + g * D_STATE:D_SSM + (g + 1) * D_STATE]
        cg = xbc[:, D_SSM + (SSM_GROUPS + g) * D_STATE:D_SSM + (SSM_GROUPS + g + 1) * D_STATE]
        h0 = h0_ref[g * hpg:(g + 1) * hpg].reshape(gw, D_STATE)
        yoff = lax.dot_general(cg.astype(BF16), h0.astype(BF16), NT_DIMS, preferred_element_type=F32)
        rows = []
        for t in range(t_new):
            yt = jnp.exp(acs[t][:, sl]) * yoff[t:t + 1]
            for s in range(t + 1):
                gts = jnp.sum(cg[t:t + 1] * bg[s:s + 1], axis=-1, keepdims=True)
                yt = yt + gts * jnp.exp(acs[t][:, sl] - acs[s][:, sl]) * xdt[s:s + 1, sl]
            rows.append(yt)
        yg = jnp.concatenate(rows, axis=0) + dsk_ref[:, sl] * xs[:, sl]
        yg = yg * _silu(zc[:, sl])
        var = jnp.mean(yg * yg, axis=-1, keepdims=True)
        y_ref[:, sl] = (yg * lax.rsqrt(var + RMS_EPS) * nw_ref[:, sl]).astype(y_ref.dtype)
        wxg = wx[:, sl]
        wx_hi = wxg.astype(BF16).astype(F32)
        bg_hi = bg.astype(BF16).astype(F32)
        pad_rows = 16 - 3 * t_new
        lhs = jnp.concatenate([wx_hi, wxg - wx_hi, wx_hi, jnp.zeros((pad_rows, gw), F32)], axis=0).astype(BF16)
        rhs = jnp.concatenate([bg_hi, bg_hi, bg - bg_hi, jnp.zeros((pad_rows, D_STATE), F32)], axis=0).astype(BF16)
        upd = lax.dot_general(lhs, rhs, TN_DIMS, preferred_element_type=F32)
        dec_full = lax.dot_general(dec3[:, sl], ones, TN_DIMS, preferred_element_type=F32)
        hout_ref[g * hpg:(g + 1) * hpg] = (dec_full * h0 + upd).reshape(hpg, SSM_HEADDIM, D_STATE)


def _ssd_decode(zs, dtx, state_conv, state_ssm, cw, cb, dtbx, alogx, dskx, nw, *, layer):
    n_seq, t_new, _ = zs.shape
    kern = functools.partial(_ssd_decode_kernel, t_new=t_new)
    full = lambda shape: pl.BlockSpec(shape, lambda b: (0,) * len(shape))
    return pl.pallas_call(
        kern,
        grid=(n_seq,),
        in_specs=[
            pl.BlockSpec((None, t_new, D_SSM), lambda b: (b, 0, ZC_COL // D_SSM)),
            pl.BlockSpec((None, t_new, D_SSM), lambda b: (b, 0, XS_COL // D_SSM)),
            pl.BlockSpec((None, t_new, 512), lambda b: (b, 0, BC_COL // 512)),
            pl.BlockSpec((None, t_new, D_SSM), lambda b: (b, 0, 0)),
            pl.BlockSpec((None, None, CONV_W - 1, D_XBC), lambda b: (layer, b, 0, 0)),
            pl.BlockSpec((None, None, SSM_HEADS, SSM_HEADDIM, D_STATE), lambda b: (layer, b, 0, 0, 0)),
            full((CONV_W, D_XBC)), full((1, D_XBC)), full((1, D_SSM)), full((1, D_SSM)),
            full((1, D_SSM)), full((1, D_SSM)),
        ],
        out_specs=[
            pl.BlockSpec((None, t_new, D_SSM), lambda b: (b, 0, 0)),
            pl.BlockSpec((None, SSM_HEADS, SSM_HEADDIM, D_STATE), lambda b: (b, 0, 0, 0)),
            pl.BlockSpec((None, CONV_W - 1, D_XBC), lambda b: (b, 0, 0)),
        ],
        out_shape=[
            jax.ShapeDtypeStruct((n_seq, t_new, D_SSM), BF16),
            jax.ShapeDtypeStruct((n_seq, SSM_HEADS, SSM_HEADDIM, D_STATE), F32),
            jax.ShapeDtypeStruct((n_seq, CONV_W - 1, D_XBC), F32),
        ],
        scratch_shapes=[pltpu.VMEM((8, D_XBC), F32)],
        compiler_params=_cparams(1),
        name="ssd_decode",
    )(zs, zs, zs, dtx, state_conv, state_ssm, cw, cb, dtbx, alogx, dskx, nw)


def _layer_norm(x, w, b):
    mu = jnp.mean(x, axis=-1, keepdims=True)
    xc = x - mu
    var = jnp.mean(xc * xc, axis=-1, keepdims=True)
    return xc * lax.rsqrt(var + LN_EPS) * w + b


def _top2_of(vals, col, n):
    m1 = jnp.max(vals, axis=-1, keepdims=True)
    i1 = jnp.min(jnp.where(vals == m1, col, n), axis=-1, keepdims=True)
    rest = jnp.where(col == i1, -jnp.inf, vals)
    m2 = jnp.max(rest, axis=-1, keepdims=True)
    i2 = jnp.min(jnp.where(rest == m2, col, n), axis=-1, keepdims=True)
    return m1, i1, m2, i2


def _router_gates(logits, b_router):
    scores = jax.nn.sigmoid(logits)
    biased = scores + b_router
    col = lax.broadcasted_iota(jnp.int32, logits.shape, 1)
    grp = col // EXPERTS_PER_GROUP
    best = jnp.zeros((logits.shape[0], 1), jnp.int32)
    best_v = None
    for g in range(N_EXPERT_GROUPS):
        m1, _, m2, _ = _top2_of(jnp.where(grp == g, biased, -jnp.inf), col, N_EXPERTS)
        gs = m1 + m2
        if g == 0:
            best_v = gs
        else:
            upd = gs > best_v
            best = jnp.where(upd, g, best)
            best_v = jnp.where(upd, gs, best_v)
    _, i1, _, i2 = _top2_of(jnp.where(grp == best, biased, -jnp.inf), col, N_EXPERTS)
    w = jnp.where((col == i1) | (col == i2), scores, 0.0)
    return w / jnp.sum(w, axis=-1, keepdims=True)


ROW_CHUNKS = 16


def _store_token_rows(ref, val):
    rows = val.shape[0]
    for c in range(ROW_CHUNKS):
        ref[pl.ds(c, rows, stride=ROW_CHUNKS), :] = val[:, c * LANES:(c + 1) * LANES]


def _load_token_rows(ref, rows):
    return jnp.concatenate([ref[pl.ds(c, rows, stride=ROW_CHUNKS), :] for c in range(ROW_CHUNKS)], axis=1)


def _post1_kernel(x_ref, oa_ref, ob_ref, oc_ref, wa_ref, wb_ref, wc_ref, lw_ref, lb_ref, wr_ref, br_ref,
                  x1r_ref, x1b_ref, g_ref, *, alpha):
    tm = x_ref.shape[0]
    sub = 128
    mixes = []
    for r in range(tm // sub):
        rows = slice(r * sub, (r + 1) * sub)
        mixes.append(jnp.dot(oa_ref[rows, :], wa_ref[...], preferred_element_type=F32)
                     + jnp.dot(ob_ref[rows, :], wb_ref[...], preferred_element_type=F32)
                     + jnp.dot(oc_ref[rows, :], wc_ref[...], preferred_element_type=F32))
    for r in range(tm // sub):
        rows = slice(r * sub, (r + 1) * sub)
        x1 = _layer_norm(alpha * x_ref[rows, :] + mixes[r], lw_ref[...], lb_ref[...])
        _store_token_rows(x1r_ref.at[pl.ds(r * sub * ROW_CHUNKS, sub * ROW_CHUNKS), :], x1)
        x1b_ref[rows, :] = x1.astype(BF16)
        x1_hi = x1.astype(BF16)
        x1_lo = (x1 - x1_hi.astype(F32)).astype(BF16)
        logits = jnp.dot(jnp.concatenate([x1_hi, x1_lo, x1_hi], axis=1), wr_ref[...], preferred_element_type=F32)
        g_ref[rows, :] = _router_gates(logits, br_ref[...])


def _router_weight_passes(w_router):
    wr_hi = w_router.astype(BF16)
    wr_lo = (w_router - wr_hi.astype(F32)).astype(BF16)
    return jnp.concatenate([wr_hi, wr_hi, wr_lo], axis=0)


def _post1(x, oa, ob, oc, w_out_b, lw, lb, w_router3, b_router, *, layer, alpha, tm=512):
    n, d = x.shape
    assert d == ROW_CHUNKS * LANES
    full = lambda shape: pl.BlockSpec(shape, lambda i: (0,) * len(shape))
    row = lambda width: pl.BlockSpec((tm, width), lambda i: (i, 0))
    per_layer = lambda shape: pl.BlockSpec((None,) + shape, lambda i: (layer,) + (0,) * len(shape))
    wa, wb = DA_HEADS * HEAD_DIM, MB_HEADS * HEAD_DIM
    return pl.pallas_call(
        functools.partial(_post1_kernel, alpha=alpha),
        grid=(n // tm,),
        in_specs=[
            row(d), row(wa), row(wb), row(D_SSM),
            pl.BlockSpec((None, wa, d), lambda i: (layer, 0, 0)), pl.BlockSpec((None, wb, d), lambda i: (layer, 1, 0)),
            pl.BlockSpec((None, D_SSM, d), lambda i: (layer, 1, 0)),
            per_layer((1, d)), per_layer((1, d)), full((3 * d, N_EXPERTS)), full((1, N_EXPERTS)),
        ],
        out_specs=[pl.BlockSpec((tm * ROW_CHUNKS, LANES), lambda i: (i, 0)), row(d), row(N_EXPERTS)],
        out_shape=[jax.ShapeDtypeStruct((n * ROW_CHUNKS, LANES), F32), jax.ShapeDtypeStruct((n, d), BF16),
                   jax.ShapeDtypeStruct((n, N_EXPERTS), F32)],
        compiler_params=_cparams(1),
        name="out_proj_ln1_router",
    )(x, oa, ob, oc, w_out_b, w_out_b, w_out_b, lw, lb, w_router3, b_router)


MOE_TILE = 256
EXPERTS_PER_STEP = 2


def _moe_routing(gates, tile):
    n = gates.shape[0]
    n_tiles = n // tile + N_EXPERT_GROUPS
    n_slots = n_tiles * tile
    expert_ids = jnp.arange(N_EXPERTS, dtype=jnp.int32)[None, :]
    gid = jnp.min(jnp.where(gates > 0, expert_ids, N_EXPERTS), axis=1) // EXPERTS_PER_GROUP
    onehot = (gid[:, None] == jnp.arange(N_EXPERT_GROUPS, dtype=jnp.int32)[None, :]).astype(jnp.int32)
    rank = jnp.sum((jnp.cumsum(onehot, axis=0) - onehot) * onehot, axis=1)
    counts = jnp.sum(onehot, axis=0)
    padded = (counts + tile - 1) // tile * tile
    seg_end = jnp.cumsum(padded)
    slot = (seg_end - padded)[gid] + rank
    tok = jnp.arange(n, dtype=jnp.int32)
    slot_ids = jnp.arange(n_slots + tile, dtype=jnp.int32)
    slot_tok = jnp.full((n_slots + tile,), -1, jnp.int32).at[slot].set(tok)
    is_real = slot_tok >= 0
    slot_src = jnp.maximum(slot_tok, 0)
    slot_dst = jnp.where(is_real, slot_tok, n + (slot_ids // tile % 2) * tile + slot_ids % tile)
    grp_gates = jnp.take_along_axis(gates.reshape(n, N_EXPERT_GROUPS, EXPERTS_PER_GROUP),
                                    gid[:, None, None], axis=1)[:, 0]
    slot_gates = jnp.where(is_real[:n_slots, None], grp_gates[slot_src[:n_slots]], 0.0)
    tile_start = jnp.arange(n_tiles, dtype=jnp.int32) * tile
    tile_grp = jnp.minimum(jnp.sum((tile_start[:, None] >= seg_end[None, :]).astype(jnp.int32), axis=1),
                           N_EXPERT_GROUPS - 1)
    return tile_grp, slot_src, slot_dst, slot_gates


def _moe_kernel(tile_grp_ref, src_ref, dst_ref, x_hbm, g_ref, wg_ref, wu_ref, wd_ref, ff_hbm,
                xg_sc, x_sc, acc_sc, ys_sc, gsem, ssem, *, tile, n_tok):
    del tile_grp_ref
    i = pl.program_id(0)
    e = pl.program_id(1)
    n_tiles = pl.num_programs(0)
    n_e = EXPERTS_PER_GROUP
    slab = ROW_CHUNKS

    def gather_copy(slot, row, buf):
        tok = src_ref[slot]
        return pltpu.make_async_copy(x_hbm.at[pl.ds(tok * slab, slab), :],
                                     xg_sc.at[buf, pl.ds(row * slab, slab), :], gsem.at[buf])

    def scatter_copy(slot, row, buf):
        tok = dst_ref[slot]
        return pltpu.make_async_copy(ys_sc.at[buf, pl.ds(row * slab, slab), :],
                                     ff_hbm.at[pl.ds(tok * slab, slab), :], ssem.at[buf])

    def wait_all(sc, sem, buf):
        pltpu.make_async_copy(sc.at[buf], sc.at[buf], sem.at[buf]).wait()

    cur = i % 2
    nxt = (i + 1) % 2
    n_steps = n_e // EXPERTS_PER_STEP

    @pl.when((i == 0) & (e == 0))
    def _():
        for j in range(tile):
            gather_copy(j, j, 0).start()
        ys_sc[...] = jnp.zeros(ys_sc.shape, F32)
        for b in range(2):
            fill = pltpu.make_async_copy(
                ys_sc.at[b], ff_hbm.at[pl.ds((n_tok + b * tile) * slab, tile * slab), :], ssem.at[b])
            fill.start()
            fill.wait()

    part_rows = tile // n_steps
    for j in range(part_rows):
        row = e * part_rows + j
        gather_copy((i + 1) * tile + row, row, nxt).start()

    @pl.when(e == 0)
    def _():
        wait_all(xg_sc, gsem, cur)
        for c in range(ROW_CHUNKS):
            x_sc[:, c * LANES:(c + 1) * LANES] = xg_sc[cur, pl.ds(c, tile, stride=ROW_CHUNKS), :].astype(BF16)

    x = x_sc[...]
    col = lax.broadcasted_iota(jnp.int32, g_ref.shape, 1)
    hids = []
    for u in range(EXPERTS_PER_STEP):
        hid = _silu(jnp.dot(x, wg_ref[u], preferred_element_type=F32)) * jnp.dot(x, wu_ref[u],
                                                                                 preferred_element_type=F32)
        gate = jnp.sum(jnp.where(col == e * EXPERTS_PER_STEP + u, g_ref[...], 0.0), axis=-1, keepdims=True)
        hids.append((hid * gate).astype(BF16))
    part = sum(jnp.dot(hids[u], wd_ref[u], preferred_element_type=F32) for u in range(EXPERTS_PER_STEP))

    @pl.when(e == 0)
    def _():
        acc_sc[...] = part

    @pl.when(e > 0)
    def _():
        acc_sc[...] += part

    @pl.when(e == n_steps - 1)
    def _():
        @pl.when(i >= 2)
        def _():
            wait_all(ys_sc, ssem, cur)

        y = acc_sc[...]
        for c in range(ROW_CHUNKS):
            ys_sc[cur, pl.ds(c, tile, stride=ROW_CHUNKS), :] = y[:, c * LANES:(c + 1) * LANES]
        for j in range(tile):
            scatter_copy(i * tile + j, j, cur).start()

        @pl.when(i == n_tiles - 1)
        def _():
            wait_all(ys_sc, ssem, cur)

            @pl.when(n_tiles >= 2)
            def _():
                wait_all(ys_sc, ssem, nxt)

            wait_all(xg_sc, gsem, nxt)


def _moe(x1r, gates, wg, wu, wd, *, layer=0, tile=MOE_TILE):
    n = gates.shape[0]
    d, f = wg.shape[1], wg.shape[2]
    e0 = layer * N_EXPERTS
    tile_grp, slot_src, slot_dst, slot_gates = _moe_routing(gates, tile)
    n_tiles = tile_grp.shape[0]
    n_rows = n + 2 * tile
    n_e = EXPERTS_PER_GROUP
    eps = EXPERTS_PER_STEP
    n_steps = n_e // eps
    w_idx = lambda i, e, tg, s, t: ((e0 + tg[i] * n_e) // eps + e, 0, 0)
    grid_spec = pltpu.PrefetchScalarGridSpec(
        num_scalar_prefetch=3,
        grid=(n_tiles, n_steps),
        in_specs=[
            pl.BlockSpec(memory_space=pl.ANY),
            pl.BlockSpec((tile, n_e), lambda i, e, tg, s, t: (i, 0)),
            pl.BlockSpec((eps, d, f), w_idx), pl.BlockSpec((eps, d, f), w_idx), pl.BlockSpec((eps, f, d), w_idx),
        ],
        out_specs=pl.BlockSpec(memory_space=pl.ANY),
        scratch_shapes=[
            pltpu.VMEM((2, tile * ROW_CHUNKS, LANES), F32), pltpu.VMEM((tile, d), BF16),
            pltpu.VMEM((tile, d), F32), pltpu.VMEM((2, tile * ROW_CHUNKS, LANES), F32),
            pltpu.SemaphoreType.DMA((2,)), pltpu.SemaphoreType.DMA((2,)),
        ],
    )
    return pl.pallas_call(
        functools.partial(_moe_kernel, tile=tile, n_tok=n),
        grid_spec=grid_spec,
        out_shape=jax.ShapeDtypeStruct((n_rows * ROW_CHUNKS, LANES), F32),
        compiler_params=_cparams(2),
        name="moe_grouped",
    )(tile_grp, slot_src, slot_dst, x1r, slot_gates, wg, wu, wd)


def _post2_kernel(x1r_ref, x1b_ref, ffr_ref, p_ref, wgate_ref, wproj_ref, lw_ref, lb_ref, out_a_ref, out_b_ref, *,
                  alpha, head_tiles):
    i = pl.program_id(0)
    tm = x1b_ref.shape[0]
    sub = 128
    gates, projs = [], []
    for r in range(tm // sub):
        rows = slice(r * sub, (r + 1) * sub)
        gates.append(jnp.dot(x1b_ref[rows, :], wgate_ref[...], preferred_element_type=F32))
        projs.append(jnp.dot(p_ref[rows, :], wproj_ref[...], preferred_element_type=F32))
    for r in range(tm // sub):
        rows = slice(r * sub, (r + 1) * sub)
        slab = pl.ds(r * sub * ROW_CHUNKS, sub * ROW_CHUNKS)
        ple = jax.nn.sigmoid(gates[r]) * projs[r]
        x2 = _layer_norm(alpha * _load_token_rows(x1r_ref.at[slab, :], sub)
                         + _load_token_rows(ffr_ref.at[slab, :], sub) + ple, lw_ref[...], lb_ref[...])
        if head_tiles is None:
            out_a_ref[rows, :] = x2
            out_b_ref[rows, :] = x2.astype(BF16)
        else:
            @pl.when(i < head_tiles)
            def _():
                out_a_ref[rows, :] = x2

            @pl.when(i >= head_tiles)
            def _():
                out_b_ref[rows, :] = x2


def _post2(x1r, x1b, ffr, p, w_gate_b, w_proj_b, lw, lb, *, layer, alpha, split_at=None, tm=512):
    n, d = x1b.shape
    dp = p.shape[1]
    per_layer = lambda shape: pl.BlockSpec((None,) + shape, lambda i: (layer,) + (0,) * len(shape))
    row = lambda width: pl.BlockSpec((tm, width), lambda i: (i, 0))
    slab = pl.BlockSpec((tm * ROW_CHUNKS, LANES), lambda i: (i, 0))
    if split_at is None:
        head_tiles = None
        out_specs = [row(d), row(d)]
        out_shape = [jax.ShapeDtypeStruct((n, d), F32), jax.ShapeDtypeStruct((n, d), BF16)]
    else:
        assert split_at % tm == 0 and (n - split_at) % tm == 0
        head_tiles = split_at // tm
        out_specs = [pl.BlockSpec((tm, d), lambda i: (jnp.minimum(i, head_tiles - 1), 0)),
                     pl.BlockSpec((tm, d), lambda i: (jnp.maximum(i - head_tiles, 0), 0))]
        out_shape = [jax.ShapeDtypeStruct((split_at, d), F32), jax.ShapeDtypeStruct((n - split_at, d), F32)]
    return pl.pallas_call(
        functools.partial(_post2_kernel, alpha=alpha, head_tiles=head_tiles),
        grid=(n // tm,),
        in_specs=[slab, row(d), slab, row(dp), per_layer((d, d)), per_layer((dp, d)), per_layer((1, d)),
                  per_layer((1, d))],
        out_specs=out_specs,
        out_shape=out_shape,
        compiler_params=_cparams(1),
        name="ple_ln2",
    )(x1r, x1b, ffr, p, w_gate_b, w_proj_b, lw, lb)


def kernel(x_prompt, x_sample, p_prompt, p_sample, cache_diff_k, cache_diff_v, cache_moba_k, cache_moba_v, state_ssm, state_conv, page_table, w_in, w_out, diff_lambda, diff_norm_w, ssm_conv_w, ssm_conv_b, ssm_dt_bias, ssm_a_log, ssm_d, ssm_norm_w, ln1_w, ln1_b, ln2_w, ln2_b, w_router, b_router, w_exp_gate, w_exp_up, w_exp_down, w_ple_proj, w_ple_gate):
    depth = w_in.shape[0]
    batch, seq, d_model = x_prompt.shape
    n_seq, t_new, _ = x_sample.shape
    n_p, n_s = batch * seq, n_seq * t_new
    alpha = (2 * depth) ** 0.25
    n_heads = DA_HEADS + MB_HEADS
    slopes = jnp.asarray(2.0 ** (-8.0 * np.arange(1, n_heads + 1) / n_heads), dtype=F32)

    def score_col_slopes(sl, cols_per_head):
        v = jnp.repeat(sl, cols_per_head)
        return jnp.pad(v, (0, LANES - v.shape[0])).reshape(1, LANES)

    slopes_diff_cols = score_col_slopes(slopes[:DA_HEADS], 2 * t_new)
    slopes_moba_cols = score_col_slopes(slopes[DA_HEADS:], t_new)
    pad_lanes = lambda v: jnp.pad(v, (0, LANES - v.shape[0])).reshape(1, LANES)
    per_channel = lambda v: jnp.repeat(v, SSM_HEADDIM).reshape(1, D_SSM)

    n_pool, page = cache_diff_k.shape[1], cache_diff_k.shape[2]
    flat_cache = lambda c: c.reshape(depth, n_pool, page * DA_HEADS, HEAD_DIM)
    ck_a, cv_a, ck_b, cv_b = map(flat_cache, (cache_diff_k, cache_diff_v, cache_moba_k, cache_moba_v))

    w_in_b = jnp.pad(w_in.astype(BF16), ((0, 0), (0, 0), (0, D_IN_PAD - D_IN_PROJ)))
    w_dtx = jnp.repeat(w_in[:, :, DT_COL:DT_COL + SSM_HEADS].astype(BF16), SSM_HEADDIM, axis=2)
    w_out_b = w_out.astype(BF16)
    w_gate_b, w_proj_b = w_ple_gate.astype(BF16), w_ple_proj.astype(BF16)
    d_exp = w_exp_gate.shape[-1]
    wg_b = w_exp_gate.astype(BF16).reshape(depth * N_EXPERTS, d_model, d_exp)
    wu_b = w_exp_up.astype(BF16).reshape(depth * N_EXPERTS, d_model, d_exp)
    wd_b = w_exp_down.astype(BF16).reshape(depth * N_EXPERTS, d_exp, d_model)
    w_router3 = _router_weight_passes(w_router)
    b_router2 = b_router.reshape(1, N_EXPERTS)
    ln_rows = lambda v: v.reshape(depth, 1, d_model)
    ln1_w3, ln1_b3, ln2_w3, ln2_b3 = map(ln_rows, (ln1_w, ln1_b, ln2_w, ln2_b))

    x = jnp.concatenate([x_prompt.reshape(n_p, d_model), x_sample.reshape(n_s, d_model)], axis=0)
    xb = x.astype(BF16)
    outs_p = [[] for _ in range(6)]
    outs_s = [[] for _ in range(6)]
    for i in range(depth):
        lam_init = 0.8 - 0.6 * math.exp(-0.3 * i)
        z = _matmul(xb, w_in_b, i, tm=512, tn=D_IN_PAD // 3)
        zs = z[n_p:].reshape(n_seq, t_new, D_IN_PAD)
        dtx = _matmul(xb[n_p:], w_dtx, i, tm=n_s, tn=D_SSM).reshape(n_seq, t_new, D_SSM)

        dl = diff_lambda[i]
        nw_a = diff_norm_w[i].reshape(1, HEAD_DIM)
        cw, cb = ssm_conv_w[i], ssm_conv_b[i].reshape(1, D_XBC)
        nw_c = ssm_norm_w[i].reshape(1, D_SSM)

        ka, vt, kmean, *kv_new_p = _kv_prep(z, batch=batch, seq=seq)
        oa_p = _diff_prompt(z, ka, vt, slopes, dl, nw_a, batch=batch, seq=seq, lam_init=lam_init)
        ob_p = _moba_prompt(z, ka, vt, kmean, slopes, batch=batch, seq=seq)
        oc_p, h_p = _ssd_prompt(z, cw, cb, pad_lanes(ssm_dt_bias[i]), pad_lanes(ssm_a_log[i]),
                                per_channel(ssm_d[i]), nw_c, batch=batch, seq=seq)
        oa_s = _attn_decode(zs, ck_a, cv_a, page_table, slopes_diff_cols, dl, nw_a, layer=i, moba=False,
                            lam_init=lam_init)
        ob_s = _attn_decode(zs, ck_b, cv_b, page_table, slopes_moba_cols, dl, nw_a, layer=i, moba=True,
                            lam_init=lam_init)
        oc_s, h_s, conv_s = _ssd_decode(zs, dtx, state_conv, state_ssm, cw, cb, per_channel(ssm_dt_bias[i]),
                                        per_channel(ssm_a_log[i]), per_channel(ssm_d[i]), nw_c, layer=i)

        head_shape = lambda a, lead: a.reshape(*lead, DA_HEADS, HEAD_DIM)
        for j, blk in enumerate((KA_BLK, VA_BLK, KB_BLK, VB_BLK)):
            c0 = blk * LANES
            outs_p[j].append(kv_new_p[j])
            outs_s[j].append(head_shape(zs[:, :, c0:c0 + DA_HEADS * HEAD_DIM], (n_seq, t_new)))
        outs_p[4].append(h_p)
        outs_p[5].append(jnp.stack([z[(b + 1) * seq - (CONV_W - 1):(b + 1) * seq, XS_COL:XS_COL + D_XBC]
                                    for b in range(batch)]))
        outs_s[4].append(h_s)
        outs_s[5].append(conv_s)

        oa = jnp.concatenate([oa_p, oa_s.reshape(n_s, -1)], axis=0)
        ob = jnp.concatenate([ob_p, ob_s.reshape(n_s, -1)], axis=0)
        oc = jnp.concatenate([oc_p, oc_s.reshape(n_s, -1)], axis=0)
        x1r, x1b, gates = _post1(x, oa, ob, oc, w_out_b, ln1_w3, ln1_b3, w_router3, b_router2, layer=i, alpha=alpha)
        ffr = _moe(x1r, gates, wg_b, wu_b, wd_b, layer=i)
        p = jnp.concatenate([p_prompt[i].reshape(n_p, -1), p_sample[i].reshape(n_s, -1)], axis=0).astype(BF16)
        last = i == depth - 1
        x, xb = _post2(x1r, x1b, ffr, p, w_gate_b, w_proj_b, ln2_w3, ln2_b3, layer=i, alpha=alpha,
                       split_at=n_p if last else None, tm=256 if last else 512)

    y_p = x.reshape(batch, seq, d_model)
    y_s = xb.reshape(n_seq, t_new, d_model)
    return (y_p, y_s, *[jnp.stack(o) for o in outs_p], *[jnp.stack(o) for o in outs_s])
```

```python
import functools
import math

import jax
import jax.numpy as jnp
import numpy as np
from jax import lax
from jax.experimental import pallas as pl
from jax.experimental.pallas import tpu as pltpu

F32 = jnp.float32
BF16 = jnp.bfloat16

LANES = 128
HEAD_DIM = 128
DA_HEADS = 4
DA_QK = 64
MB_HEADS = 4
MOBA_BLOCK = 256
MOBA_TOPK = 3
D_SSM = 1024
SSM_HEADS = 16
SSM_HEADDIM = 64
SSM_GROUPS = 2
D_STATE = 128
CONV_W = 4
D_XBC = D_SSM + 2 * SSM_GROUPS * D_STATE
N_EXPERTS = 16
N_EXPERT_GROUPS = 4
EXPERTS_PER_GROUP = 4
LN_EPS = 1e-5
RMS_EPS = 1e-6
VMEM_LIMIT = 56 * 1024 * 1024

QA_BLK, KA_BLK, VA_BLK, QB_BLK, KB_BLK, VB_BLK = 0, 4, 8, 12, 16, 20
ZC_COL, XS_COL, BC_COL, DT_COL = 3072, 4096, 5120, 5632
D_IN_PROJ = 5648
D_IN_PAD = 5760

NT_DIMS = (((1,), (1,)), ((), ()))
TN_DIMS = (((0,), (0,)), ((), ()))


def _cparams(n_axes):
    return pltpu.CompilerParams(dimension_semantics=("arbitrary",) * n_axes, vmem_limit_bytes=VMEM_LIMIT)


def _mm_kernel(x_ref, w_ref, o_ref):
    o_ref[...] = jnp.dot(x_ref[...], w_ref[...], preferred_element_type=F32).astype(o_ref.dtype)


def _matmul(x, w, layer, tm, tn, out_dtype=F32):
    m, k = x.shape
    n = w.shape[2]
    return pl.pallas_call(
        _mm_kernel,
        grid=(n // tn, m // tm),
        in_specs=[pl.BlockSpec((tm, k), lambda j, i: (i, 0)), pl.BlockSpec((None, k, tn), lambda j, i: (layer, 0, j))],
        out_specs=pl.BlockSpec((tm, tn), lambda j, i: (i, j)),
        out_shape=jax.ShapeDtypeStruct((m, n), out_dtype),
        compiler_params=_cparams(2),
        name="in_proj",
    )(x, w)


POS_SPLIT = 64
ONEHOT_LANE0 = 8
LOG2E = math.log2(math.e)
MASK_BIAS = -1e30


ATT_TILE = 256
N_ATT_HEADS = DA_HEADS + MB_HEADS
VT_ROWS = HEAD_DIM + 16
HEADS_PER_PHASE = 4


def _kv_prep_kernel(k_ref, v_ref, ka_ref, vt_ref, km_ref, ko_a_ref, vo_a_ref, ko_b_ref, vo_b_ref, *, ck):
    typ = pl.program_id(1)
    c = pl.program_id(2)
    tiles = ck // ATT_TILE
    pos = c * ck + lax.broadcasted_iota(jnp.int32, (ck, LANES), 0)
    lane = lax.broadcasted_iota(jnp.int32, (ck, LANES), 1)
    kind = lane % 3
    feat = jnp.where(lane < 6, jnp.where(kind == 0, pos // POS_SPLIT, jnp.where(kind == 1, pos % POS_SPLIT, 1)), 0)
    feat = jnp.where(lane - ONEHOT_LANE0 == pos // MOBA_BLOCK, 1, feat).astype(F32).astype(BF16)
    for hd in range(DA_HEADS):
        lanes = slice(hd * HEAD_DIM, (hd + 1) * HEAD_DIM)
        ka_ref[hd, :, :HEAD_DIM] = k_ref[:, lanes].astype(BF16)
        ka_ref[hd, :, HEAD_DIM:] = feat
        for j in range(tiles):
            rows = slice(j * ATT_TILE, (j + 1) * ATT_TILE)
            sum_rows = lax.broadcasted_iota(jnp.int32, (VT_ROWS - HEAD_DIM, ATT_TILE), 0) == 0
            vt_ref[hd, j] = jnp.concatenate([v_ref[rows, lanes].T, sum_rows.astype(F32)], axis=0).astype(BF16)
            km_ref[hd, pl.ds(c * tiles + j, 1), :] = jnp.mean(k_ref[rows, lanes], axis=0, keepdims=True)
    for t, (ko_ref, vo_ref) in enumerate(((ko_a_ref, vo_a_ref), (ko_b_ref, vo_b_ref))):
        @pl.when(typ == t)
        def _():
            for hd in range(DA_HEADS):
                lanes = slice(hd * HEAD_DIM, (hd + 1) * HEAD_DIM)
                ko_ref[:, hd, :] = k_ref[:, lanes]
                vo_ref[:, hd, :] = v_ref[:, lanes]


def _kv_prep(z, *, batch, seq, ck=1024):
    nc = seq // ck
    nq = seq // ATT_TILE
    tiles = ck // ATT_TILE
    width = DA_HEADS * HEAD_DIM
    k_blk = lambda t: (KA_BLK + (KB_BLK - KA_BLK) * t) // DA_HEADS
    v_blk = lambda t: (VA_BLK + (VB_BLK - VA_BLK) * t) // DA_HEADS
    new_a = pl.BlockSpec((None, ck, DA_HEADS, HEAD_DIM), lambda b, t, c: (b, jnp.where(t == 0, c, nc - 1), 0, 0))
    new_b = pl.BlockSpec((None, ck, DA_HEADS, HEAD_DIM), lambda b, t, c: (b, jnp.where(t == 1, c, 0), 0, 0))
    return pl.pallas_call(
        functools.partial(_kv_prep_kernel, ck=ck),
        grid=(batch, 2, nc),
        in_specs=[
            pl.BlockSpec((ck, width), lambda b, t, c: (b * nc + c, k_blk(t))),
            pl.BlockSpec((ck, width), lambda b, t, c: (b * nc + c, v_blk(t))),
        ],
        out_specs=[
            pl.BlockSpec((None, DA_HEADS, ck, 2 * HEAD_DIM), lambda b, t, c: (b, t, c, 0)),
            pl.BlockSpec((None, DA_HEADS, tiles, VT_ROWS, ATT_TILE), lambda b, t, c: (b, t, c, 0, 0)),
            pl.BlockSpec((None, DA_HEADS, nq, HEAD_DIM), lambda b, t, c: (b, t, 0, 0)),
            new_a, new_a, new_b, new_b,
        ],
        out_shape=[
            jax.ShapeDtypeStruct((batch, N_ATT_HEADS, seq, 2 * HEAD_DIM), BF16),
            jax.ShapeDtypeStruct((batch, N_ATT_HEADS, seq // ATT_TILE, VT_ROWS, ATT_TILE), BF16),
            jax.ShapeDtypeStruct((batch, N_ATT_HEADS, seq // ATT_TILE, HEAD_DIM), F32),
        ] + [jax.ShapeDtypeStruct((batch, seq, DA_HEADS, HEAD_DIM), F32)] * 4,
        compiler_params=_cparams(3),
        name="kv_prep",
    )(z, z)


def _query_feature_rows(slope, q0, n_cols):
    row = lax.broadcasted_iota(jnp.int32, (8, n_cols), 0)
    c = slope * LOG2E
    kind = row % 3
    val = jnp.where(kind == 0, c * POS_SPLIT, jnp.where(kind == 1, c, -(c * q0.astype(F32))))
    head = val.astype(BF16).astype(F32)
    return jnp.where(row < 3, head, jnp.where(row < 6, val - head, 0.0))


def _softmax_t_step(n_heads, kj, ka_ref, vt_ref, qa_sc, m_sc, acc_sc, bias=None):
    t = ATT_TILE
    rows = pl.ds(pl.multiple_of(kj * t, t), t)
    for h0 in range(0, n_heads, HEADS_PER_PHASE):
        heads = range(h0, min(h0 + HEADS_PER_PHASE, n_heads))
        scores = {h: jnp.dot(ka_ref[h, rows, :], qa_sc[h], preferred_element_type=F32) for h in heads}
        probs = {}
        for h in heads:
            s = scores[h] if bias is None else scores[h] + bias
            m = m_sc[h]
            m_new = jnp.maximum(m, jnp.max(s, axis=0, keepdims=True))
            alpha = jnp.exp2(m - m_new)
            p = jnp.exp2(s - m_new)
            m_sc[h] = m_new
            probs[h] = (alpha, p.astype(BF16))
        for h in heads:
            alpha, p = probs[h]
            acc_sc[h] = alpha * acc_sc[h] + jnp.dot(vt_ref[h, kj], p, preferred_element_type=F32)


def _softmax_t_run(n_heads, own, ka_ref, vt_ref, mask_ref, qa_sc, m_sc, acc_sc):
    m_sc[...] = jnp.full(m_sc.shape, -jnp.inf, F32)
    acc_sc[...] = jnp.zeros(acc_sc.shape, F32)
    _softmax_t_step(n_heads, own, ka_ref, vt_ref, qa_sc, m_sc, acc_sc, bias=mask_ref[...])

    def body(kj, carry):
        _softmax_t_step(n_heads, kj, ka_ref, vt_ref, qa_sc, m_sc, acc_sc)
        return carry

    lax.fori_loop(0, own, body, 0)


def _causal_mask_bias(n_maps):
    t = ATT_TILE
    k_i = lax.broadcasted_iota(jnp.int32, (t, n_maps * t), 0)
    q_i = lax.broadcasted_iota(jnp.int32, (t, n_maps * t), 1) % t
    return jnp.where(k_i <= q_i, 0.0, -jnp.inf).astype(F32)


def _diff_prompt_kernel(slopes_ref, dl_ref, q_ref, ka_ref, vt_ref, mask_ref, nw_ref, o_ref,
                        qa_sc, m_sc, acc_sc, *, lam_init):
    t = ATT_TILE
    qi = pl.program_id(1)
    q0 = qi * t
    n_cols = 2 * t
    for h in range(DA_HEADS):
        qt = q_ref[:, h * HEAD_DIM:(h + 1) * HEAD_DIM].T * (DA_QK ** -0.5 * LOG2E)
        row = lax.broadcasted_iota(jnp.int32, qt.shape, 0)
        q_top = jnp.concatenate([jnp.where(row < DA_QK, qt, 0.0), jnp.where(row >= DA_QK, qt, 0.0)], axis=1)
        q_feat = jnp.concatenate([_query_feature_rows(slopes_ref[h], q0, n_cols),
                                  jnp.zeros((LANES - 8, n_cols), F32)], axis=0)
        qa_sc[h] = jnp.concatenate([q_top, q_feat], axis=0).astype(BF16)

    _softmax_t_run(DA_HEADS, qi, ka_ref, vt_ref, mask_ref, qa_sc, m_sc, acc_sc)

    dl = dl_ref[...]
    lam = (jnp.exp(jnp.sum(dl[0:1] * dl[1:2], axis=-1, keepdims=True))
           - jnp.exp(jnp.sum(dl[2:3] * dl[3:4], axis=-1, keepdims=True)) + lam_init)
    for h in range(DA_HEADS):
        o2 = acc_sc[h, :HEAD_DIM, :] / acc_sc[h, HEAD_DIM:HEAD_DIM + 1, :]
        ot = o2[:, :t] - lam * o2[:, t:]
        var = jnp.mean(ot * ot, axis=0, keepdims=True)
        o = (ot * lax.rsqrt(var + RMS_EPS)).T * nw_ref[...] * (1.0 - lam_init)
        o_ref[:, h * HEAD_DIM:(h + 1) * HEAD_DIM] = o.astype(o_ref.dtype)


def _diff_prompt(z, ka, vt, slopes, dl, nw, *, batch, seq, lam_init):
    t = ATT_TILE
    nq = seq // t
    width = DA_HEADS * HEAD_DIM
    n_cols = 2 * t
    kern = functools.partial(_diff_prompt_kernel, lam_init=lam_init)
    return pl.pallas_call(
        kern,
        grid=(batch, nq),
        in_specs=[
            pl.BlockSpec(memory_space=pltpu.SMEM),
            pl.BlockSpec((4, DA_QK), lambda b, i: (0, 0)),
            pl.BlockSpec((t, width), lambda b, i: (b * nq + i, QA_BLK // DA_HEADS)),
            pl.BlockSpec((None, DA_HEADS, seq, 2 * HEAD_DIM), lambda b, i: (b, 0, 0, 0)),
            pl.BlockSpec((None, DA_HEADS, nq, VT_ROWS, t), lambda b, i: (b, 0, 0, 0, 0)),
            pl.BlockSpec((t, n_cols), lambda b, i: (0, 0)),
            pl.BlockSpec((1, HEAD_DIM), lambda b, i: (0, 0)),
        ],
        out_specs=pl.BlockSpec((t, width), lambda b, i: (b * nq + i, 0)),
        out_shape=jax.ShapeDtypeStruct((batch * seq, width), BF16),
        scratch_shapes=[
            pltpu.VMEM((DA_HEADS, 2 * HEAD_DIM, n_cols), BF16),
            pltpu.VMEM((DA_HEADS, 1, n_cols), F32), pltpu.VMEM((DA_HEADS, VT_ROWS, n_cols), F32),
        ],
        compiler_params=_cparams(2),
        name="diff_prompt",
    )(slopes, dl, z, ka, vt, _causal_mask_bias(2), nw)


def _top3_mask_rows(gate_t, n_valid, own):
    nb = gate_t.shape[0]
    row = lax.broadcasted_iota(jnp.int32, gate_t.shape, 0)
    g = jnp.where(row < n_valid, gate_t, -jnp.inf)
    sel = row == own
    for _ in range(MOBA_TOPK):
        mx = jnp.max(g, axis=0, keepdims=True)
        idx = jnp.min(jnp.where(g == mx, row, nb), axis=0, keepdims=True)
        pick = (row == idx) & (mx > -jnp.inf)
        sel = sel | pick
        g = jnp.where(pick, -jnp.inf, g)
    return jnp.where(sel, 0.0, MASK_BIAS)


def _moba_prompt_kernel(slopes_ref, q_ref, ka_ref, vt_ref, km_ref, mask_ref, o_ref,
                        qa_sc, m_sc, acc_sc, *, nb):
    blk = MOBA_BLOCK
    own = pl.program_id(1)
    q0 = own * blk
    for h in range(MB_HEADS):
        qt = q_ref[:, h * HEAD_DIM:(h + 1) * HEAD_DIM].T
        gate_t = jnp.dot(km_ref[h], qt, preferred_element_type=F32, precision=lax.Precision.HIGHEST)
        q_feat = jnp.concatenate([_query_feature_rows(slopes_ref[DA_HEADS + h], q0, blk),
                                  _top3_mask_rows(gate_t, own, own),
                                  jnp.zeros((LANES - ONEHOT_LANE0 - nb, blk), F32)], axis=0)
        qa_sc[h] = jnp.concatenate([qt * (HEAD_DIM ** -0.5 * LOG2E), q_feat], axis=0).astype(BF16)

    _softmax_t_run(MB_HEADS, own, ka_ref, vt_ref, mask_ref, qa_sc, m_sc, acc_sc)

    for h in range(MB_HEADS):
        o = acc_sc[h, :HEAD_DIM, :] / acc_sc[h, HEAD_DIM:HEAD_DIM + 1, :]
        o_ref[:, h * HEAD_DIM:(h + 1) * HEAD_DIM] = o.T.astype(o_ref.dtype)


def _moba_prompt(z, ka, vt, kmean, slopes, *, batch, seq):
    blk = MOBA_BLOCK
    assert blk == ATT_TILE
    nb = seq // blk
    assert nb <= LANES - ONEHOT_LANE0 and nb % 8 == 0
    width = MB_HEADS * HEAD_DIM
    kern = functools.partial(_moba_prompt_kernel, nb=nb)
    return pl.pallas_call(
        kern,
        grid=(batch, nb),
        in_specs=[
            pl.BlockSpec(memory_space=pltpu.SMEM),
            pl.BlockSpec((blk, width), lambda b, i: (b * nb + i, QB_BLK // MB_HEADS)),
            pl.BlockSpec((None, MB_HEADS, seq, 2 * HEAD_DIM), lambda b, i: (b, 1, 0, 0)),
            pl.BlockSpec((None, MB_HEADS, nb, VT_ROWS, blk), lambda b, i: (b, 1, 0, 0, 0)),
            pl.BlockSpec((None, MB_HEADS, nb, HEAD_DIM), lambda b, i: (b, 1, 0, 0)),
            pl.BlockSpec((blk, blk), lambda b, i: (0, 0)),
        ],
        out_specs=pl.BlockSpec((blk, width), lambda b, i: (b * nb + i, 0)),
        out_shape=jax.ShapeDtypeStruct((batch * seq, width), BF16),
        scratch_shapes=[
            pltpu.VMEM((MB_HEADS, 2 * HEAD_DIM, blk), BF16),
            pltpu.VMEM((MB_HEADS, 1, blk), F32), pltpu.VMEM((MB_HEADS, VT_ROWS, blk), F32),
        ],
        compiler_params=_cparams(2),
        name="moba_prompt",
    )(slopes, z, ka, vt, kmean, _causal_mask_bias(1))


def _softplus(x):
    return jnp.maximum(x, 0.0) + jnp.log1p(jnp.exp(-jnp.abs(x)))


def _silu(x):
    return x * jax.nn.sigmoid(x)


def _conv_silu(xp_sc, w, b, t):
    y = b
    for j in range(CONV_W):
        y = y + xp_sc[5 + j:5 + j + t, :] * w[j:j + 1, :]
    return _silu(y)


def _ssd_prompt_kernel(zc_ref, x_ref, bc_ref, dt_ref, cw_ref, cb_ref, dtb_ref, alog_ref, dsk_ref, nw_ref,
                       y_ref, hfin_ref, xpx_sc, xpbc_sc, h_sc, y_sc, *, t):
    c = pl.program_id(1)
    nc = pl.num_programs(1)
    hp = LANES // SSM_HEADDIM
    hpg = SSM_HEADS // SSM_GROUPS

    @pl.when(c == 0)
    def _():
        xpx_sc[0:8, :] = jnp.zeros((8, D_SSM), F32)
        xpbc_sc[0:8, :] = jnp.zeros((8, D_XBC - D_SSM), F32)
        h_sc[...] = jnp.zeros(h_sc.shape, F32)

    xpx_sc[8:8 + t, :] = x_ref[...]
    xpbc_sc[8:8 + t, :] = bc_ref[...]
    cw = cw_ref[...]
    cb = cb_ref[...]
    xs = _conv_silu(xpx_sc, cw[:, :D_SSM], cb[:, :D_SSM], t)
    bc = _conv_silu(xpbc_sc, cw[:, D_SSM:], cb[:, D_SSM:], t)
    xpx_sc[5:8, :] = xpx_sc[t + 5:t + 8, :]
    xpbc_sc[5:8, :] = xpbc_sc[t + 5:t + 8, :]

    dt = _softplus(dt_ref[...] + dtb_ref[...])
    da = dt * (-jnp.exp(alog_ref[...]))
    r_i = lax.broadcasted_iota(jnp.int32, (t, t), 0)
    c_i = lax.broadcasted_iota(jnp.int32, (t, t), 1)
    tril = c_i <= r_i
    acs = jnp.dot(tril.astype(F32), da, preferred_element_type=F32, precision=lax.Precision.HIGHEST)
    acs_t = acs.T
    lane = lax.broadcasted_iota(jnp.int32, (t, LANES), 1)
    sub = lax.broadcasted_iota(jnp.int32, (LANES, LANES), 0)
    lo_lane = lane < SSM_HEADDIM
    lo_sub = sub < SSM_HEADDIM

    for g in range(SSM_GROUPS):
        bg = bc[:, g * D_STATE:(g + 1) * D_STATE]
        cg = bc[:, (SSM_GROUPS + g) * D_STATE:(SSM_GROUPS + g + 1) * D_STATE]
        bgb = bg.astype(BF16)
        cgb = cg.astype(BF16)
        gmat = lax.dot_general(cgb, bgb, NT_DIMS, preferred_element_type=F32)
        for pr in range(hpg // hp):
            ha = g * hpg + pr * hp
            cb0 = ha * SSM_HEADDIM
            x2 = xs[:, cb0:cb0 + LANES]
            dt2 = jnp.where(lo_lane, dt[:, ha:ha + 1], dt[:, ha + 1:ha + 2])
            xdt = x2 * dt2
            ydiag = jnp.zeros((t, LANES), F32)
            for u in range(hp):
                hh = ha + u
                seg = acs[:, hh:hh + 1] - acs_t[hh:hh + 1, :]
                lmat = jnp.exp(jnp.where(tril, seg, -jnp.inf))
                xu = jnp.where(lo_lane if u == 0 else ~lo_lane, xdt, 0.0)
                ydiag = ydiag + jnp.dot((gmat * lmat).astype(BF16), xu.astype(BF16), preferred_element_type=F32)
            h2 = h_sc[ha:ha + hp].reshape(LANES, D_STATE)
            yoff = lax.dot_general(cgb, h2.astype(BF16), NT_DIMS, preferred_element_type=F32)
            e2 = jnp.where(lo_lane, jnp.exp(acs[:, ha:ha + 1]), jnp.exp(acs[:, ha + 1:ha + 2]))
            y = ydiag + e2 * yoff + dsk_ref[:, cb0:cb0 + LANES] * x2
            y_sc[:, cb0:cb0 + LANES] = y * _silu(zc_ref[:, cb0:cb0 + LANES])
            last = acs[t - 1:t, :]
            te2 = jnp.where(lo_lane, jnp.exp(last[:, ha:ha + 1] - acs[:, ha:ha + 1]),
                            jnp.exp(last[:, ha + 1:ha + 2] - acs[:, ha + 1:ha + 2]))
            upd = lax.dot_general((xdt * te2).astype(BF16), bgb, TN_DIMS, preferred_element_type=F32)
            dec = jnp.where(lo_sub, jnp.exp(last[:, ha:ha + 1]), jnp.exp(last[:, ha + 1:ha + 2]))
            h_sc[ha:ha + hp] = (dec * h2 + upd).reshape(hp, SSM_HEADDIM, D_STATE)

    gw = D_SSM // SSM_GROUPS
    for g in range(SSM_GROUPS):
        yg = y_sc[:, g * gw:(g + 1) * gw]
        var = jnp.mean(yg * yg, axis=-1, keepdims=True)
        y_ref[:, g * gw:(g + 1) * gw] = (yg * lax.rsqrt(var + RMS_EPS) * nw_ref[:, g * gw:(g + 1) * gw]).astype(y_ref.dtype)

    @pl.when(c == nc - 1)
    def _():
        hfin_ref[0] = h_sc[...]


def _ssd_prompt(z, cw, cb, dtb, alog, dsk, nw, *, batch, seq, t=128):
    nc = seq // t
    kern = functools.partial(_ssd_prompt_kernel, t=t)
    full = lambda shape: pl.BlockSpec(shape, lambda b, c: (0,) * len(shape))
    return pl.pallas_call(
        kern,
        grid=(batch, nc),
        in_specs=[
            pl.BlockSpec((t, D_SSM), lambda b, c: (b * nc + c, ZC_COL // D_SSM)),
            pl.BlockSpec((t, D_SSM), lambda b, c: (b * nc + c, XS_COL // D_SSM)),
            pl.BlockSpec((t, 512), lambda b, c: (b * nc + c, BC_COL // 512)),
            pl.BlockSpec((t, LANES), lambda b, c: (b * nc + c, DT_COL // LANES)),
            full((CONV_W, D_XBC)), full((1, D_XBC)), full((1, LANES)), full((1, LANES)),
            full((1, D_SSM)), full((1, D_SSM)),
        ],
        out_specs=[
            pl.BlockSpec((t, D_SSM), lambda b, c: (b * nc + c, 0)),
            pl.BlockSpec((1, SSM_HEADS, SSM_HEADDIM, D_STATE), lambda b, c: (b, 0, 0, 0)),
        ],
        out_shape=[
            jax.ShapeDtypeStruct((batch * seq, D_SSM), BF16),
            jax.ShapeDtypeStruct((batch, SSM_HEADS, SSM_HEADDIM, D_STATE), F32),
        ],
        scratch_shapes=[
            pltpu.VMEM((t + 8, D_SSM), F32), pltpu.VMEM((t + 8, D_XBC - D_SSM), F32),
            pltpu.VMEM((SSM_HEADS, SSM_HEADDIM, D_STATE), F32), pltpu.VMEM((t, D_SSM), F32),
        ],
        compiler_params=_cparams(2),
        name="ssd_prompt",
    )(z, z, z, z, cw, cb, dtb, alog, dsk, nw)


NEW_ROWS = 16


SEQS_PER_STEP = 2


def _attn_decode_kernel(*refs, moba, n_pages, page, t_new, lam_init, spb):
    pt_ref = refs[0]
    del pt_ref
    slopes_ref, dl_ref, nw_ref, q_ref, kn_ref, vn_ref = refs[1:7]
    n_pg = spb * n_pages
    kp_refs = refs[7:7 + n_pg]
    vp_refs = refs[7 + n_pg:7 + 2 * n_pg]
    o_ref = refs[7 + 2 * n_pg]
    scratch = refs[8 + 2 * n_pg:]
    kb_scs, vb_scs, new_sc = scratch[:spb], scratch[spb:2 * spb], scratch[2 * spb]
    past = n_pages * page
    width = DA_HEADS * HEAD_DIM
    cols_per_head = t_new if moba else 2 * t_new
    lanes_per_col_grp = HEAD_DIM if moba else DA_QK
    scale = HEAD_DIM ** -0.5 if moba else DA_QK ** -0.5
    seqs = range(spb)
    coli = lax.broadcasted_iota(jnp.int32, (1, LANES), 1)
    slope = slopes_ref[...]

    ksums = []
    for s in seqs:
        ksum = []
        for p in range(n_pages):
            ksum_p = []
            kp, vp = kp_refs[s * n_pages + p], vp_refs[s * n_pages + p]
            for hd in range(DA_HEADS):
                lo = hd * HEAD_DIM
                kh = kp[pl.ds(hd, page, stride=DA_HEADS), :]
                kb_scs[s][p * page:(p + 1) * page, lo:lo + HEAD_DIM] = kh.astype(BF16)
                vb_scs[s][p * page:(p + 1) * page, lo:lo + HEAD_DIM] = vp[pl.ds(hd, page, stride=DA_HEADS), :].astype(BF16)
                if moba:
                    ksum_p.append(jnp.sum(kh, axis=0, keepdims=True))
            if moba:
                ksum.append(jnp.concatenate(ksum_p, axis=1))
        ksums.append(ksum)
        for j, (src, dst) in enumerate(((kn_ref, kb_scs[s]), (vn_ref, vb_scs[s]))):
            new_sc[s, j] = jnp.zeros(new_sc.shape[2:], F32)
            new_sc[s, j, 0:t_new, :] = src[s]
            dst[past:past + NEW_ROWS, :] = new_sc[s, j].astype(BF16)

    qs, scores = [], []
    for s in seqs:
        q = q_ref[s]
        qt = jnp.concatenate([q] * (LANES // t_new), axis=0)
        r_i = lax.broadcasted_iota(jnp.int32, qt.shape, 0)
        l_i = lax.broadcasted_iota(jnp.int32, qt.shape, 1)
        qmat = jnp.where(l_i // lanes_per_col_grp == r_i // t_new, qt, 0.0)
        qs.append(q)
        scores.append(lax.dot_general(kb_scs[s][...], (qmat * scale).astype(BF16), NT_DIMS,
                                      preferred_element_type=F32))

    probs = []
    for s in seqs:
        rowp = lax.broadcasted_iota(jnp.int32, (past, LANES), 0)
        s_past = scores[s][:past] + slope * (rowp - past).astype(F32)
        rown = lax.broadcasted_iota(jnp.int32, (NEW_ROWS, LANES), 0)
        s_new = scores[s][past:] + slope * rown.astype(F32)
        s_new = jnp.where(rown <= coli % t_new, s_new, -jnp.inf)

        if moba:
            nbp = past // MOBA_BLOCK
            ppb = MOBA_BLOCK // page
            gate = jnp.zeros((nbp, LANES), F32)
            kmean = jnp.concatenate(
                [sum(ksums[s][n * ppb:(n + 1) * ppb]) for n in range(nbp)], axis=0) * (1.0 / MOBA_BLOCK)
            for i in range(MB_HEADS * t_new):
                hd, tok = i // t_new, i % t_new
                lo = hd * HEAD_DIM
                g_i = jnp.sum(kmean[:, lo:lo + HEAD_DIM] * qs[s][tok:tok + 1, lo:lo + HEAD_DIM], axis=-1,
                              keepdims=True)
                gate = jnp.where(coli == i, g_i, gate)
            rowb = lax.broadcasted_iota(jnp.int32, gate.shape, 0)
            sel = jnp.zeros(gate.shape, jnp.bool_)
            g = gate
            for _ in range(MOBA_TOPK):
                mx = jnp.max(g, axis=0, keepdims=True)
                idx = jnp.min(jnp.where(g == mx, rowb, nbp), axis=0, keepdims=True)
                pick = (rowb == idx) & (mx > -jnp.inf)
                sel = sel | pick
                g = jnp.where(pick, -jnp.inf, g)
            sel_bias = jnp.where(sel, 0.0, -jnp.inf)
            s_past = (s_past.reshape(nbp, MOBA_BLOCK, LANES) + sel_bias[:, None, :]).reshape(past, LANES)

        m = jnp.maximum(jnp.max(s_past, axis=0, keepdims=True), jnp.max(s_new, axis=0, keepdims=True))
        p_past = jnp.exp(s_past - m)
        p_new = jnp.exp(s_new - m)
        l = jnp.sum(p_past, axis=0, keepdims=True) + jnp.sum(p_new, axis=0, keepdims=True)
        probs.append((jnp.concatenate([p_past, p_new], axis=0).astype(BF16), l))

    outs_t = [lax.dot_general(vb_scs[s][...], probs[s][0], TN_DIMS, preferred_element_type=F32)
              for s in seqs]

    for s in seqs:
        o = outs_t[s].T
        l_col = jnp.broadcast_to(probs[s][1], (LANES, LANES)).T
        o = o / jnp.concatenate([l_col] * (width // LANES), axis=1)
        outs = []
        for hd in range(DA_HEADS):
            lo = hd * HEAD_DIM
            r0 = hd * cols_per_head
            if moba:
                outs.append(o[r0:r0 + t_new, lo:lo + HEAD_DIM])
            else:
                dl = dl_ref[...]
                lam = (jnp.exp(jnp.sum(dl[0:1] * dl[1:2], axis=-1, keepdims=True))
                       - jnp.exp(jnp.sum(dl[2:3] * dl[3:4], axis=-1, keepdims=True)) + lam_init)
                oh = o[r0:r0 + t_new, lo:lo + HEAD_DIM] - lam * o[r0 + t_new:r0 + 2 * t_new, lo:lo + HEAD_DIM]
                var = jnp.mean(oh * oh, axis=-1, keepdims=True)
                outs.append(oh * lax.rsqrt(var + RMS_EPS) * nw_ref[...] * (1.0 - lam_init))
        o_ref[s] = jnp.concatenate(outs, axis=1).astype(o_ref.dtype)


def _attn_decode(zs, cache_k, cache_v, page_table, slopes_cols, dl, nw, *, layer, moba, lam_init, spb=SEQS_PER_STEP):
    n_seq, t_new, _ = zs.shape
    page = cache_k.shape[2] // DA_HEADS
    n_pages = page_table.shape[1]
    width = DA_HEADS * HEAD_DIM
    qblk, kblk, vblk = (QB_BLK, KB_BLK, VB_BLK) if moba else (QA_BLK, KA_BLK, VA_BLK)
    past = n_pages * page
    spb = spb if n_seq % spb == 0 else 1

    def page_spec(s, p):
        return pl.BlockSpec((None, None, page * DA_HEADS, HEAD_DIM), lambda b, pt: (layer, pt[b * spb + s, p], 0, 0))

    def z_spec(blk):
        return pl.BlockSpec((spb, t_new, width), lambda b, pt: (b, 0, blk // DA_HEADS))

    kern = functools.partial(_attn_decode_kernel, moba=moba, n_pages=n_pages, page=page, t_new=t_new,
                             lam_init=lam_init, spb=spb)
    page_specs = [page_spec(s, p) for s in range(spb) for p in range(n_pages)]
    grid_spec = pltpu.PrefetchScalarGridSpec(
        num_scalar_prefetch=1,
        grid=(n_seq // spb,),
        in_specs=[
            pl.BlockSpec((1, LANES), lambda b, pt: (0, 0)),
            pl.BlockSpec((4, DA_QK), lambda b, pt: (0, 0)),
            pl.BlockSpec((1, HEAD_DIM), lambda b, pt: (0, 0)),
            z_spec(qblk), z_spec(kblk), z_spec(vblk),
        ] + page_specs * 2,
        out_specs=pl.BlockSpec((spb, t_new, width), lambda b, pt: (b, 0, 0)),
        scratch_shapes=[pltpu.VMEM((past + NEW_ROWS, width), BF16)] * (2 * spb)
        + [pltpu.VMEM((spb, 2, NEW_ROWS, width), F32)],
    )
    return pl.pallas_call(
        kern,
        grid_spec=grid_spec,
        out_shape=jax.ShapeDtypeStruct((n_seq, t_new, width), BF16),
        compiler_params=_cparams(1),
        name="moba_decode" if moba else "diff_decode",
    )(page_table, slopes_cols, dl, nw, zs, zs, zs, *([cache_k] * len(page_specs)), *([cache_v] * len(page_specs)))


def _ssd_decode_kernel(zc_ref, xs_ref, bc_ref, dtx_ref, cst_ref, h0_ref, cw_ref, cb_ref, dtb_ref, alog_ref,
                       dsk_ref, nw_ref, y_ref, hout_ref, cout_ref, xp_sc, *, t_new, spb):
    for s in range(spb):
        _ssd_decode_one(zc_ref.at[s], xs_ref.at[s], bc_ref.at[s], dtx_ref.at[s], cst_ref.at[s], h0_ref.at[s],
                        cw_ref, cb_ref, dtb_ref, alog_ref, dsk_ref, nw_ref, y_ref.at[s], hout_ref.at[s],
                        cout_ref.at[s], xp_sc.at[s], t_new=t_new)


def _ssd_decode_one(zc_ref, xs_ref, bc_ref, dtx_ref, cst_ref, h0_ref, cw_ref, cb_ref, dtb_ref, alog_ref,
                    dsk_ref, nw_ref, y_ref, hout_ref, cout_ref, xp_sc, *, t_new):
    hpg = SSM_HEADS // SSM_GROUPS
    gw = D_SSM // SSM_GROUPS
    xp_sc[0:CONV_W - 1, :] = cst_ref[...]
    xp_sc[CONV_W - 1:CONV_W - 1 + t_new, :D_SSM] = xs_ref[...]
    xp_sc[CONV_W - 1:CONV_W - 1 + t_new, D_SSM:] = bc_ref[...]
    cout_ref[...] = xp_sc[t_new:t_new + CONV_W - 1, :]
    y = cb_ref[...]
    for j in range(CONV_W):
        y = y + xp_sc[j:j + t_new, :] * cw_ref[j:j + 1, :]
    xbc = _silu(y)
    xs = xbc[:, :D_SSM]
    dt = _softplus(dtx_ref[...] + dtb_ref[...])
    da = dt * (-jnp.exp(alog_ref[...]))
    acs = [da[0:1]]
    for s in range(1, t_new):
        acs.append(acs[-1] + da[s:s + 1])
    xdt = xs * dt
    last = acs[-1]
    wx = jnp.concatenate([xdt[s:s + 1] * jnp.exp(last - acs[s]) for s in range(t_new)], axis=0)
    dec = jnp.exp(last)
    d_hi = dec.astype(BF16).astype(F32)
    d_md = (dec - d_hi).astype(BF16).astype(F32)
    d_lo = (dec - d_hi - d_md).astype(BF16).astype(F32)
    dec3 = jnp.concatenate([d_hi, d_md, d_lo, jnp.zeros((5, D_SSM), F32)], axis=0).astype(BF16)
    ones = jnp.ones((8, D_STATE), BF16)
    zc = zc_ref[...]
    for g in range(SSM_GROUPS):
        sl = slice(g * gw, (g + 1) * gw)
        bg = xbc[:, D_SSM + g * D_STATE:D_SSM + (g + 1) * D_STATE]
        cg = xbc[:, D_SSM + (SSM_GROUPS + g) * D_STATE:D_SSM + (SSM_GROUPS + g + 1) * D_STATE]
        h0 = h0_ref[g * hpg:(g + 1) * hpg].reshape(gw, D_STATE)
        yoff = lax.dot_general(cg.astype(BF16), h0.astype(BF16), NT_DIMS, preferred_element_type=F32)
        rows = []
        for t in range(t_new):
            yt = jnp.exp(acs[t][:, sl]) * yoff[t:t + 1]
            for s in range(t + 1):
                gts = jnp.sum(cg[t:t + 1] * bg[s:s + 1], axis=-1, keepdims=True)
                yt = yt + gts * jnp.exp(acs[t][:, sl] - acs[s][:, sl]) * xdt[s:s + 1, sl]
            rows.append(yt)
        yg = jnp.concatenate(rows, axis=0) + dsk_ref[:, sl] * xs[:, sl]
        yg = yg * _silu(zc[:, sl])
        var = jnp.mean(yg * yg, axis=-1, keepdims=True)
        y_ref[:, sl] = (yg * lax.rsqrt(var + RMS_EPS) * nw_ref[:, sl]).astype(y_ref.dtype)
        wxg = wx[:, sl]
        wx_hi = wxg.astype(BF16).astype(F32)
        bg_hi = bg.astype(BF16).astype(F32)
        pad_rows = 16 - 3 * t_new
        lhs = jnp.concatenate([wx_hi, wxg - wx_hi, wx_hi, jnp.zeros((pad_rows, gw), F32)], axis=0).astype(BF16)
        rhs = jnp.concatenate([bg_hi, bg_hi, bg - bg_hi, jnp.zeros((pad_rows, D_STATE), F32)], axis=0).astype(BF16)
        upd = lax.dot_general(lhs, rhs, TN_DIMS, preferred_element_type=F32)
        dec_full = lax.dot_general(dec3[:, sl], ones, TN_DIMS, preferred_element_type=F32)
        hout_ref[g * hpg:(g + 1) * hpg] = (dec_full * h0 + upd).reshape(hpg, SSM_HEADDIM, D_STATE)


def _ssd_decode(zs, dtx, state_conv, state_ssm, cw, cb, dtbx, alogx, dskx, nw, *, layer, spb=4):
    n_seq, t_new, _ = zs.shape
    spb = spb if n_seq % spb == 0 else 1
    kern = functools.partial(_ssd_decode_kernel, t_new=t_new, spb=spb)
    full = lambda shape: pl.BlockSpec(shape, lambda b: (0,) * len(shape))
    return pl.pallas_call(
        kern,
        grid=(n_seq // spb,),
        in_specs=[
            pl.BlockSpec((spb, t_new, D_SSM), lambda b: (b, 0, ZC_COL // D_SSM)),
            pl.BlockSpec((spb, t_new, D_SSM), lambda b: (b, 0, XS_COL // D_SSM)),
            pl.BlockSpec((spb, t_new, 512), lambda b: (b, 0, BC_COL // 512)),
            pl.BlockSpec((spb, t_new, D_SSM), lambda b: (b, 0, 0)),
            pl.BlockSpec((None, spb, CONV_W - 1, D_XBC), lambda b: (layer, b, 0, 0)),
            pl.BlockSpec((None, spb, SSM_HEADS, SSM_HEADDIM, D_STATE), lambda b: (layer, b, 0, 0, 0)),
            full((CONV_W, D_XBC)), full((1, D_XBC)), full((1, D_SSM)), full((1, D_SSM)),
            full((1, D_SSM)), full((1, D_SSM)),
        ],
        out_specs=[
            pl.BlockSpec((spb, t_new, D_SSM), lambda b: (b, 0, 0)),
            pl.BlockSpec((spb, SSM_HEADS, SSM_HEADDIM, D_STATE), lambda b: (b, 0, 0, 0)),
            pl.BlockSpec((spb, CONV_W - 1, D_XBC), lambda b: (b, 0, 0)),
        ],
        out_shape=[
            jax.ShapeDtypeStruct((n_seq, t_new, D_SSM), BF16),
            jax.ShapeDtypeStruct((n_seq, SSM_HEADS, SSM_HEADDIM, D_STATE), F32),
            jax.ShapeDtypeStruct((n_seq, CONV_W - 1, D_XBC), F32),
        ],
        scratch_shapes=[pltpu.VMEM((spb, 8, D_XBC), F32)],
        compiler_params=_cparams(1),
        name="ssd_decode",
    )(zs, zs, zs, dtx, state_conv, state_ssm, cw, cb, dtbx, alogx, dskx, nw)


def _layer_norm(x, w, b):
    mu = jnp.mean(x, axis=-1, keepdims=True)
    xc = x - mu
    var = jnp.mean(xc * xc, axis=-1, keepdims=True)
    return xc * lax.rsqrt(var + LN_EPS) * w + b


def _top2_of(vals, col, n):
    m1 = jnp.max(vals, axis=-1, keepdims=True)
    i1 = jnp.min(jnp.where(vals == m1, col, n), axis=-1, keepdims=True)
    rest = jnp.where(col == i1, -jnp.inf, vals)
    m2 = jnp.max(rest, axis=-1, keepdims=True)
    i2 = jnp.min(jnp.where(rest == m2, col, n), axis=-1, keepdims=True)
    return m1, i1, m2, i2


def _router_gates(logits, b_router):
    scores = jax.nn.sigmoid(logits)
    biased = scores + b_router
    col = lax.broadcasted_iota(jnp.int32, logits.shape, 1)
    grp = col // EXPERTS_PER_GROUP
    best = jnp.zeros((logits.shape[0], 1), jnp.int32)
    best_v = None
    for g in range(N_EXPERT_GROUPS):
        m1, _, m2, _ = _top2_of(jnp.where(grp == g, biased, -jnp.inf), col, N_EXPERTS)
        gs = m1 + m2
        if g == 0:
            best_v = gs
        else:
            upd = gs > best_v
            best = jnp.where(upd, g, best)
            best_v = jnp.where(upd, gs, best_v)
    _, i1, _, i2 = _top2_of(jnp.where(grp == best, biased, -jnp.inf), col, N_EXPERTS)
    w = jnp.where((col == i1) | (col == i2), scores, 0.0)
    return w / jnp.sum(w, axis=-1, keepdims=True)


ROW_CHUNKS = 16


def _store_token_rows(ref, val):
    rows = val.shape[0]
    for c in range(ROW_CHUNKS):
        ref[pl.ds(c, rows, stride=ROW_CHUNKS), :] = val[:, c * LANES:(c + 1) * LANES]


def _load_token_rows(ref, rows):
    return jnp.concatenate([ref[pl.ds(c, rows, stride=ROW_CHUNKS), :] for c in range(ROW_CHUNKS)], axis=1)


def _post1_kernel(x_ref, oa_ref, ob_ref, oc_ref, wa_ref, wb_ref, wc_ref, lw_ref, lb_ref, wr_ref, br_ref,
                  x1r_ref, x1b_ref, g_ref, *, alpha):
    tm = x_ref.shape[0]
    sub = 128
    mixes = []
    for r in range(tm // sub):
        rows = slice(r * sub, (r + 1) * sub)
        mixes.append(jnp.dot(oa_ref[rows, :], wa_ref[...], preferred_element_type=F32)
                     + jnp.dot(ob_ref[rows, :], wb_ref[...], preferred_element_type=F32)
                     + jnp.dot(oc_ref[rows, :], wc_ref[...], preferred_element_type=F32))
    for r in range(tm // sub):
        rows = slice(r * sub, (r + 1) * sub)
        x1 = _layer_norm(alpha * x_ref[rows, :] + mixes[r], lw_ref[...], lb_ref[...])
        _store_token_rows(x1r_ref.at[pl.ds(r * sub * ROW_CHUNKS, sub * ROW_CHUNKS), :], x1)
        x1b_ref[rows, :] = x1.astype(BF16)
        x1_hi = x1.astype(BF16)
        x1_lo = (x1 - x1_hi.astype(F32)).astype(BF16)
        logits = jnp.dot(jnp.concatenate([x1_hi, x1_lo, x1_hi], axis=1), wr_ref[...], preferred_element_type=F32)
        g_ref[rows, :] = _router_gates(logits, br_ref[...])


def _router_weight_passes(w_router):
    wr_hi = w_router.astype(BF16)
    wr_lo = (w_router - wr_hi.astype(F32)).astype(BF16)
    return jnp.concatenate([wr_hi, wr_hi, wr_lo], axis=0)


def _post1(x, oa, ob, oc, w_out_b, lw, lb, w_router3, b_router, *, layer, alpha, tm=512):
    n, d = x.shape
    assert d == ROW_CHUNKS * LANES
    full = lambda shape: pl.BlockSpec(shape, lambda i: (0,) * len(shape))
    row = lambda width: pl.BlockSpec((tm, width), lambda i: (i, 0))
    per_layer = lambda shape: pl.BlockSpec((None,) + shape, lambda i: (layer,) + (0,) * len(shape))
    wa, wb = DA_HEADS * HEAD_DIM, MB_HEADS * HEAD_DIM
    return pl.pallas_call(
        functools.partial(_post1_kernel, alpha=alpha),
        grid=(n // tm,),
        in_specs=[
            row(d), row(wa), row(wb), row(D_SSM),
            pl.BlockSpec((None, wa, d), lambda i: (layer, 0, 0)), pl.BlockSpec((None, wb, d), lambda i: (layer, 1, 0)),
            pl.BlockSpec((None, D_SSM, d), lambda i: (layer, 1, 0)),
            per_layer((1, d)), per_layer((1, d)), full((3 * d, N_EXPERTS)), full((1, N_EXPERTS)),
        ],
        out_specs=[pl.BlockSpec((tm * ROW_CHUNKS, LANES), lambda i: (i, 0)), row(d), row(N_EXPERTS)],
        out_shape=[jax.ShapeDtypeStruct((n * ROW_CHUNKS, LANES), F32), jax.ShapeDtypeStruct((n, d), BF16),
                   jax.ShapeDtypeStruct((n, N_EXPERTS), F32)],
        compiler_params=_cparams(1),
        name="out_proj_ln1_router",
    )(x, oa, ob, oc, w_out_b, w_out_b, w_out_b, lw, lb, w_router3, b_router)


MOE_TILE = 256
EXPERTS_PER_STEP = 2


def _moe_routing(gates, tile):
    n = gates.shape[0]
    n_tiles = n // tile + N_EXPERT_GROUPS
    n_slots = n_tiles * tile
    expert_ids = jnp.arange(N_EXPERTS, dtype=jnp.int32)[None, :]
    gid = jnp.min(jnp.where(gates > 0, expert_ids, N_EXPERTS), axis=1) // EXPERTS_PER_GROUP
    onehot = (gid[:, None] == jnp.arange(N_EXPERT_GROUPS, dtype=jnp.int32)[None, :]).astype(jnp.int32)
    rank = jnp.sum((jnp.cumsum(onehot, axis=0) - onehot) * onehot, axis=1)
    counts = jnp.sum(onehot, axis=0)
    padded = (counts + tile - 1) // tile * tile
    seg_end = jnp.cumsum(padded)
    slot = (seg_end - padded)[gid] + rank
    tok = jnp.arange(n, dtype=jnp.int32)
    slot_ids = jnp.arange(n_slots + tile, dtype=jnp.int32)
    slot_tok = jnp.full((n_slots + tile,), -1, jnp.int32).at[slot].set(tok)
    is_real = slot_tok >= 0
    slot_src = jnp.maximum(slot_tok, 0)
    slot_dst = jnp.where(is_real, slot_tok, n + (slot_ids // tile % 2) * tile + slot_ids % tile)
    grp_gates = jnp.take_along_axis(gates.reshape(n, N_EXPERT_GROUPS, EXPERTS_PER_GROUP),
                                    gid[:, None, None], axis=1)[:, 0]
    slot_gates = jnp.where(is_real[:n_slots, None], grp_gates[slot_src[:n_slots]], 0.0)
    tile_start = jnp.arange(n_tiles, dtype=jnp.int32) * tile
    tile_grp = jnp.minimum(jnp.sum((tile_start[:, None] >= seg_end[None, :]).astype(jnp.int32), axis=1),
                           N_EXPERT_GROUPS - 1)
    return tile_grp, slot_src, slot_dst, slot_gates


def _moe_kernel(tile_grp_ref, src_ref, dst_ref, x_hbm, g_ref, wg_ref, wu_ref, wd_ref, ff_hbm,
                xg_sc, x_sc, acc_sc, ys_sc, gsem, ssem, *, tile, n_tok):
    del tile_grp_ref
    i = pl.program_id(0)
    e = pl.program_id(1)
    n_tiles = pl.num_programs(0)
    n_e = EXPERTS_PER_GROUP
    slab = ROW_CHUNKS

    def gather_copy(slot, row, buf):
        tok = src_ref[slot]
        return pltpu.make_async_copy(x_hbm.at[pl.ds(tok * slab, slab), :],
                                     xg_sc.at[buf, pl.ds(row * slab, slab), :], gsem.at[buf])

    def scatter_copy(slot, row, buf):
        tok = dst_ref[slot]
        return pltpu.make_async_copy(ys_sc.at[buf, pl.ds(row * slab, slab), :],
                                     ff_hbm.at[pl.ds(tok * slab, slab), :], ssem.at[buf])

    def wait_all(sc, sem, buf):
        pltpu.make_async_copy(sc.at[buf], sc.at[buf], sem.at[buf]).wait()

    cur = i % 2
    nxt = (i + 1) % 2
    n_steps = n_e // EXPERTS_PER_STEP

    @pl.when((i == 0) & (e == 0))
    def _():
        for j in range(tile):
            gather_copy(j, j, 0).start()
        ys_sc[...] = jnp.zeros(ys_sc.shape, F32)
        for b in range(2):
            fill = pltpu.make_async_copy(
                ys_sc.at[b], ff_hbm.at[pl.ds((n_tok + b * tile) * slab, tile * slab), :], ssem.at[b])
            fill.start()
            fill.wait()

    part_rows = tile // n_steps
    for j in range(part_rows):
        row = e * part_rows + j
        gather_copy((i + 1) * tile + row, row, nxt).start()

    @pl.when(e == 0)
    def _():
        wait_all(xg_sc, gsem, cur)
        for c in range(ROW_CHUNKS):
            x_sc[:, c * LANES:(c + 1) * LANES] = xg_sc[cur, pl.ds(c, tile, stride=ROW_CHUNKS), :].astype(BF16)

    x = x_sc[...]
    col = lax.broadcasted_iota(jnp.int32, g_ref.shape, 1)
    hids = []
    for u in range(EXPERTS_PER_STEP):
        hid = _silu(jnp.dot(x, wg_ref[u], preferred_element_type=F32)) * jnp.dot(x, wu_ref[u],
                                                                                 preferred_element_type=F32)
        gate = jnp.sum(jnp.where(col == e * EXPERTS_PER_STEP + u, g_ref[...], 0.0), axis=-1, keepdims=True)
        hids.append((hid * gate).astype(BF16))
    part = sum(jnp.dot(hids[u], wd_ref[u], preferred_element_type=F32) for u in range(EXPERTS_PER_STEP))

    @pl.when(e == 0)
    def _():
        acc_sc[...] = part

    @pl.when(e > 0)
    def _():
        acc_sc[...] += part

    @pl.when(e == n_steps - 1)
    def _():
        @pl.when(i >= 2)
        def _():
            wait_all(ys_sc, ssem, cur)

        y = acc_sc[...]
        for c in range(ROW_CHUNKS):
            ys_sc[cur, pl.ds(c, tile, stride=ROW_CHUNKS), :] = y[:, c * LANES:(c + 1) * LANES]
        for j in range(tile):
            scatter_copy(i * tile + j, j, cur).start()

        @pl.when(i == n_tiles - 1)
        def _():
            wait_all(ys_sc, ssem, cur)

            @pl.when(n_tiles >= 2)
            def _():
                wait_all(ys_sc, ssem, nxt)

            wait_all(xg_sc, gsem, nxt)


def _moe(x1r, gates, wg, wu, wd, *, layer=0, tile=MOE_TILE):
    n = gates.shape[0]
    d, f = wg.shape[1], wg.shape[2]
    e0 = layer * N_EXPERTS
    tile_grp, slot_src, slot_dst, slot_gates = _moe_routing(gates, tile)
    n_tiles = tile_grp.shape[0]
    n_rows = n + 2 * tile
    n_e = EXPERTS_PER_GROUP
    eps = EXPERTS_PER_STEP
    n_steps = n_e // eps
    w_idx = lambda i, e, tg, s, t: ((e0 + tg[i] * n_e) // eps + e, 0, 0)
    grid_spec = pltpu.PrefetchScalarGridSpec(
        num_scalar_prefetch=3,
        grid=(n_tiles, n_steps),
        in_specs=[
            pl.BlockSpec(memory_space=pl.ANY),
            pl.BlockSpec((tile, n_e), lambda i, e, tg, s, t: (i, 0)),
            pl.BlockSpec((eps, d, f), w_idx), pl.BlockSpec((eps, d, f), w_idx), pl.BlockSpec((eps, f, d), w_idx),
        ],
        out_specs=pl.BlockSpec(memory_space=pl.ANY),
        scratch_shapes=[
            pltpu.VMEM((2, tile * ROW_CHUNKS, LANES), F32), pltpu.VMEM((tile, d), BF16),
            pltpu.VMEM((tile, d), F32), pltpu.VMEM((2, tile * ROW_CHUNKS, LANES), F32),
            pltpu.SemaphoreType.DMA((2,)), pltpu.SemaphoreType.DMA((2,)),
        ],
    )
    return pl.pallas_call(
        functools.partial(_moe_kernel, tile=tile, n_tok=n),
        grid_spec=grid_spec,
        out_shape=jax.ShapeDtypeStruct((n_rows * ROW_CHUNKS, LANES), F32),
        compiler_params=_cparams(2),
        name="moe_grouped",
    )(tile_grp, slot_src, slot_dst, x1r, slot_gates, wg, wu, wd)


def _post2_kernel(x1r_ref, x1b_ref, ffr_ref, p_ref, wgate_ref, wproj_ref, lw_ref, lb_ref, out_a_ref, out_b_ref, *,
                  alpha, head_tiles):
    i = pl.program_id(0)
    tm = x1b_ref.shape[0]
    sub = 128
    gates, projs = [], []
    for r in range(tm // sub):
        rows = slice(r * sub, (r + 1) * sub)
        gates.append(jnp.dot(x1b_ref[rows, :], wgate_ref[...], preferred_element_type=F32))
        projs.append(jnp.dot(p_ref[rows, :], wproj_ref[...], preferred_element_type=F32))
    for r in range(tm // sub):
        rows = slice(r * sub, (r + 1) * sub)
        slab = pl.ds(r * sub * ROW_CHUNKS, sub * ROW_CHUNKS)
        ple = jax.nn.sigmoid(gates[r]) * projs[r]
        x2 = _layer_norm(alpha * _load_token_rows(x1r_ref.at[slab, :], sub)
                         + _load_token_rows(ffr_ref.at[slab, :], sub) + ple, lw_ref[...], lb_ref[...])
        if head_tiles is None:
            out_a_ref[rows, :] = x2
            out_b_ref[rows, :] = x2.astype(BF16)
        else:
            @pl.when(i < head_tiles)
            def _():
                out_a_ref[rows, :] = x2

            @pl.when(i >= head_tiles)
            def _():
                out_b_ref[rows, :] = x2


def _post2(x1r, x1b, ffr, p, w_gate_b, w_proj_b, lw, lb, *, layer, alpha, split_at=None, tm=512):
    n, d = x1b.shape
    dp = p.shape[1]
    per_layer = lambda shape: pl.BlockSpec((None,) + shape, lambda i: (layer,) + (0,) * len(shape))
    row = lambda width: pl.BlockSpec((tm, width), lambda i: (i, 0))
    slab = pl.BlockSpec((tm * ROW_CHUNKS, LANES), lambda i: (i, 0))
    if split_at is None:
        head_tiles = None
        out_specs = [row(d), row(d)]
        out_shape = [jax.ShapeDtypeStruct((n, d), F32), jax.ShapeDtypeStruct((n, d), BF16)]
    else:
        assert split_at % tm == 0 and (n - split_at) % tm == 0
        head_tiles = split_at // tm
        out_specs = [pl.BlockSpec((tm, d), lambda i: (jnp.minimum(i, head_tiles - 1), 0)),
                     pl.BlockSpec((tm, d), lambda i: (jnp.maximum(i - head_tiles, 0), 0))]
        out_shape = [jax.ShapeDtypeStruct((split_at, d), F32), jax.ShapeDtypeStruct((n - split_at, d), F32)]
    return pl.pallas_call(
        functools.partial(_post2_kernel, alpha=alpha, head_tiles=head_tiles),
        grid=(n // tm,),
        in_specs=[slab, row(d), slab, row(dp), per_layer((d, d)), per_layer((dp, d)), per_layer((1, d)),
                  per_layer((1, d))],
        out_specs=out_specs,
        out_shape=out_shape,
        compiler_params=_cparams(1),
        name="ple_ln2",
    )(x1r, x1b, ffr, p, w_gate_b, w_proj_b, lw, lb)


def kernel(x_prompt, x_sample, p_prompt, p_sample, cache_diff_k, cache_diff_v, cache_moba_k, cache_moba_v, state_ssm, state_conv, page_table, w_in, w_out, diff_lambda, diff_norm_w, ssm_conv_w, ssm_conv_b, ssm_dt_bias, ssm_a_log, ssm_d, ssm_norm_w, ln1_w, ln1_b, ln2_w, ln2_b, w_router, b_router, w_exp_gate, w_exp_up, w_exp_down, w_ple_proj, w_ple_gate):
    depth = w_in.shape[0]
    batch, seq, d_model = x_prompt.shape
    n_seq, t_new, _ = x_sample.shape
    n_p, n_s = batch * seq, n_seq * t_new
    alpha = (2 * depth) ** 0.25
    n_heads = DA_HEADS + MB_HEADS
    slopes = jnp.asarray(2.0 ** (-8.0 * np.arange(1, n_heads + 1) / n_heads), dtype=F32)

    def score_col_slopes(sl, cols_per_head):
        v = jnp.repeat(sl, cols_per_head)
        return jnp.pad(v, (0, LANES - v.shape[0])).reshape(1, LANES)

    slopes_diff_cols = score_col_slopes(slopes[:DA_HEADS], 2 * t_new)
    slopes_moba_cols = score_col_slopes(slopes[DA_HEADS:], t_new)
    pad_lanes = lambda v: jnp.pad(v, (0, LANES - v.shape[0])).reshape(1, LANES)
    per_channel = lambda v: jnp.repeat(v, SSM_HEADDIM).reshape(1, D_SSM)

    n_pool, page = cache_diff_k.shape[1], cache_diff_k.shape[2]
    flat_cache = lambda c: c.reshape(depth, n_pool, page * DA_HEADS, HEAD_DIM)
    ck_a, cv_a, ck_b, cv_b = map(flat_cache, (cache_diff_k, cache_diff_v, cache_moba_k, cache_moba_v))

    w_in_b = jnp.pad(w_in.astype(BF16), ((0, 0), (0, 0), (0, D_IN_PAD - D_IN_PROJ)))
    w_dtx = jnp.repeat(w_in[:, :, DT_COL:DT_COL + SSM_HEADS].astype(BF16), SSM_HEADDIM, axis=2)
    w_out_b = w_out.astype(BF16)
    w_gate_b, w_proj_b = w_ple_gate.astype(BF16), w_ple_proj.astype(BF16)
    d_exp = w_exp_gate.shape[-1]
    wg_b = w_exp_gate.astype(BF16).reshape(depth * N_EXPERTS, d_model, d_exp)
    wu_b = w_exp_up.astype(BF16).reshape(depth * N_EXPERTS, d_model, d_exp)
    wd_b = w_exp_down.astype(BF16).reshape(depth * N_EXPERTS, d_exp, d_model)
    w_router3 = _router_weight_passes(w_router)
    b_router2 = b_router.reshape(1, N_EXPERTS)
    ln_rows = lambda v: v.reshape(depth, 1, d_model)
    ln1_w3, ln1_b3, ln2_w3, ln2_b3 = map(ln_rows, (ln1_w, ln1_b, ln2_w, ln2_b))

    x = jnp.concatenate([x_prompt.reshape(n_p, d_model), x_sample.reshape(n_s, d_model)], axis=0)
    xb = x.astype(BF16)
    outs_p = [[] for _ in range(6)]
    outs_s = [[] for _ in range(6)]
    for i in range(depth):
        lam_init = 0.8 - 0.6 * math.exp(-0.3 * i)
        z = _matmul(xb, w_in_b, i, tm=512, tn=D_IN_PAD // 3)
        zs = z[n_p:].reshape(n_seq, t_new, D_IN_PAD)
        dtx = _matmul(xb[n_p:], w_dtx, i, tm=n_s, tn=D_SSM).reshape(n_seq, t_new, D_SSM)

        dl = diff_lambda[i]
        nw_a = diff_norm_w[i].reshape(1, HEAD_DIM)
        cw, cb = ssm_conv_w[i], ssm_conv_b[i].reshape(1, D_XBC)
        nw_c = ssm_norm_w[i].reshape(1, D_SSM)

        ka, vt, kmean, *kv_new_p = _kv_prep(z, batch=batch, seq=seq)
        oa_p = _diff_prompt(z, ka, vt, slopes, dl, nw_a, batch=batch, seq=seq, lam_init=lam_init)
        ob_p = _moba_prompt(z, ka, vt, kmean, slopes, batch=batch, seq=seq)
        oc_p, h_p = _ssd_prompt(z, cw, cb, pad_lanes(ssm_dt_bias[i]), pad_lanes(ssm_a_log[i]),
                                per_channel(ssm_d[i]), nw_c, batch=batch, seq=seq)
        oa_s = _attn_decode(zs, ck_a, cv_a, page_table, slopes_diff_cols, dl, nw_a, layer=i, moba=False,
                            lam_init=lam_init)
        ob_s = _attn_decode(zs, ck_b, cv_b, page_table, slopes_moba_cols, dl, nw_a, layer=i, moba=True,
                            lam_init=lam_init)
        oc_s, h_s, conv_s = _ssd_decode(zs, dtx, state_conv, state_ssm, cw, cb, per_channel(ssm_dt_bias[i]),
                                        per_channel(ssm_a_log[i]), per_channel(ssm_d[i]), nw_c, layer=i)

        head_shape = lambda a, lead: a.reshape(*lead, DA_HEADS, HEAD_DIM)
        for j, blk in enumerate((KA_BLK, VA_BLK, KB_BLK, VB_BLK)):
            c0 = blk * LANES
            outs_p[j].append(kv_new_p[j])
            outs_s[j].append(head_shape(zs[:, :, c0:c0 + DA_HEADS * HEAD_DIM], (n_seq, t_new)))
        outs_p[4].append(h_p)
        outs_p[5].append(jnp.stack([z[(b + 1) * seq - (CONV_W - 1):(b + 1) * seq, XS_COL:XS_COL + D_XBC]
                                    for b in range(batch)]))
        outs_s[4].append(h_s)
        outs_s[5].append(conv_s)

        oa = jnp.concatenate([oa_p, oa_s.reshape(n_s, -1)], axis=0)
        ob = jnp.concatenate([ob_p, ob_s.reshape(n_s, -1)], axis=0)
        oc = jnp.concatenate([oc_p, oc_s.reshape(n_s, -1)], axis=0)
        x1r, x1b, gates = _post1(x, oa, ob, oc, w_out_b, ln1_w3, ln1_b3, w_router3, b_router2, layer=i, alpha=alpha)
        ffr = _moe(x1r, gates, wg_b, wu_b, wd_b, layer=i)
        p = jnp.concatenate([p_prompt[i].reshape(n_p, -1), p_sample[i].reshape(n_s, -1)], axis=0).astype(BF16)
        last = i == depth - 1
        x, xb = _post2(x1r, x1b, ffr, p, w_gate_b, w_proj_b, ln2_w3, ln2_b3, layer=i, alpha=alpha,
                       split_at=n_p if last else None, tm=256 if last else 512)

    y_p = x.reshape(batch, seq, d_model)
    y_s = xb.reshape(n_seq, t_new, d_model)
    return (y_p, y_s, *[jnp.stack(o) for o in outs_p], *[jnp.stack(o) for o in outs_s])
```

```python
import functools
import math

import jax
import jax.numpy as jnp
import numpy as np
from jax import lax
from jax.experimental import pallas as pl
from jax.experimental.pallas import tpu as pltpu

F32 = jnp.float32
BF16 = jnp.bfloat16

LANES = 128
HEAD_DIM = 128
DA_HEADS = 4
DA_QK = 64
MB_HEADS = 4
MOBA_BLOCK = 256
MOBA_TOPK = 3
D_SSM = 1024
SSM_HEADS = 16
SSM_HEADDIM = 64
SSM_GROUPS = 2
D_STATE = 128
CONV_W = 4
D_XBC = D_SSM + 2 * SSM_GROUPS * D_STATE
N_EXPERTS = 16
N_EXPERT_GROUPS = 4
EXPERTS_PER_GROUP = 4
LN_EPS = 1e-5
RMS_EPS = 1e-6
VMEM_LIMIT = 56 * 1024 * 1024

QA_BLK, KA_BLK, VA_BLK, QB_BLK, KB_BLK, VB_BLK = 0, 4, 8, 12, 16, 20
ZC_COL, XS_COL, BC_COL, DT_COL = 3072, 4096, 5120, 5632
D_IN_PROJ = 5648
D_IN_PAD = 5760

NT_DIMS = (((1,), (1,)), ((), ()))
TN_DIMS = (((0,), (0,)), ((), ()))


def _cparams(n_axes):
    return pltpu.CompilerParams(dimension_semantics=("arbitrary",) * n_axes, vmem_limit_bytes=VMEM_LIMIT)


def _mm_kernel(x_ref, w_ref, o_ref):
    o_ref[...] = jnp.dot(x_ref[...], w_ref[...], preferred_element_type=F32).astype(o_ref.dtype)


def _matmul(x, w, layer, tm, tn, out_dtype=F32):
    m, k = x.shape
    n = w.shape[2]
    return pl.pallas_call(
        _mm_kernel,
        grid=(n // tn, m // tm),
        in_specs=[pl.BlockSpec((tm, k), lambda j, i: (i, 0)), pl.BlockSpec((None, k, tn), lambda j, i: (layer, 0, j))],
        out_specs=pl.BlockSpec((tm, tn), lambda j, i: (i, j)),
        out_shape=jax.ShapeDtypeStruct((m, n), out_dtype),
        compiler_params=_cparams(2),
        name="in_proj",
    )(x, w)


POS_SPLIT = 64
ONEHOT_LANE0 = 8
LOG2E = math.log2(math.e)
MASK_BIAS = -1e30


ATT_TILE = 256
N_ATT_HEADS = DA_HEADS + MB_HEADS
VT_ROWS = HEAD_DIM + 16
HEADS_PER_PHASE = 4


def _kv_prep_kernel(k_ref, v_ref, ka_ref, vt_ref, km_ref, ko_a_ref, vo_a_ref, ko_b_ref, vo_b_ref, *, ck):
    typ = pl.program_id(1)
    c = pl.program_id(2)
    tiles = ck // ATT_TILE
    pos = c * ck + lax.broadcasted_iota(jnp.int32, (ck, LANES), 0)
    lane = lax.broadcasted_iota(jnp.int32, (ck, LANES), 1)
    kind = lane % 3
    feat = jnp.where(lane < 6, jnp.where(kind == 0, pos // POS_SPLIT, jnp.where(kind == 1, pos % POS_SPLIT, 1)), 0)
    feat = jnp.where(lane - ONEHOT_LANE0 == pos // MOBA_BLOCK, 1, feat).astype(F32).astype(BF16)
    for hd in range(DA_HEADS):
        lanes = slice(hd * HEAD_DIM, (hd + 1) * HEAD_DIM)
        ka_ref[hd, :, :HEAD_DIM] = k_ref[:, lanes].astype(BF16)
        ka_ref[hd, :, HEAD_DIM:] = feat
        for j in range(tiles):
            rows = slice(j * ATT_TILE, (j + 1) * ATT_TILE)
            sum_rows = lax.broadcasted_iota(jnp.int32, (VT_ROWS - HEAD_DIM, ATT_TILE), 0) == 0
            vt_ref[hd, j] = jnp.concatenate([v_ref[rows, lanes].T, sum_rows.astype(F32)], axis=0).astype(BF16)
            km_ref[hd, pl.ds(c * tiles + j, 1), :] = jnp.mean(k_ref[rows, lanes], axis=0, keepdims=True)
    for t, (ko_ref, vo_ref) in enumerate(((ko_a_ref, vo_a_ref), (ko_b_ref, vo_b_ref))):
        @pl.when(typ == t)
        def _():
            for hd in range(DA_HEADS):
                lanes = slice(hd * HEAD_DIM, (hd + 1) * HEAD_DIM)
                ko_ref[:, hd, :] = k_ref[:, lanes]
                vo_ref[:, hd, :] = v_ref[:, lanes]


def _kv_prep(z, *, batch, seq, ck=1024):
    nc = seq // ck
    nq = seq // ATT_TILE
    tiles = ck // ATT_TILE
    width = DA_HEADS * HEAD_DIM
    k_blk = lambda t: (KA_BLK + (KB_BLK - KA_BLK) * t) // DA_HEADS
    v_blk = lambda t: (VA_BLK + (VB_BLK - VA_BLK) * t) // DA_HEADS
    new_a = pl.BlockSpec((None, ck, DA_HEADS, HEAD_DIM), lambda b, t, c: (b, jnp.where(t == 0, c, nc - 1), 0, 0))
    new_b = pl.BlockSpec((None, ck, DA_HEADS, HEAD_DIM), lambda b, t, c: (b, jnp.where(t == 1, c, 0), 0, 0))
    return pl.pallas_call(
        functools.partial(_kv_prep_kernel, ck=ck),
        grid=(batch, 2, nc),
        in_specs=[
            pl.BlockSpec((ck, width), lambda b, t, c: (b * nc + c, k_blk(t))),
            pl.BlockSpec((ck, width), lambda b, t, c: (b * nc + c, v_blk(t))),
        ],
        out_specs=[
            pl.BlockSpec((None, DA_HEADS, ck, 2 * HEAD_DIM), lambda b, t, c: (b, t, c, 0)),
            pl.BlockSpec((None, DA_HEADS, tiles, VT_ROWS, ATT_TILE), lambda b, t, c: (b, t, c, 0, 0)),
            pl.BlockSpec((None, DA_HEADS, nq, HEAD_DIM), lambda b, t, c: (b, t, 0, 0)),
            new_a, new_a, new_b, new_b,
        ],
        out_shape=[
            jax.ShapeDtypeStruct((batch, N_ATT_HEADS, seq, 2 * HEAD_DIM), BF16),
            jax.ShapeDtypeStruct((batch, N_ATT_HEADS, seq // ATT_TILE, VT_ROWS, ATT_TILE), BF16),
            jax.ShapeDtypeStruct((batch, N_ATT_HEADS, seq // ATT_TILE, HEAD_DIM), F32),
        ] + [jax.ShapeDtypeStruct((batch, seq, DA_HEADS, HEAD_DIM), F32)] * 4,
        compiler_params=_cparams(3),
        name="kv_prep",
    )(z, z)


def _query_feature_rows(slope, q0, n_cols):
    row = lax.broadcasted_iota(jnp.int32, (8, n_cols), 0)
    c = slope * LOG2E
    kind = row % 3
    val = jnp.where(kind == 0, c * POS_SPLIT, jnp.where(kind == 1, c, -(c * q0.astype(F32))))
    head = val.astype(BF16).astype(F32)
    return jnp.where(row < 3, head, jnp.where(row < 6, val - head, 0.0))


def _softmax_t_step(n_heads, kj, ka_ref, vt_ref, qa_sc, m_sc, acc_sc, bias=None):
    t = ATT_TILE
    rows = pl.ds(pl.multiple_of(kj * t, t), t)
    for h0 in range(0, n_heads, HEADS_PER_PHASE):
        heads = range(h0, min(h0 + HEADS_PER_PHASE, n_heads))
        scores = {h: jnp.dot(ka_ref[h, rows, :], qa_sc[h], preferred_element_type=F32) for h in heads}
        probs = {}
        for h in heads:
            s = scores[h] if bias is None else scores[h] + bias
            m = m_sc[h]
            m_new = jnp.maximum(m, jnp.max(s, axis=0, keepdims=True))
            alpha = jnp.exp2(m - m_new)
            p = jnp.exp2(s - m_new)
            m_sc[h] = m_new
            probs[h] = (alpha, p.astype(BF16))
        for h in heads:
            alpha, p = probs[h]
            acc_sc[h] = alpha * acc_sc[h] + jnp.dot(vt_ref[h, kj], p, preferred_element_type=F32)


def _softmax_t_run(n_heads, own, ka_ref, vt_ref, mask_ref, qa_sc, m_sc, acc_sc):
    m_sc[...] = jnp.full(m_sc.shape, -jnp.inf, F32)
    acc_sc[...] = jnp.zeros(acc_sc.shape, F32)
    _softmax_t_step(n_heads, own, ka_ref, vt_ref, qa_sc, m_sc, acc_sc, bias=mask_ref[...])

    def body(kj, carry):
        _softmax_t_step(n_heads, kj, ka_ref, vt_ref, qa_sc, m_sc, acc_sc)
        return carry

    lax.fori_loop(0, own, body, 0)


def _causal_mask_bias(n_maps):
    t = ATT_TILE
    k_i = lax.broadcasted_iota(jnp.int32, (t, n_maps * t), 0)
    q_i = lax.broadcasted_iota(jnp.int32, (t, n_maps * t), 1) % t
    return jnp.where(k_i <= q_i, 0.0, -jnp.inf).astype(F32)


def _diff_prompt_kernel(slopes_ref, dl_ref, q_ref, ka_ref, vt_ref, mask_ref, nw_ref, o_ref,
                        qa_sc, m_sc, acc_sc, *, lam_init):
    t = ATT_TILE
    qi = pl.program_id(1)
    q0 = qi * t
    n_cols = 2 * t
    for h in range(DA_HEADS):
        qt = q_ref[:, h * HEAD_DIM:(h + 1) * HEAD_DIM].T * (DA_QK ** -0.5 * LOG2E)
        row = lax.broadcasted_iota(jnp.int32, qt.shape, 0)
        q_top = jnp.concatenate([jnp.where(row < DA_QK, qt, 0.0), jnp.where(row >= DA_QK, qt, 0.0)], axis=1)
        q_feat = jnp.concatenate([_query_feature_rows(slopes_ref[h], q0, n_cols),
                                  jnp.zeros((LANES - 8, n_cols), F32)], axis=0)
        qa_sc[h] = jnp.concatenate([q_top, q_feat], axis=0).astype(BF16)

    _softmax_t_run(DA_HEADS, qi, ka_ref, vt_ref, mask_ref, qa_sc, m_sc, acc_sc)

    dl = dl_ref[...]
    lam = (jnp.exp(jnp.sum(dl[0:1] * dl[1:2], axis=-1, keepdims=True))
           - jnp.exp(jnp.sum(dl[2:3] * dl[3:4], axis=-1, keepdims=True)) + lam_init)
    for h in range(DA_HEADS):
        o2 = acc_sc[h, :HEAD_DIM, :] / acc_sc[h, HEAD_DIM:HEAD_DIM + 1, :]
        ot = o2[:, :t] - lam * o2[:, t:]
        var = jnp.mean(ot * ot, axis=0, keepdims=True)
        o = (ot * lax.rsqrt(var + RMS_EPS)).T * nw_ref[...] * (1.0 - lam_init)
        o_ref[:, h * HEAD_DIM:(h + 1) * HEAD_DIM] = o.astype(o_ref.dtype)


def _diff_prompt(z, ka, vt, slopes, dl, nw, *, batch, seq, lam_init):
    t = ATT_TILE
    nq = seq // t
    width = DA_HEADS * HEAD_DIM
    n_cols = 2 * t
    kern = functools.partial(_diff_prompt_kernel, lam_init=lam_init)
    return pl.pallas_call(
        kern,
        grid=(batch, nq),
        in_specs=[
            pl.BlockSpec(memory_space=pltpu.SMEM),
            pl.BlockSpec((4, DA_QK), lambda b, i: (0, 0)),
            pl.BlockSpec((t, width), lambda b, i: (b * nq + i, QA_BLK // DA_HEADS)),
            pl.BlockSpec((None, DA_HEADS, seq, 2 * HEAD_DIM), lambda b, i: (b, 0, 0, 0)),
            pl.BlockSpec((None, DA_HEADS, nq, VT_ROWS, t), lambda b, i: (b, 0, 0, 0, 0)),
            pl.BlockSpec((t, n_cols), lambda b, i: (0, 0)),
            pl.BlockSpec((1, HEAD_DIM), lambda b, i: (0, 0)),
        ],
        out_specs=pl.BlockSpec((t, width), lambda b, i: (b * nq + i, 0)),
        out_shape=jax.ShapeDtypeStruct((batch * seq, width), BF16),
        scratch_shapes=[
            pltpu.VMEM((DA_HEADS, 2 * HEAD_DIM, n_cols), BF16),
            pltpu.VMEM((DA_HEADS, 1, n_cols), F32), pltpu.VMEM((DA_HEADS, VT_ROWS, n_cols), F32),
        ],
        compiler_params=_cparams(2),
        name="diff_prompt",
    )(slopes, dl, z, ka, vt, _causal_mask_bias(2), nw)


def _top3_mask_rows(gate_t, n_valid, own):
    nb = gate_t.shape[0]
    row = lax.broadcasted_iota(jnp.int32, gate_t.shape, 0)
    g = jnp.where(row < n_valid, gate_t, -jnp.inf)
    sel = row == own
    for _ in range(MOBA_TOPK):
        mx = jnp.max(g, axis=0, keepdims=True)
        idx = jnp.min(jnp.where(g == mx, row, nb), axis=0, keepdims=True)
        pick = (row == idx) & (mx > -jnp.inf)
        sel = sel | pick
        g = jnp.where(pick, -jnp.inf, g)
    return jnp.where(sel, 0.0, MASK_BIAS)


def _moba_prompt_kernel(slopes_ref, q_ref, ka_ref, vt_ref, km_ref, mask_ref, o_ref,
                        qa_sc, m_sc, acc_sc, *, nb):
    blk = MOBA_BLOCK
    own = pl.program_id(1)
    q0 = own * blk
    for h in range(MB_HEADS):
        qt = q_ref[:, h * HEAD_DIM:(h + 1) * HEAD_DIM].T
        gate_t = jnp.dot(km_ref[h], qt, preferred_element_type=F32, precision=lax.Precision.HIGHEST)
        q_feat = jnp.concatenate([_query_feature_rows(slopes_ref[DA_HEADS + h], q0, blk),
                                  _top3_mask_rows(gate_t, own, own),
                                  jnp.zeros((LANES - ONEHOT_LANE0 - nb, blk), F32)], axis=0)
        qa_sc[h] = jnp.concatenate([qt * (HEAD_DIM ** -0.5 * LOG2E), q_feat], axis=0).astype(BF16)

    _softmax_t_run(MB_HEADS, own, ka_ref, vt_ref, mask_ref, qa_sc, m_sc, acc_sc)

    for h in range(MB_HEADS):
        o = acc_sc[h, :HEAD_DIM, :] / acc_sc[h, HEAD_DIM:HEAD_DIM + 1, :]
        o_ref[:, h * HEAD_DIM:(h + 1) * HEAD_DIM] = o.T.astype(o_ref.dtype)


def _moba_prompt(z, ka, vt, kmean, slopes, *, batch, seq):
    blk = MOBA_BLOCK
    assert blk == ATT_TILE
    nb = seq // blk
    assert nb <= LANES - ONEHOT_LANE0 and nb % 8 == 0
    width = MB_HEADS * HEAD_DIM
    kern = functools.partial(_moba_prompt_kernel, nb=nb)
    return pl.pallas_call(
        kern,
        grid=(batch, nb),
        in_specs=[
            pl.BlockSpec(memory_space=pltpu.SMEM),
            pl.BlockSpec((blk, width), lambda b, i: (b * nb + i, QB_BLK // MB_HEADS)),
            pl.BlockSpec((None, MB_HEADS, seq, 2 * HEAD_DIM), lambda b, i: (b, 1, 0, 0)),
            pl.BlockSpec((None, MB_HEADS, nb, VT_ROWS, blk), lambda b, i: (b, 1, 0, 0, 0)),
            pl.BlockSpec((None, MB_HEADS, nb, HEAD_DIM), lambda b, i: (b, 1, 0, 0)),
            pl.BlockSpec((blk, blk), lambda b, i: (0, 0)),
        ],
        out_specs=pl.BlockSpec((blk, width), lambda b, i: (b * nb + i, 0)),
        out_shape=jax.ShapeDtypeStruct((batch * seq, width), BF16),
        scratch_shapes=[
            pltpu.VMEM((MB_HEADS, 2 * HEAD_DIM, blk), BF16),
            pltpu.VMEM((MB_HEADS, 1, blk), F32), pltpu.VMEM((MB_HEADS, VT_ROWS, blk), F32),
        ],
        compiler_params=_cparams(2),
        name="moba_prompt",
    )(slopes, z, ka, vt, kmean, _causal_mask_bias(1))


def _softplus(x):
    return jnp.maximum(x, 0.0) + jnp.log1p(jnp.exp(-jnp.abs(x)))


def _silu(x):
    return x * jax.nn.sigmoid(x)


def _conv_silu(xp_sc, w, b, t):
    y = b
    for j in range(CONV_W):
        y = y + xp_sc[5 + j:5 + j + t, :] * w[j:j + 1, :]
    return _silu(y)


def _ssd_prompt_kernel(zc_ref, x_ref, bc_ref, dt_ref, cw_ref, cb_ref, dtb_ref, alog_ref, dsk_ref, nw_ref,
                       y_ref, hfin_ref, xpx_sc, xpbc_sc, h_sc, y_sc, *, t):
    c = pl.program_id(1)
    nc = pl.num_programs(1)
    hp = LANES // SSM_HEADDIM
    hpg = SSM_HEADS // SSM_GROUPS

    @pl.when(c == 0)
    def _():
        xpx_sc[0:8, :] = jnp.zeros((8, D_SSM), F32)
        xpbc_sc[0:8, :] = jnp.zeros((8, D_XBC - D_SSM), F32)
        h_sc[...] = jnp.zeros(h_sc.shape, F32)

    xpx_sc[8:8 + t, :] = x_ref[...]
    xpbc_sc[8:8 + t, :] = bc_ref[...]
    cw = cw_ref[...]
    cb = cb_ref[...]
    xs = _conv_silu(xpx_sc, cw[:, :D_SSM], cb[:, :D_SSM], t)
    bc = _conv_silu(xpbc_sc, cw[:, D_SSM:], cb[:, D_SSM:], t)
    xpx_sc[5:8, :] = xpx_sc[t + 5:t + 8, :]
    xpbc_sc[5:8, :] = xpbc_sc[t + 5:t + 8, :]

    dt = _softplus(dt_ref[...] + dtb_ref[...])
    da = dt * (-jnp.exp(alog_ref[...]))
    r_i = lax.broadcasted_iota(jnp.int32, (t, t), 0)
    c_i = lax.broadcasted_iota(jnp.int32, (t, t), 1)
    tril = c_i <= r_i
    acs = jnp.dot(tril.astype(F32), da, preferred_element_type=F32, precision=lax.Precision.HIGHEST)
    acs_t = acs.T
    lane = lax.broadcasted_iota(jnp.int32, (t, LANES), 1)
    sub = lax.broadcasted_iota(jnp.int32, (LANES, LANES), 0)
    lo_lane = lane < SSM_HEADDIM
    lo_sub = sub < SSM_HEADDIM

    for g in range(SSM_GROUPS):
        bg = bc[:, g * D_STATE:(g + 1) * D_STATE]
        cg = bc[:, (SSM_GROUPS + g) * D_STATE:(SSM_GROUPS + g + 1) * D_STATE]
        bgb = bg.astype(BF16)
        cgb = cg.astype(BF16)
        gmat = lax.dot_general(cgb, bgb, NT_DIMS, preferred_element_type=F32)
        for pr in range(hpg // hp):
            ha = g * hpg + pr * hp
            cb0 = ha * SSM_HEADDIM
            x2 = xs[:, cb0:cb0 + LANES]
            dt2 = jnp.where(lo_lane, dt[:, ha:ha + 1], dt[:, ha + 1:ha + 2])
            xdt = x2 * dt2
            ydiag = jnp.zeros((t, LANES), F32)
            for u in range(hp):
                hh = ha + u
                seg = acs[:, hh:hh + 1] - acs_t[hh:hh + 1, :]
                lmat = jnp.exp(jnp.where(tril, seg, -jnp.inf))
                xu = jnp.where(lo_lane if u == 0 else ~lo_lane, xdt, 0.0)
                ydiag = ydiag + jnp.dot((gmat * lmat).astype(BF16), xu.astype(BF16), preferred_element_type=F32)
            h2 = h_sc[ha:ha + hp].reshape(LANES, D_STATE)
            yoff = lax.dot_general(cgb, h2.astype(BF16), NT_DIMS, preferred_element_type=F32)
            e2 = jnp.where(lo_lane, jnp.exp(acs[:, ha:ha + 1]), jnp.exp(acs[:, ha + 1:ha + 2]))
            y = ydiag + e2 * yoff + dsk_ref[:, cb0:cb0 + LANES] * x2
            y_sc[:, cb0:cb0 + LANES] = y * _silu(zc_ref[:, cb0:cb0 + LANES])
            last = acs[t - 1:t, :]
            te2 = jnp.where(lo_lane, jnp.exp(last[:, ha:ha + 1] - acs[:, ha:ha + 1]),
                            jnp.exp(last[:, ha + 1:ha + 2] - acs[:, ha + 1:ha + 2]))
            upd = lax.dot_general((xdt * te2).astype(BF16), bgb, TN_DIMS, preferred_element_type=F32)
            dec = jnp.where(lo_sub, jnp.exp(last[:, ha:ha + 1]), jnp.exp(last[:, ha + 1:ha + 2]))
            h_sc[ha:ha + hp] = (dec * h2 + upd).reshape(hp, SSM_HEADDIM, D_STATE)

    gw = D_SSM // SSM_GROUPS
    for g in range(SSM_GROUPS):
        yg = y_sc[:, g * gw:(g + 1) * gw]
        var = jnp.mean(yg * yg, axis=-1, keepdims=True)
        y_ref[:, g * gw:(g + 1) * gw] = (yg * lax.rsqrt(var + RMS_EPS) * nw_ref[:, g * gw:(g + 1) * gw]).astype(y_ref.dtype)

    @pl.when(c == nc - 1)
    def _():
        hfin_ref[0] = h_sc[...]


def _ssd_prompt(z, cw, cb, dtb, alog, dsk, nw, *, batch, seq, t=128):
    nc = seq // t
    kern = functools.partial(_ssd_prompt_kernel, t=t)
    full = lambda shape: pl.BlockSpec(shape, lambda b, c: (0,) * len(shape))
    return pl.pallas_call(
        kern,
        grid=(batch, nc),
        in_specs=[
            pl.BlockSpec((t, D_SSM), lambda b, c: (b * nc + c, ZC_COL // D_SSM)),
            pl.BlockSpec((t, D_SSM), lambda b, c: (b * nc + c, XS_COL // D_SSM)),
            pl.BlockSpec((t, 512), lambda b, c: (b * nc + c, BC_COL // 512)),
            pl.BlockSpec((t, LANES), lambda b, c: (b * nc + c, DT_COL // LANES)),
            full((CONV_W, D_XBC)), full((1, D_XBC)), full((1, LANES)), full((1, LANES)),
            full((1, D_SSM)), full((1, D_SSM)),
        ],
        out_specs=[
            pl.BlockSpec((t, D_SSM), lambda b, c: (b * nc + c, 0)),
            pl.BlockSpec((1, SSM_HEADS, SSM_HEADDIM, D_STATE), lambda b, c: (b, 0, 0, 0)),
        ],
        out_shape=[
            jax.ShapeDtypeStruct((batch * seq, D_SSM), BF16),
            jax.ShapeDtypeStruct((batch, SSM_HEADS, SSM_HEADDIM, D_STATE), F32),
        ],
        scratch_shapes=[
            pltpu.VMEM((t + 8, D_SSM), F32), pltpu.VMEM((t + 8, D_XBC - D_SSM), F32),
            pltpu.VMEM((SSM_HEADS, SSM_HEADDIM, D_STATE), F32), pltpu.VMEM((t, D_SSM), F32),
        ],
        compiler_params=_cparams(2),
        name="ssd_prompt",
    )(z, z, z, z, cw, cb, dtb, alog, dsk, nw)


NEW_ROWS = 16


SEQS_PER_STEP = 2


def _attn_decode_kernel(*refs, moba, n_pages, page, t_new, lam_init, spb):
    pt_ref = refs[0]
    del pt_ref
    slopes_ref, dl_ref, nw_ref, q_ref, kn_ref, vn_ref = refs[1:7]
    n_pg = spb * n_pages
    kp_refs = refs[7:7 + n_pg]
    vp_refs = refs[7 + n_pg:7 + 2 * n_pg]
    o_ref = refs[7 + 2 * n_pg]
    scratch = refs[8 + 2 * n_pg:]
    kb_scs, vb_scs, new_sc = scratch[:spb], scratch[spb:2 * spb], scratch[2 * spb]
    past = n_pages * page
    width = DA_HEADS * HEAD_DIM
    cols_per_head = t_new if moba else 2 * t_new
    lanes_per_col_grp = HEAD_DIM if moba else DA_QK
    scale = HEAD_DIM ** -0.5 if moba else DA_QK ** -0.5
    seqs = range(spb)
    coli = lax.broadcasted_iota(jnp.int32, (1, LANES), 1)
    slope = slopes_ref[...]

    ksums = []
    for s in seqs:
        ksum = []
        for p in range(n_pages):
            ksum_p = []
            kp, vp = kp_refs[s * n_pages + p], vp_refs[s * n_pages + p]
            for hd in range(DA_HEADS):
                lo = hd * HEAD_DIM
                kh = kp[pl.ds(hd, page, stride=DA_HEADS), :]
                kb_scs[s][p * page:(p + 1) * page, lo:lo + HEAD_DIM] = kh.astype(BF16)
                vb_scs[s][p * page:(p + 1) * page, lo:lo + HEAD_DIM] = vp[pl.ds(hd, page, stride=DA_HEADS), :].astype(BF16)
                if moba:
                    ksum_p.append(jnp.sum(kh, axis=0, keepdims=True))
            if moba:
                ksum.append(jnp.concatenate(ksum_p, axis=1))
        ksums.append(ksum)
        for j, (src, dst) in enumerate(((kn_ref, kb_scs[s]), (vn_ref, vb_scs[s]))):
            new_sc[s, j] = jnp.zeros(new_sc.shape[2:], F32)
            new_sc[s, j, 0:t_new, :] = src[s]
            dst[past:past + NEW_ROWS, :] = new_sc[s, j].astype(BF16)

    qs, scores = [], []
    for s in seqs:
        q = q_ref[s]
        qt = jnp.concatenate([q] * (LANES // t_new), axis=0)
        r_i = lax.broadcasted_iota(jnp.int32, qt.shape, 0)
        l_i = lax.broadcasted_iota(jnp.int32, qt.shape, 1)
        qmat = jnp.where(l_i // lanes_per_col_grp == r_i // t_new, qt, 0.0)
        qs.append(q)
        scores.append(lax.dot_general(kb_scs[s][...], (qmat * scale).astype(BF16), NT_DIMS,
                                      preferred_element_type=F32))

    probs = []
    for s in seqs:
        rowp = lax.broadcasted_iota(jnp.int32, (past, LANES), 0)
        s_past = scores[s][:past] + slope * (rowp - past).astype(F32)
        rown = lax.broadcasted_iota(jnp.int32, (NEW_ROWS, LANES), 0)
        s_new = scores[s][past:] + slope * rown.astype(F32)
        s_new = jnp.where(rown <= coli % t_new, s_new, -jnp.inf)

        if moba:
            nbp = past // MOBA_BLOCK
            ppb = MOBA_BLOCK // page
            gate = jnp.zeros((nbp, LANES), F32)
            kmean = jnp.concatenate(
                [sum(ksums[s][n * ppb:(n + 1) * ppb]) for n in range(nbp)], axis=0) * (1.0 / MOBA_BLOCK)
            for i in range(MB_HEADS * t_new):
                hd, tok = i // t_new, i % t_new
                lo = hd * HEAD_DIM
                g_i = jnp.sum(kmean[:, lo:lo + HEAD_DIM] * qs[s][tok:tok + 1, lo:lo + HEAD_DIM], axis=-1,
                              keepdims=True)
                gate = jnp.where(coli == i, g_i, gate)
            rowb = lax.broadcasted_iota(jnp.int32, gate.shape, 0)
            sel = jnp.zeros(gate.shape, jnp.bool_)
            g = gate
            for _ in range(MOBA_TOPK):
                mx = jnp.max(g, axis=0, keepdims=True)
                idx = jnp.min(jnp.where(g == mx, rowb, nbp), axis=0, keepdims=True)
                pick = (rowb == idx) & (mx > -jnp.inf)
                sel = sel | pick
                g = jnp.where(pick, -jnp.inf, g)
            sel_bias = jnp.where(sel, 0.0, -jnp.inf)
            s_past = (s_past.reshape(nbp, MOBA_BLOCK, LANES) + sel_bias[:, None, :]).reshape(past, LANES)

        m = jnp.maximum(jnp.max(s_past, axis=0, keepdims=True), jnp.max(s_new, axis=0, keepdims=True))
        p_past = jnp.exp(s_past - m)
        p_new = jnp.exp(s_new - m)
        l = jnp.sum(p_past, axis=0, keepdims=True) + jnp.sum(p_new, axis=0, keepdims=True)
        probs.append((jnp.concatenate([p_past, p_new], axis=0).astype(BF16), l))

    outs_t = [lax.dot_general(vb_scs[s][...], probs[s][0], TN_DIMS, preferred_element_type=F32)
              for s in seqs]

    for s in seqs:
        o = outs_t[s].T
        l_col = jnp.broadcast_to(probs[s][1], (LANES, LANES)).T
        o = o / jnp.concatenate([l_col] * (width // LANES), axis=1)
        outs = []
        for hd in range(DA_HEADS):
            lo = hd * HEAD_DIM
            r0 = hd * cols_per_head
            if moba:
                outs.append(o[r0:r0 + t_new, lo:lo + HEAD_DIM])
            else:
                dl = dl_ref[...]
                lam = (jnp.exp(jnp.sum(dl[0:1] * dl[1:2], axis=-1, keepdims=True))
                       - jnp.exp(jnp.sum(dl[2:3] * dl[3:4], axis=-1, keepdims=True)) + lam_init)
                oh = o[r0:r0 + t_new, lo:lo + HEAD_DIM] - lam * o[r0 + t_new:r0 + 2 * t_new, lo:lo + HEAD_DIM]
                var = jnp.mean(oh * oh, axis=-1, keepdims=True)
                outs.append(oh * lax.rsqrt(var + RMS_EPS) * nw_ref[...] * (1.0 - lam_init))
        o_ref[s] = jnp.concatenate(outs, axis=1).astype(o_ref.dtype)


def _attn_decode(zs, cache_k, cache_v, page_table, slopes_cols, dl, nw, *, layer, moba, lam_init, spb=SEQS_PER_STEP):
    n_seq, t_new, _ = zs.shape
    page = cache_k.shape[2] // DA_HEADS
    n_pages = page_table.shape[1]
    width = DA_HEADS * HEAD_DIM
    qblk, kblk, vblk = (QB_BLK, KB_BLK, VB_BLK) if moba else (QA_BLK, KA_BLK, VA_BLK)
    past = n_pages * page
    spb = spb if n_seq % spb == 0 else 1

    def page_spec(s, p):
        return pl.BlockSpec((None, None, page * DA_HEADS, HEAD_DIM), lambda b, pt: (layer, pt[b * spb + s, p], 0, 0))

    def z_spec(blk):
        return pl.BlockSpec((spb, t_new, width), lambda b, pt: (b, 0, blk // DA_HEADS))

    kern = functools.partial(_attn_decode_kernel, moba=moba, n_pages=n_pages, page=page, t_new=t_new,
                             lam_init=lam_init, spb=spb)
    page_specs = [page_spec(s, p) for s in range(spb) for p in range(n_pages)]
    grid_spec = pltpu.PrefetchScalarGridSpec(
        num_scalar_prefetch=1,
        grid=(n_seq // spb,),
        in_specs=[
            pl.BlockSpec((1, LANES), lambda b, pt: (0, 0)),
            pl.BlockSpec((4, DA_QK), lambda b, pt: (0, 0)),
            pl.BlockSpec((1, HEAD_DIM), lambda b, pt: (0, 0)),
            z_spec(qblk), z_spec(kblk), z_spec(vblk),
        ] + page_specs * 2,
        out_specs=pl.BlockSpec((spb, t_new, width), lambda b, pt: (b, 0, 0)),
        scratch_shapes=[pltpu.VMEM((past + NEW_ROWS, width), BF16)] * (2 * spb)
        + [pltpu.VMEM((spb, 2, NEW_ROWS, width), F32)],
    )
    return pl.pallas_call(
        kern,
        grid_spec=grid_spec,
        out_shape=jax.ShapeDtypeStruct((n_seq, t_new, width), BF16),
        compiler_params=_cparams(1),
        name="moba_decode" if moba else "diff_decode",
    )(page_table, slopes_cols, dl, nw, zs, zs, zs, *([cache_k] * len(page_specs)), *([cache_v] * len(page_specs)))


def _ssd_decode_kernel(zc_ref, xs_ref, bc_ref, dtx_ref, cst_ref, h0_ref, cw_ref, cb_ref, dtb_ref, alog_ref,
                       dsk_ref, nw_ref, y_ref, hout_ref, cout_ref, xp_sc, *, t_new, spb):
    for s in range(spb):
        _ssd_decode_one(zc_ref.at[s], xs_ref.at[s], bc_ref.at[s], dtx_ref.at[s], cst_ref.at[s], h0_ref.at[s],
                        cw_ref, cb_ref, dtb_ref, alog_ref, dsk_ref, nw_ref, y_ref.at[s], hout_ref.at[s],
                        cout_ref.at[s], xp_sc.at[s], t_new=t_new)


def _ssd_decode_one(zc_ref, xs_ref, bc_ref, dtx_ref, cst_ref, h0_ref, cw_ref, cb_ref, dtb_ref, alog_ref,
                    dsk_ref, nw_ref, y_ref, hout_ref, cout_ref, xp_sc, *, t_new):
    hpg = SSM_HEADS // SSM_GROUPS
    gw = D_SSM // SSM_GROUPS
    xp_sc[0:CONV_W - 1, :] = cst_ref[...]
    xp_sc[CONV_W - 1:CONV_W - 1 + t_new, :D_SSM] = xs_ref[...]
    xp_sc[CONV_W - 1:CONV_W - 1 + t_new, D_SSM:] = bc_ref[...]
    cout_ref[...] = xp_sc[t_new:t_new + CONV_W - 1, :]
    y = cb_ref[...]
    for j in range(CONV_W):
        y = y + xp_sc[j:j + t_new, :] * cw_ref[j:j + 1, :]
    xbc = _silu(y)
    xs = xbc[:, :D_SSM]
    dt = _softplus(dtx_ref[...] + dtb_ref[...])
    da = dt * (-jnp.exp(alog_ref[...]))
    acs = [da[0:1]]
    for s in range(1, t_new):
        acs.append(acs[-1] + da[s:s + 1])
    xdt = xs * dt
    last = acs[-1]
    wx = jnp.concatenate([xdt[s:s + 1] * jnp.exp(last - acs[s]) for s in range(t_new)], axis=0)
    dec = jnp.exp(last)
    d_hi = dec.astype(BF16).astype(F32)
    d_md = (dec - d_hi).astype(BF16).astype(F32)
    d_lo = (dec - d_hi - d_md).astype(BF16).astype(F32)
    dec3 = jnp.concatenate([d_hi, d_md, d_lo, jnp.zeros((5, D_SSM), F32)], axis=0).astype(BF16)
    ones = jnp.ones((8, D_STATE), BF16)
    zc = zc_ref[...]
    for g in range(SSM_GROUPS):
        sl = slice(g * gw, (g + 1) * gw)
        bg = xbc[:, D_SSM + g * D_STATE:D_SSM + (g + 1) * D_STATE]
        cg = xbc[:, D_SSM + (SSM_GROUPS + g) * D_STATE:D_SSM + (SSM_GROUPS + g + 1) * D_STATE]
        h0 = h0_ref[g * hpg:(g + 1) * hpg].reshape(gw, D_STATE)
        yoff = lax.dot_general(cg.astype(BF16), h0.astype(BF16), NT_DIMS, preferred_element_type=F32)
        rows = []
        for t in range(t_new):
            yt = jnp.exp(acs[t][:, sl]) * yoff[t:t + 1]
            for s in range(t + 1):
                gts = jnp.sum(cg[t:t + 1] * bg[s:s + 1], axis=-1, keepdims=True)
                yt = yt + gts * jnp.exp(acs[t][:, sl] - acs[s][:, sl]) * xdt[s:s + 1, sl]
            rows.append(yt)
        yg = jnp.concatenate(rows, axis=0) + dsk_ref[:, sl] * xs[:, sl]
        yg = yg * _silu(zc[:, sl])
        var = jnp.mean(yg * yg, axis=-1, keepdims=True)
        y_ref[:, sl] = (yg * lax.rsqrt(var + RMS_EPS) * nw_ref[:, sl]).astype(y_ref.dtype)
        wxg = wx[:, sl]
        wx_hi = wxg.astype(BF16).astype(F32)
        bg_hi = bg.astype(BF16).astype(F32)
        pad_rows = 16 - 3 * t_new
        lhs = jnp.concatenate([wx_hi, wxg - wx_hi, wx_hi, jnp.zeros((pad_rows, gw), F32)], axis=0).astype(BF16)
        rhs = jnp.concatenate([bg_hi, bg_hi, bg - bg_hi, jnp.zeros((pad_rows, D_STATE), F32)], axis=0).astype(BF16)
        upd = lax.dot_general(lhs, rhs, TN_DIMS, preferred_element_type=F32)
        dec_full = lax.dot_general(dec3[:, sl], ones, TN_DIMS, preferred_element_type=F32)
        hout_ref[g * hpg:(g + 1) * hpg] = (dec_full * h0 + upd).reshape(hpg, SSM_HEADDIM, D_STATE)


def _ssd_decode(zs, dtx, state_conv, state_ssm, cw, cb, dtbx, alogx, dskx, nw, *, layer, spb=4):
    n_seq, t_new, _ = zs.shape
    spb = spb if n_seq % spb == 0 else 1
    kern = functools.partial(_ssd_decode_kernel, t_new=t_new, spb=spb)
    full = lambda shape: pl.BlockSpec(shape, lambda b: (0,) * len(shape))
    return pl.pallas_call(
        kern,
        grid=(n_seq // spb,),
        in_specs=[
            pl.BlockSpec((spb, t_new, D_SSM), lambda b: (b, 0, ZC_COL // D_SSM)),
            pl.BlockSpec((spb, t_new, D_SSM), lambda b: (b, 0, XS_COL // D_SSM)),
            pl.BlockSpec((spb, t_new, 512), lambda b: (b, 0, BC_COL // 512)),
            pl.BlockSpec((spb, t_new, D_SSM), lambda b: (b, 0, 0)),
            pl.BlockSpec((None, spb, CONV_W - 1, D_XBC), lambda b: (layer, b, 0, 0)),
            pl.BlockSpec((None, spb, SSM_HEADS, SSM_HEADDIM, D_STATE), lambda b: (layer, b, 0, 0, 0)),
            full((CONV_W, D_XBC)), full((1, D_XBC)), full((1, D_SSM)), full((1, D_SSM)),
            full((1, D_SSM)), full((1, D_SSM)),
        ],
        out_specs=[
            pl.BlockSpec((spb, t_new, D_SSM), lambda b: (b, 0, 0)),
            pl.BlockSpec((spb, SSM_HEADS, SSM_HEADDIM, D_STATE), lambda b: (b, 0, 0, 0)),
            pl.BlockSpec((spb, CONV_W - 1, D_XBC), lambda b: (b, 0, 0)),
        ],
        out_shape=[
            jax.ShapeDtypeStruct((n_seq, t_new, D_SSM), BF16),
            jax.ShapeDtypeStruct((n_seq, SSM_HEADS, SSM_HEADDIM, D_STATE), F32),
            jax.ShapeDtypeStruct((n_seq, CONV_W - 1, D_XBC), F32),
        ],
        scratch_shapes=[pltpu.VMEM((spb, 8, D_XBC), F32)],
        compiler_params=_cparams(1),
        name="ssd_decode",
    )(zs, zs, zs, dtx, state_conv, state_ssm, cw, cb, dtbx, alogx, dskx, nw)


def _layer_norm(x, w, b):
    mu = jnp.mean(x, axis=-1, keepdims=True)
    xc = x - mu
    var = jnp.mean(xc * xc, axis=-1, keepdims=True)
    return xc * lax.rsqrt(var + LN_EPS) * w + b


def _top2_of(vals, col, n):
    m1 = jnp.max(vals, axis=-1, keepdims=True)
    i1 = jnp.min(jnp.where(vals == m1, col, n), axis=-1, keepdims=True)
    rest = jnp.where(col == i1, -jnp.inf, vals)
    m2 = jnp.max(rest, axis=-1, keepdims=True)
    i2 = jnp.min(jnp.where(rest == m2, col, n), axis=-1, keepdims=True)
    return m1, i1, m2, i2


def _router_gates(logits, b_router):
    scores = jax.nn.sigmoid(logits)
    biased = scores + b_router
    col = lax.broadcasted_iota(jnp.int32, logits.shape, 1)
    grp = col // EXPERTS_PER_GROUP
    best = jnp.zeros((logits.shape[0], 1), jnp.int32)
    best_v = None
    for g in range(N_EXPERT_GROUPS):
        m1, _, m2, _ = _top2_of(jnp.where(grp == g, biased, -jnp.inf), col, N_EXPERTS)
        gs = m1 + m2
        if g == 0:
            best_v = gs
        else:
            upd = gs > best_v
            best = jnp.where(upd, g, best)
            best_v = jnp.where(upd, gs, best_v)
    _, i1, _, i2 = _top2_of(jnp.where(grp == best, biased, -jnp.inf), col, N_EXPERTS)
    w = jnp.where((col == i1) | (col == i2), scores, 0.0)
    return w / jnp.sum(w, axis=-1, keepdims=True)


ROW_CHUNKS = 16


def _store_token_rows(ref, val):
    rows = val.shape[0]
    for c in range(ROW_CHUNKS):
        ref[pl.ds(c, rows, stride=ROW_CHUNKS), :] = val[:, c * LANES:(c + 1) * LANES]


def _load_token_rows(ref, rows):
    return jnp.concatenate([ref[pl.ds(c, rows, stride=ROW_CHUNKS), :] for c in range(ROW_CHUNKS)], axis=1)


def _post1_kernel(x_ref, oa_ref, ob_ref, oc_ref, wa_ref, wb_ref, wc_ref, lw_ref, lb_ref, wr_ref, br_ref,
                  x1r_ref, x1b_ref, g_ref, *, alpha):
    tm = x_ref.shape[0]
    sub = 128
    mixes = []
    for r in range(tm // sub):
        rows = slice(r * sub, (r + 1) * sub)
        mixes.append(jnp.dot(oa_ref[rows, :], wa_ref[...], preferred_element_type=F32)
                     + jnp.dot(ob_ref[rows, :], wb_ref[...], preferred_element_type=F32)
                     + jnp.dot(oc_ref[rows, :], wc_ref[...], preferred_element_type=F32))
    for r in range(tm // sub):
        rows = slice(r * sub, (r + 1) * sub)
        x1 = _layer_norm(alpha * x_ref[rows, :] + mixes[r], lw_ref[...], lb_ref[...])
        _store_token_rows(x1r_ref.at[pl.ds(r * sub * ROW_CHUNKS, sub * ROW_CHUNKS), :], x1)
        x1b_ref[rows, :] = x1.astype(BF16)
        x1_hi = x1.astype(BF16)
        x1_lo = (x1 - x1_hi.astype(F32)).astype(BF16)
        logits = jnp.dot(jnp.concatenate([x1_hi, x1_lo, x1_hi], axis=1), wr_ref[...], preferred_element_type=F32)
        g_ref[rows, :] = _router_gates(logits, br_ref[...])


def _router_weight_passes(w_router):
    wr_hi = w_router.astype(BF16)
    wr_lo = (w_router - wr_hi.astype(F32)).astype(BF16)
    return jnp.concatenate([wr_hi, wr_hi, wr_lo], axis=0)


def _post1(x, oa, ob, oc, w_out_b, lw, lb, w_router3, b_router, *, layer, alpha, tm=512):
    n, d = x.shape
    assert d == ROW_CHUNKS * LANES
    full = lambda shape: pl.BlockSpec(shape, lambda i: (0,) * len(shape))
    row = lambda width: pl.BlockSpec((tm, width), lambda i: (i, 0))
    per_layer = lambda shape: pl.BlockSpec((None,) + shape, lambda i: (layer,) + (0,) * len(shape))
    wa, wb = DA_HEADS * HEAD_DIM, MB_HEADS * HEAD_DIM
    return pl.pallas_call(
        functools.partial(_post1_kernel, alpha=alpha),
        grid=(n // tm,),
        in_specs=[
            row(d), row(wa), row(wb), row(D_SSM),
            pl.BlockSpec((None, wa, d), lambda i: (layer, 0, 0)), pl.BlockSpec((None, wb, d), lambda i: (layer, 1, 0)),
            pl.BlockSpec((None, D_SSM, d), lambda i: (layer, 1, 0)),
            per_layer((1, d)), per_layer((1, d)), full((3 * d, N_EXPERTS)), full((1, N_EXPERTS)),
        ],
        out_specs=[pl.BlockSpec((tm * ROW_CHUNKS, LANES), lambda i: (i, 0)), row(d), row(N_EXPERTS)],
        out_shape=[jax.ShapeDtypeStruct((n * ROW_CHUNKS, LANES), F32), jax.ShapeDtypeStruct((n, d), BF16),
                   jax.ShapeDtypeStruct((n, N_EXPERTS), F32)],
        compiler_params=_cparams(1),
        name="out_proj_ln1_router",
    )(x, oa, ob, oc, w_out_b, w_out_b, w_out_b, lw, lb, w_router3, b_router)


MOE_TILE = 256
EXPERTS_PER_STEP = 2


def _moe_routing(gates, tile):
    n = gates.shape[0]
    n_tiles = n // tile + N_EXPERT_GROUPS
    n_slots = n_tiles * tile
    expert_ids = jnp.arange(N_EXPERTS, dtype=jnp.int32)[None, :]
    gid = jnp.min(jnp.where(gates > 0, expert_ids, N_EXPERTS), axis=1) // EXPERTS_PER_GROUP
    onehot = (gid[:, None] == jnp.arange(N_EXPERT_GROUPS, dtype=jnp.int32)[None, :]).astype(jnp.int32)
    rank = jnp.sum((jnp.cumsum(onehot, axis=0) - onehot) * onehot, axis=1)
    counts = jnp.sum(onehot, axis=0)
    padded = (counts + tile - 1) // tile * tile
    seg_end = jnp.cumsum(padded)
    slot = (seg_end - padded)[gid] + rank
    tok = jnp.arange(n, dtype=jnp.int32)
    slot_ids = jnp.arange(n_slots + tile, dtype=jnp.int32)
    slot_tok = jnp.full((n_slots + tile,), -1, jnp.int32).at[slot].set(tok)
    is_real = slot_tok >= 0
    slot_src = jnp.maximum(slot_tok, 0)
    slot_dst = jnp.where(is_real, slot_tok, n + (slot_ids // tile % 2) * tile + slot_ids % tile)
    grp_gates = jnp.take_along_axis(gates.reshape(n, N_EXPERT_GROUPS, EXPERTS_PER_GROUP),
                                    gid[:, None, None], axis=1)[:, 0]
    slot_gates = jnp.where(is_real[:n_slots, None], grp_gates[slot_src[:n_slots]], 0.0)
    tile_start = jnp.arange(n_tiles, dtype=jnp.int32) * tile
    tile_grp = jnp.minimum(jnp.sum((tile_start[:, None] >= seg_end[None, :]).astype(jnp.int32), axis=1),
                           N_EXPERT_GROUPS - 1)
    return tile_grp, slot_src, slot_dst, slot_gates


def _moe_kernel(tile_grp_ref, src_ref, dst_ref, x_hbm, g_ref, wg_ref, wu_ref, wd_ref, ff_hbm,
                xg_sc, x_sc, acc_sc, ys_sc, gsem, ssem, *, tile, n_tok):
    del tile_grp_ref
    i = pl.program_id(0)
    e = pl.program_id(1)
    n_tiles = pl.num_programs(0)
    n_e = EXPERTS_PER_GROUP
    slab = ROW_CHUNKS

    def gather_copy(slot, row, buf):
        tok = src_ref[slot]
        return pltpu.make_async_copy(x_hbm.at[pl.ds(tok * slab, slab), :],
                                     xg_sc.at[buf, pl.ds(row * slab, slab), :], gsem.at[buf])

    def scatter_copy(slot, row, buf):
        tok = dst_ref[slot]
        return pltpu.make_async_copy(ys_sc.at[buf, pl.ds(row * slab, slab), :],
                                     ff_hbm.at[pl.ds(tok * slab, slab), :], ssem.at[buf])

    def wait_all(sc, sem, buf):
        pltpu.make_async_copy(sc.at[buf], sc.at[buf], sem.at[buf]).wait()

    cur = i % 2
    nxt = (i + 1) % 2
    n_steps = n_e // EXPERTS_PER_STEP

    @pl.when((i == 0) & (e == 0))
    def _():
        for j in range(tile):
            gather_copy(j, j, 0).start(priority=j % 2)
        ys_sc[...] = jnp.zeros(ys_sc.shape, F32)
        for b in range(2):
            fill = pltpu.make_async_copy(
                ys_sc.at[b], ff_hbm.at[pl.ds((n_tok + b * tile) * slab, tile * slab), :], ssem.at[b])
            fill.start()
            fill.wait()

    part_rows = tile // n_steps
    for j in range(part_rows):
        row = e * part_rows + j
        gather_copy((i + 1) * tile + row, row, nxt).start(priority=j % 2)

    @pl.when(e == 0)
    def _():
        wait_all(xg_sc, gsem, cur)
        for c in range(ROW_CHUNKS):
            x_sc[:, c * LANES:(c + 1) * LANES] = xg_sc[cur, pl.ds(c, tile, stride=ROW_CHUNKS), :].astype(BF16)

    x = x_sc[...]
    col = lax.broadcasted_iota(jnp.int32, g_ref.shape, 1)
    hids = []
    for u in range(EXPERTS_PER_STEP):
        hid = _silu(jnp.dot(x, wg_ref[u], preferred_element_type=F32)) * jnp.dot(x, wu_ref[u],
                                                                                 preferred_element_type=F32)
        gate = jnp.sum(jnp.where(col == e * EXPERTS_PER_STEP + u, g_ref[...], 0.0), axis=-1, keepdims=True)
        hids.append((hid * gate).astype(BF16))
    part = sum(jnp.dot(hids[u], wd_ref[u], preferred_element_type=F32) for u in range(EXPERTS_PER_STEP))

    @pl.when(e == 0)
    def _():
        acc_sc[...] = part

    @pl.when(e > 0)
    def _():
        acc_sc[...] += part

    @pl.when(e == n_steps - 1)
    def _():
        @pl.when(i >= 2)
        def _():
            wait_all(ys_sc, ssem, cur)

        y = acc_sc[...]
        for c in range(ROW_CHUNKS):
            ys_sc[cur, pl.ds(c, tile, stride=ROW_CHUNKS), :] = y[:, c * LANES:(c + 1) * LANES]
        for j in range(tile):
            scatter_copy(i * tile + j, j, cur).start(priority=j % 2)

        @pl.when(i == n_tiles - 1)
        def _():
            wait_all(ys_sc, ssem, cur)

            @pl.when(n_tiles >= 2)
            def _():
                wait_all(ys_sc, ssem, nxt)

            wait_all(xg_sc, gsem, nxt)


def _moe(x1r, gates, wg, wu, wd, *, layer=0, tile=MOE_TILE):
    n = gates.shape[0]
    d, f = wg.shape[1], wg.shape[2]
    e0 = layer * N_EXPERTS
    tile_grp, slot_src, slot_dst, slot_gates = _moe_routing(gates, tile)
    n_tiles = tile_grp.shape[0]
    n_rows = n + 2 * tile
    n_e = EXPERTS_PER_GROUP
    eps = EXPERTS_PER_STEP
    n_steps = n_e // eps
    w_idx = lambda i, e, tg, s, t: ((e0 + tg[i] * n_e) // eps + e, 0, 0)
    grid_spec = pltpu.PrefetchScalarGridSpec(
        num_scalar_prefetch=3,
        grid=(n_tiles, n_steps),
        in_specs=[
            pl.BlockSpec(memory_space=pl.ANY),
            pl.BlockSpec((tile, n_e), lambda i, e, tg, s, t: (i, 0)),
            pl.BlockSpec((eps, d, f), w_idx), pl.BlockSpec((eps, d, f), w_idx), pl.BlockSpec((eps, f, d), w_idx),
        ],
        out_specs=pl.BlockSpec(memory_space=pl.ANY),
        scratch_shapes=[
            pltpu.VMEM((2, tile * ROW_CHUNKS, LANES), F32), pltpu.VMEM((tile, d), BF16),
            pltpu.VMEM((tile, d), F32), pltpu.VMEM((2, tile * ROW_CHUNKS, LANES), F32),
            pltpu.SemaphoreType.DMA((2,)), pltpu.SemaphoreType.DMA((2,)),
        ],
    )
    return pl.pallas_call(
        functools.partial(_moe_kernel, tile=tile, n_tok=n),
        grid_spec=grid_spec,
        out_shape=jax.ShapeDtypeStruct((n_rows * ROW_CHUNKS, LANES), F32),
        compiler_params=_cparams(2),
        name="moe_grouped",
    )(tile_grp, slot_src, slot_dst, x1r, slot_gates, wg, wu, wd)


def _post2_kernel(x1r_ref, x1b_ref, ffr_ref, p_ref, wgate_ref, wproj_ref, lw_ref, lb_ref, out_a_ref, out_b_ref, *,
                  alpha, head_tiles):
    i = pl.program_id(0)
    tm = x1b_ref.shape[0]
    sub = 128
    gates, projs = [], []
    for r in range(tm // sub):
        rows = slice(r * sub, (r + 1) * sub)
        gates.append(jnp.dot(x1b_ref[rows, :], wgate_ref[...], preferred_element_type=F32))
        projs.append(jnp.dot(p_ref[rows, :], wproj_ref[...], preferred_element_type=F32))
    for r in range(tm // sub):
        rows = slice(r * sub, (r + 1) * sub)
        slab = pl.ds(r * sub * ROW_CHUNKS, sub * ROW_CHUNKS)
        ple = jax.nn.sigmoid(gates[r]) * projs[r]
        x2 = _layer_norm(alpha * _load_token_rows(x1r_ref.at[slab, :], sub)
                         + _load_token_rows(ffr_ref.at[slab, :], sub) + ple, lw_ref[...], lb_ref[...])
        if head_tiles is None:
            out_a_ref[rows, :] = x2
            out_b_ref[rows, :] = x2.astype(BF16)
        else:
            @pl.when(i < head_tiles)
            def _():
                out_a_ref[rows, :] = x2

            @pl.when(i >= head_tiles)
            def _():
                out_b_ref[rows, :] = x2


def _post2(x1r, x1b, ffr, p, w_gate_b, w_proj_b, lw, lb, *, layer, alpha, split_at=None, tm=512):
    n, d = x1b.shape
    dp = p.shape[1]
    per_layer = lambda shape: pl.BlockSpec((None,) + shape, lambda i: (layer,) + (0,) * len(shape))
    row = lambda width: pl.BlockSpec((tm, width), lambda i: (i, 0))
    slab = pl.BlockSpec((tm * ROW_CHUNKS, LANES), lambda i: (i, 0))
    if split_at is None:
        head_tiles = None
        out_specs = [row(d), row(d)]
        out_shape = [jax.ShapeDtypeStruct((n, d), F32), jax.ShapeDtypeStruct((n, d), BF16)]
    else:
        assert split_at % tm == 0 and (n - split_at) % tm == 0
        head_tiles = split_at // tm
        out_specs = [pl.BlockSpec((tm, d), lambda i: (jnp.minimum(i, head_tiles - 1), 0)),
                     pl.BlockSpec((tm, d), lambda i: (jnp.maximum(i - head_tiles, 0), 0))]
        out_shape = [jax.ShapeDtypeStruct((split_at, d), F32), jax.ShapeDtypeStruct((n - split_at, d), F32)]
    return pl.pallas_call(
        functools.partial(_post2_kernel, alpha=alpha, head_tiles=head_tiles),
        grid=(n // tm,),
        in_specs=[slab, row(d), slab, row(dp), per_layer((d, d)), per_layer((dp, d)), per_layer((1, d)),
                  per_layer((1, d))],
        out_specs=out_specs,
        out_shape=out_shape,
        compiler_params=_cparams(1),
        name="ple_ln2",
    )(x1r, x1b, ffr, p, w_gate_b, w_proj_b, lw, lb)


def kernel(x_prompt, x_sample, p_prompt, p_sample, cache_diff_k, cache_diff_v, cache_moba_k, cache_moba_v, state_ssm, state_conv, page_table, w_in, w_out, diff_lambda, diff_norm_w, ssm_conv_w, ssm_conv_b, ssm_dt_bias, ssm_a_log, ssm_d, ssm_norm_w, ln1_w, ln1_b, ln2_w, ln2_b, w_router, b_router, w_exp_gate, w_exp_up, w_exp_down, w_ple_proj, w_ple_gate):
    depth = w_in.shape[0]
    batch, seq, d_model = x_prompt.shape
    n_seq, t_new, _ = x_sample.shape
    n_p, n_s = batch * seq, n_seq * t_new
    alpha = (2 * depth) ** 0.25
    n_heads = DA_HEADS + MB_HEADS
    slopes = jnp.asarray(2.0 ** (-8.0 * np.arange(1, n_heads + 1) / n_heads), dtype=F32)

    def score_col_slopes(sl, cols_per_head):
        v = jnp.repeat(sl, cols_per_head)
        return jnp.pad(v, (0, LANES - v.shape[0])).reshape(1, LANES)

    slopes_diff_cols = score_col_slopes(slopes[:DA_HEADS], 2 * t_new)
    slopes_moba_cols = score_col_slopes(slopes[DA_HEADS:], t_new)
    pad_lanes = lambda v: jnp.pad(v, (0, LANES - v.shape[0])).reshape(1, LANES)
    per_channel = lambda v: jnp.repeat(v, SSM_HEADDIM).reshape(1, D_SSM)

    n_pool, page = cache_diff_k.shape[1], cache_diff_k.shape[2]
    flat_cache = lambda c: c.reshape(depth, n_pool, page * DA_HEADS, HEAD_DIM)
    ck_a, cv_a, ck_b, cv_b = map(flat_cache, (cache_diff_k, cache_diff_v, cache_moba_k, cache_moba_v))

    w_in_b = jnp.pad(w_in.astype(BF16), ((0, 0), (0, 0), (0, D_IN_PAD - D_IN_PROJ)))
    w_dtx = jnp.repeat(w_in[:, :, DT_COL:DT_COL + SSM_HEADS].astype(BF16), SSM_HEADDIM, axis=2)
    w_out_b = w_out.astype(BF16)
    w_gate_b, w_proj_b = w_ple_gate.astype(BF16), w_ple_proj.astype(BF16)
    d_exp = w_exp_gate.shape[-1]
    wg_b = w_exp_gate.astype(BF16).reshape(depth * N_EXPERTS, d_model, d_exp)
    wu_b = w_exp_up.astype(BF16).reshape(depth * N_EXPERTS, d_model, d_exp)
    wd_b = w_exp_down.astype(BF16).reshape(depth * N_EXPERTS, d_exp, d_model)
    w_router3 = _router_weight_passes(w_router)
    b_router2 = b_router.reshape(1, N_EXPERTS)
    ln_rows = lambda v: v.reshape(depth, 1, d_model)
    ln1_w3, ln1_b3, ln2_w3, ln2_b3 = map(ln_rows, (ln1_w, ln1_b, ln2_w, ln2_b))

    x = jnp.concatenate([x_prompt.reshape(n_p, d_model), x_sample.reshape(n_s, d_model)], axis=0)
    xb = x.astype(BF16)
    outs_p = [[] for _ in range(6)]
    outs_s = [[] for _ in range(6)]
    for i in range(depth):
        lam_init = 0.8 - 0.6 * math.exp(-0.3 * i)
        z = _matmul(xb, w_in_b, i, tm=512, tn=D_IN_PAD // 3)
        zs = z[n_p:].reshape(n_seq, t_new, D_IN_PAD)
        dtx = _matmul(xb[n_p:], w_dtx, i, tm=n_s, tn=D_SSM).reshape(n_seq, t_new, D_SSM)

        dl = diff_lambda[i]
        nw_a = diff_norm_w[i].reshape(1, HEAD_DIM)
        cw, cb = ssm_conv_w[i], ssm_conv_b[i].reshape(1, D_XBC)
        nw_c = ssm_norm_w[i].reshape(1, D_SSM)

        ka, vt, kmean, *kv_new_p = _kv_prep(z, batch=batch, seq=seq)
        oa_p = _diff_prompt(z, ka, vt, slopes, dl, nw_a, batch=batch, seq=seq, lam_init=lam_init)
        ob_p = _moba_prompt(z, ka, vt, kmean, slopes, batch=batch, seq=seq)
        oc_p, h_p = _ssd_prompt(z, cw, cb, pad_lanes(ssm_dt_bias[i]), pad_lanes(ssm_a_log[i]),
                                per_channel(ssm_d[i]), nw_c, batch=batch, seq=seq)
        oa_s = _attn_decode(zs, ck_a, cv_a, page_table, slopes_diff_cols, dl, nw_a, layer=i, moba=False,
                            lam_init=lam_init)
        ob_s = _attn_decode(zs, ck_b, cv_b, page_table, slopes_moba_cols, dl, nw_a, layer=i, moba=True,
                            lam_init=lam_init)
        oc_s, h_s, conv_s = _ssd_decode(zs, dtx, state_conv, state_ssm, cw, cb, per_channel(ssm_dt_bias[i]),
                                        per_channel(ssm_a_log[i]), per_channel(ssm_d[i]), nw_c, layer=i)

        head_shape = lambda a, lead: a.reshape(*lead, DA_HEADS, HEAD_DIM)
        for j, blk in enumerate((KA_BLK, VA_BLK, KB_BLK, VB_BLK)):
            c0 = blk * LANES
            outs_p[j].append(kv_new_p[j])
            outs_s[j].append(head_shape(zs[:, :, c0:c0 + DA_HEADS * HEAD_DIM], (n_seq, t_new)))
        outs_p[4].append(h_p)
        outs_p[5].append(jnp.stack([z[(b + 1) * seq - (CONV_W - 1):(b + 1) * seq, XS_COL:XS_COL + D_XBC]
                                    for b in range(batch)]))
        outs_s[4].append(h_s)
        outs_s[5].append(conv_s)

        oa = jnp.concatenate([oa_p, oa_s.reshape(n_s, -1)], axis=0)
        ob = jnp.concatenate([ob_p, ob_s.reshape(n_s, -1)], axis=0)
        oc = jnp.concatenate([oc_p, oc_s.reshape(n_s, -1)], axis=0)
        x1r, x1b, gates = _post1(x, oa, ob, oc, w_out_b, ln1_w3, ln1_b3, w_router3, b_router2, layer=i, alpha=alpha)
        ffr = _moe(x1r, gates, wg_b, wu_b, wd_b, layer=i)
        p = jnp.concatenate([p_prompt[i].reshape(n_p, -1), p_sample[i].reshape(n_s, -1)], axis=0).astype(BF16)
        last = i == depth - 1
        x, xb = _post2(x1r, x1b, ffr, p, w_gate_b, w_proj_b, ln2_w3, ln2_b3, layer=i, alpha=alpha,
                       split_at=n_p if last else None, tm=256 if last else 512)

    y_p = x.reshape(batch, seq, d_model)
    y_s = xb.reshape(n_seq, t_new, d_model)
    return (y_p, y_s, *[jnp.stack(o) for o in outs_p], *[jnp.stack(o) for o in outs_s])
```
